```python
import math
import jax
import jax.numpy as jnp
from jax import lax
import numpy as np

D_MODEL = 1024
BATCH = 8
SEQ = 8192
DEPTH = 2

GRID_W = 64
CTX_LEN = 256
N_MIXERS = 2
N_HEADS = 8
HEAD_DIM = D_MODEL // N_HEADS
CHUNK = 64
N_GROUPS = 4
EXPERTS_PER_GROUP = 8
N_EXPERTS = N_GROUPS * EXPERTS_PER_GROUP
TOP_K = 2
D_EXPERT = D_MODEL // 2
MOE_BLOCK = 128
N_MOD = 6
EPS = 1e-6

kernel_name = 'hybrid_deltanet_shortconv_hmoe_dit'


def rmsnorm(x, g):
    xf = x.astype(jnp.float32)
    y = xf * lax.rsqrt(jnp.mean(xf * xf, axis=-1, keepdims=True) + EPS)
    return (y * g.astype(jnp.float32)).astype(x.dtype)


def modulate(x, g, shift, scale):
    return rmsnorm(x, g) * (1 + scale) + shift


def ada_params(cond, w, b):
    m = jax.nn.silu(cond) @ w + b
    return jnp.split(m[..., None, :], N_MOD, axis=-1)


def conv3_seq(x, w):
    xp = jnp.pad(x, ((0, 0), (1, 1), (0, 0)))
    return xp[:, :-2] * w[0] + xp[:, 1:-1] * w[1] + xp[:, 2:] * w[2]


def conv3_grid(x, w, rows):
    b, s, ch = x.shape
    half = ch // 2
    g = x.reshape(b, rows, GRID_W, ch)
    wh, wv = w[:, :half], w[:, half:]
    ph = jnp.pad(g[..., :half], ((0, 0), (0, 0), (1, 1), (0, 0)))
    yh = ph[:, :, :-2] * wh[0] + ph[:, :, 1:-1] * wh[1] + ph[:, :, 2:] * wh[2]
    pv = jnp.pad(g[..., half:], ((0, 0), (1, 1), (0, 0), (0, 0)))
    yv = pv[:, :-2] * wv[0] + pv[:, 1:-1] * wv[1] + pv[:, 2:] * wv[2]
    return jnp.concatenate([yh, yv], axis=-1).reshape(b, s, ch)


def l2norm(t):
    return t * lax.rsqrt(jnp.sum(t * t, axis=-1, keepdims=True) + EPS)


def to_chunks(t):
    b, l, h = t.shape[:3]
    t = t.reshape(b, l // CHUNK, CHUNK, h, *t.shape[3:])
    return jnp.moveaxis(t, (1, 3), (0, 2))


def from_chunks(t):
    t = jnp.moveaxis(t, (0, 2), (1, 3))
    b, n, c, h, d = t.shape
    return t.reshape(b, n * c, h, d)


def gated_delta_scan(q, k, v, g, beta, s0, with_output):
    qc, kc, vc = to_chunks(q), to_chunks(k), to_chunks(v)
    gc = jnp.cumsum(to_chunks(g), axis=-1)
    bc = to_chunks(beta)
    pos = jnp.arange(CHUNK)
    incl = pos[:, None] >= pos[None, :]
    decay = jnp.exp(jnp.where(incl, gc[..., :, None] - gc[..., None, :], -jnp.inf))
    kb = kc * bc[..., None]
    lmat = jnp.where(pos[:, None] > pos[None, :],
                     jnp.einsum('nbhid,nbhjd->nbhij', kb, kc) * decay, 0.0)
    rhs = jnp.concatenate([vc * bc[..., None], kb * jnp.exp(gc)[..., None]], axis=-1)
    sol = lax.linalg.triangular_solve(lmat, rhs, left_side=True, lower=True, unit_diagonal=True)
    dv = vc.shape[-1]
    u, w = sol[..., :dv], sol[..., dv:]
    k_dec = kc * jnp.exp(gc[..., -1:] - gc)[..., None]
    g_last = jnp.exp(gc[..., -1])
    xs = (u, w, k_dec, g_last)
    if with_output:
        xs = xs + (qc * jnp.exp(gc)[..., None], jnp.einsum('nbhid,nbhjd->nbhij', qc, kc) * decay)

    def step(s, inp):
        u_i, w_i, kd_i, gl_i = inp[:4]
        v_new = u_i - jnp.einsum('bhcd,bhde->bhce', w_i, s)
        s_next = s * gl_i[..., None, None] + jnp.einsum('bhcd,bhce->bhde', kd_i, v_new)
        if with_output:
            qd_i, qk_i = inp[4:]
            o = jnp.einsum('bhcd,bhde->bhce', qd_i, s) + jnp.einsum('bhij,bhje->bhie', qk_i, v_new)
            return s_next, o
        return s_next, None

    s_final, o = lax.scan(step, s0, xs)
    return s_final, (from_chunks(o) if with_output else None)


def deltanet_mixer(h_lat, h_ctx, w_in, conv_w, a_log, dt_bias, onorm, w_out, ctx_out):
    d, nh, f32 = D_MODEL, N_HEADS, jnp.float32

    def project(h):
        b, l, _ = h.shape
        p = h @ w_in
        qkv = jax.nn.silu(conv3_seq(p[..., :3 * d], conv_w))
        q, k, v = (t.reshape(b, l, nh, HEAD_DIM).astype(f32) for t in jnp.split(qkv, 3, axis=-1))
        q = l2norm(q) * HEAD_DIM ** -0.5
        k = l2norm(k)
        z = p[..., 3 * d:4 * d]
        beta = jax.nn.sigmoid(p[..., 4 * d:4 * d + 2 * nh].astype(f32)).reshape(b, l, 2, nh)
        a_in = p[..., 4 * d + 2 * nh:].astype(f32).reshape(b, l, 2, nh)
        g = -jnp.exp(a_log.astype(f32)) * jax.nn.softplus(a_in + dt_bias.astype(f32))
        return q, k, v, z, beta, g

    def bidir(q, k, v, beta, g, s_f, s_b, with_output):
        flip = lambda t: t[:, ::-1]
        rf = gated_delta_scan(q, k, v, g[:, :, 0], beta[:, :, 0], s_f, with_output)
        rb = gated_delta_scan(flip(q), flip(k), flip(v), flip(g[:, :, 1]), flip(beta[:, :, 1]),
                              s_b, with_output)
        return rf, rb

    def finish(o, z):
        b, l = z.shape[:2]
        o = o * lax.rsqrt(jnp.mean(o * o, axis=-1, keepdims=True) + EPS) * onorm.astype(f32)
        y = o.astype(z.dtype) * jax.nn.silu(z).reshape(b, l, nh, HEAD_DIM)
        return y.reshape(b, l, d) @ w_out

    qc, kc, vc, zc, bc, gc = project(h_ctx)
    s_zero = jnp.zeros((h_ctx.shape[0], nh, HEAD_DIM, HEAD_DIM), f32)
    (sf, ocf), (sb, ocb) = bidir(qc, kc, vc, bc, gc, s_zero, s_zero, ctx_out)
    y_ctx = finish(ocf + ocb[:, ::-1], zc) if ctx_out else None
    ql, kl, vl, zl, bl, gl = project(h_lat)
    (_, olf), (_, olb) = bidir(ql, kl, vl, bl, gl, sf, sb, True)
    y_lat = finish(olf + olb[:, ::-1], zl)
    return y_lat, y_ctx


def shortconv_mixer(h, w_in, conv_w, w_out, rows):
    gate_b, gate_c, hx = jnp.split(h @ w_in, 3, axis=-1)
    u = gate_c * hx
    y = conv3_grid(u, conv_w, rows) if rows is not None else conv3_seq(u, conv_w)
    return (gate_b * y) @ w_out


def hier_moe(h, router_g, router_e, w1, w3, w2):
    f32 = jnp.float32
    t, d = h.shape
    hf = h.astype(f32)
    pg = jax.nn.softmax(hf @ router_g.astype(f32), axis=-1)
    gi = jnp.argmax(pg, axis=-1).astype(jnp.int32)
    pg_sel = jnp.take_along_axis(pg, gi[:, None], axis=-1)
    le = (hf @ router_e.astype(f32)).reshape(t, N_GROUPS, EXPERTS_PER_GROUP)
    le_sel = jnp.take_along_axis(le, gi[:, None, None], axis=1)[:, 0]
    pk, ik = lax.top_k(jax.nn.softmax(le_sel, axis=-1), TOP_K)
    wts = pg_sel * pk / jnp.sum(pk, axis=-1, keepdims=True)
    eid = gi[:, None] * EXPERTS_PER_GROUP + ik.astype(jnp.int32)

    n = t * TOP_K
    n_blocks = -(-n // MOE_BLOCK) + N_EXPERTS
    cap = n_blocks * MOE_BLOCK
    flat_e = eid.reshape(-1)
    flat_tok = jnp.repeat(jnp.arange(t, dtype=jnp.int32), TOP_K)
    flat_w = wts.reshape(-1)
    order = jnp.argsort(flat_e)
    se = flat_e[order]
    counts = jnp.zeros((N_EXPERTS,), jnp.int32).at[flat_e].add(1)
    padded = (counts + MOE_BLOCK - 1) // MOE_BLOCK * MOE_BLOCK
    ends_pad = jnp.cumsum(padded)
    start_pad = ends_pad - padded
    start_raw = jnp.cumsum(counts) - counts
    dest = start_pad[se] + jnp.arange(n, dtype=jnp.int32) - start_raw[se]
    buf_tok = jnp.full((cap,), t, jnp.int32).at[dest].set(flat_tok[order])
    buf_w = jnp.zeros((cap,), f32).at[dest].set(flat_w[order])
    block_e = jnp.minimum(
        jnp.searchsorted(ends_pad, jnp.arange(n_blocks, dtype=jnp.int32) * MOE_BLOCK, side='right'),
        N_EXPERTS - 1).astype(jnp.int32)
    h_pad = jnp.concatenate([h, jnp.zeros((1, d), h.dtype)], axis=0)
    xb = h_pad[buf_tok].reshape(n_blocks, MOE_BLOCK, d)

    def expert_block(args):
        xi, e = args
        return (jax.nn.silu(xi @ w1[e]) * (xi @ w3[e])) @ w2[e]

    yb = lax.map(expert_block, (xb, block_e)).reshape(cap, d)
    y = jax.ops.segment_sum(yb * buf_w[:, None].astype(yb.dtype), buf_tok, num_segments=t + 1)
    return y[:t]


def setup_inputs(seed: int = 0) -> dict:
    key = jax.random.key(seed)
    ks = jax.random.split(key, 24)
    d, nh, f32 = D_MODEL, N_HEADS, jnp.float32
    n_a = (DEPTH + N_MIXERS - 1) // N_MIXERS
    n_b = DEPTH // N_MIXERS

    def nrm(k, shape, scale):
        return jax.random.normal(k, shape, f32) * scale

    dt = jnp.exp(jax.random.uniform(ks[10], (n_a, 2, nh), f32, math.log(1e-3), math.log(1e-1)))
    return {
        'x': nrm(ks[0], (BATCH, SEQ, d), 1.0),
        'c': nrm(ks[1], (BATCH, d), 1.0),
        'ctx': nrm(ks[2], (BATCH, CTX_LEN, d), 1.0),
        'c_ctx': nrm(ks[3], (d,), 1.0),
        'ada_w': nrm(ks[4], (DEPTH, d, N_MOD * d), 0.5 * d ** -0.5),
        'ada_b': nrm(ks[5], (DEPTH, N_MOD * d), 0.02),
        'norm_mix': 1.0 + nrm(ks[6], (DEPTH, d), 0.05),
        'norm_ffn': 1.0 + nrm(ks[7], (DEPTH, d), 0.05),
        'w_in_a': nrm(ks[8], (n_a, d, 4 * d + 4 * nh), d ** -0.5),
        'conv_a': nrm(ks[9], (n_a, 3, 3 * d), 3 ** -0.5),
        'a_log_a': jnp.log(jax.random.uniform(ks[11], (n_a, 2, nh), f32, 1.0, 16.0)),
        'dt_bias_a': dt + jnp.log(-jnp.expm1(-dt)),
        'onorm_a': 1.0 + nrm(ks[12], (n_a, HEAD_DIM), 0.05),
        'w_out_a': nrm(ks[13], (n_a, d, d), d ** -0.5),
        'w_in_b': nrm(ks[14], (n_b, d, 3 * d), d ** -0.5),
        'conv_b': nrm(ks[15], (n_b, 3, d), 3 ** -0.5),
        'w_out_b': nrm(ks[16], (n_b, d, d), d ** -0.5),
        'router_g': nrm(ks[17], (DEPTH, d, N_GROUPS), d ** -0.5),
        'router_e': nrm(ks[18], (DEPTH, d, N_EXPERTS), d ** -0.5),
        'w1': nrm(ks[19], (DEPTH, N_EXPERTS, d, D_EXPERT), d ** -0.5),
        'w3': nrm(ks[20], (DEPTH, N_EXPERTS, d, D_EXPERT), d ** -0.5),
        'w2': nrm(ks[21], (DEPTH, N_EXPERTS, D_EXPERT, d), D_EXPERT ** -0.5),
        'final_norm': 1.0 + nrm(ks[22], (d,), 0.05),
    }


def reference(x, c, ctx, c_ctx, ada_w, ada_b, norm_mix, norm_ffn,
              w_in_a, conv_a, a_log_a, dt_bias_a, onorm_a, w_out_a,
              w_in_b, conv_b, w_out_b, router_g, router_e, w1, w3, w2, final_norm):
    b, s, d = x.shape
    rows = s // GRID_W
    kinds = [i % N_MIXERS for i in range(DEPTH)]
    for i in range(DEPTH):
        kind, j = kinds[i], i // N_MIXERS
        ctx_live = 0 in kinds[i + 1:]
        sh_m, sc_m, gt_m, sh_f, sc_f, gt_f = ada_params(c, ada_w[i], ada_b[i])
        if kind == 0 or ctx_live:
            csh_m, csc_m, cgt_m, csh_f, csc_f, cgt_f = ada_params(c_ctx, ada_w[i], ada_b[i])
            hc = modulate(ctx, norm_mix[i], csh_m, csc_m)
        h = modulate(x, norm_mix[i], sh_m, sc_m)
        if kind == 0:
            y, yc = deltanet_mixer(h, hc, w_in_a[j], conv_a[j], a_log_a[j], dt_bias_a[j],
                                   onorm_a[j], w_out_a[j], ctx_live)
        else:
            y = shortconv_mixer(h, w_in_b[j], conv_b[j], w_out_b[j], rows)
            yc = shortconv_mixer(hc, w_in_b[j], conv_b[j], w_out_b[j], None) if ctx_live else None
        x = x + gt_m * y
        hf = modulate(x, norm_ffn[i], sh_f, sc_f).reshape(b * s, d)
        if ctx_live:
            ctx = ctx + cgt_m * yc
            hcf = modulate(ctx, norm_ffn[i], csh_f, csc_f).reshape(-1, d)
            out = hier_moe(jnp.concatenate([hf, hcf], axis=0), router_g[i], router_e[i],
                           w1[i], w3[i], w2[i])
            x = x + gt_f * out[:b * s].reshape(b, s, d)
            ctx = ctx + cgt_f * out[b * s:].reshape(ctx.shape)
        else:
            x = x + gt_f * hier_moe(hf, router_g[i], router_e[i], w1[i], w3[i], w2[i]).reshape(b, s, d)
    return rmsnorm(x, final_norm)
```

```python
import functools

import jax
import jax.numpy as jnp
from jax import lax
from jax.experimental import pallas as pl
from jax.experimental.pallas import tpu as pltpu

F32 = jnp.float32
BF16 = jnp.bfloat16
I32 = jnp.int32
HIGHEST = lax.Precision.HIGHEST

EPS = 1e-6
N_HEADS = 8
HEAD_DIM = 128
GRID_W = 64
N_GROUPS = 4
EXPERTS_PER_GROUP = 8
N_EXPERTS = N_GROUPS * EXPERTS_PER_GROUP
N_MOD = 6
LANES = 128
SEQ_HALO = 8
VMEM_LIMIT = 48 * 1024 * 1024


def _params(*sem):
    return pltpu.CompilerParams(dimension_semantics=sem, vmem_limit_bytes=VMEM_LIMIT)


def _dot(a, b, precision=None):
    return jnp.dot(a, b, preferred_element_type=F32, precision=precision)


def _dot_nt(a, b, precision=None):
    return lax.dot_general(a, b, (((1,), (1,)), ((), ())), preferred_element_type=F32, precision=precision)


def _dot_tn(a, b, precision=None):
    return lax.dot_general(a, b, (((0,), (0,)), ((), ())), preferred_element_type=F32, precision=precision)


def _silu(x):
    return x * jax.nn.sigmoid(x)


def _rms_mod(x, g, shift, scale):
    ms = jnp.mean(x * x, axis=-1, keepdims=True)
    return x * lax.rsqrt(ms + EPS) * g * (1.0 + scale) + shift


def _ada_kernel(c_ref, w_ref, b_ref, o_ref):
    o_ref[0] = _dot(_silu(c_ref[...]), w_ref[0], HIGHEST) + b_ref[0]


def _ada_params(cond, ada_w, ada_b):
    depth, d, n = ada_w.shape
    tn = 512
    r = cond.shape[0]
    return pl.pallas_call(
        _ada_kernel,
        grid=(depth, n // tn),
        in_specs=[pl.BlockSpec((r, d), lambda l, j: (0, 0)),
                  pl.BlockSpec((1, d, tn), lambda l, j: (l, 0, j)),
                  pl.BlockSpec((1, 1, tn), lambda l, j: (l, 0, j))],
        out_specs=pl.BlockSpec((1, r, tn), lambda l, j: (l, 0, j)),
        out_shape=jax.ShapeDtypeStruct((depth, r, n), F32),
        compiler_params=_params("parallel", "parallel"),
    )(cond, ada_w, ada_b.reshape(depth, 1, n))


def _delta_in_kernel(xm_ref, xp_ref, xn_ref, sh_ref, sc_ref, g_ref, wqkv_ref, cw_ref, wz_ref, wg_ref, gp_ref,
                     q_ref, k_ref, v_ref, z_ref, gate_ref, *, tm, ck):
    i = pl.program_id(1)
    n = pl.num_programs(1)
    g, sh, sc = g_ref[...], sh_ref[0], sc_ref[0]
    hm = _rms_mod(xm_ref[0], g, sh, sc)
    hp = _rms_mod(xp_ref[0], g, sh, sc) * (i > 0).astype(F32)
    hn = _rms_mod(xn_ref[0], g, sh, sc) * (i < n - 1).astype(F32)
    hm16 = hm.astype(BF16)
    hext = jnp.concatenate([hp, hm, hn], axis=0).astype(BF16)
    rows = tm + 2 * SEQ_HALO
    d = hm.shape[1]
    outs = (q_ref, k_ref, v_ref)
    for j in range(3 * d // ck):
        p = _dot(hext, wqkv_ref[:, j * ck:(j + 1) * ck])
        cw = cw_ref[:, j * ck:(j + 1) * ck]
        y = cw[0:1] * pltpu.roll(p, 1, 0) + cw[1:2] * p + cw[2:3] * pltpu.roll(p, rows - 1, 0)
        y = _silu(y[SEQ_HALO:SEQ_HALO + tm])
        which, col0 = (j * ck) // d, (j * ck) % d
        for hh in range(ck // HEAD_DIM):
            yh = y[:, hh * HEAD_DIM:(hh + 1) * HEAD_DIM]
            if which < 2:
                yh = yh * lax.rsqrt(jnp.sum(yh * yh, axis=-1, keepdims=True) + EPS)
            if which == 0:
                yh = yh * HEAD_DIM ** -0.5
            c0 = col0 + hh * HEAD_DIM
            outs[which][0, :, c0:c0 + HEAD_DIM] = yh.astype(q_ref.dtype)
    z_ref[0] = _dot(hm16, wz_ref[...]).astype(z_ref.dtype)
    pg = _dot(hm16, wg_ref[...])
    lane = lax.broadcasted_iota(I32, pg.shape, 1) % LANES
    xg = pg + gp_ref[1:2]
    softplus = jnp.maximum(xg, 0.0) + jnp.log1p(jnp.exp(-jnp.abs(xg)))
    gval = -jnp.exp(gp_ref[0:1]) * softplus
    gate_ref[0] = jnp.where(lane < N_HEADS, jax.nn.sigmoid(pg), jnp.where(lane < 2 * N_HEADS, gval, 0.0))


def _delta_in(x, sh, sc, gnorm, wqkv, conv_w, wz, wg, gp, tm):
    b, s, d = x.shape
    tm = min(tm, s)
    hb = tm // SEQ_HALO
    last = s // SEQ_HALO - 1
    row = lambda bb, i: (bb, i, 0)
    vec = lambda bb, i: (bb, 0, 0)
    full = lambda bb, i: (0, 0)
    outs = [jax.ShapeDtypeStruct((b, s, d), BF16)] * 4 + [jax.ShapeDtypeStruct((b, s, 2 * LANES), F32)]
    return pl.pallas_call(
        functools.partial(_delta_in_kernel, tm=tm, ck=512),
        grid=(b, s // tm),
        in_specs=[pl.BlockSpec((1, tm, d), row),
                  pl.BlockSpec((1, SEQ_HALO, d), lambda bb, i: (bb, jnp.maximum(i * hb - 1, 0), 0)),
                  pl.BlockSpec((1, SEQ_HALO, d), lambda bb, i: (bb, jnp.minimum((i + 1) * hb, last), 0)),
                  pl.BlockSpec((1, 1, d), vec), pl.BlockSpec((1, 1, d), vec),
                  pl.BlockSpec((1, d), full),
                  pl.BlockSpec((d, 3 * d), full), pl.BlockSpec((3, 3 * d), full),
                  pl.BlockSpec((d, d), full), pl.BlockSpec((d, 2 * LANES), full),
                  pl.BlockSpec((2, 2 * LANES), full)],
        out_specs=[pl.BlockSpec((1, tm, d), row)] * 4 + [pl.BlockSpec((1, tm, 2 * LANES), row)],
        out_shape=outs,
        compiler_params=_params("parallel", "parallel"),
    )(x, x, x, sh, sc, gnorm, wqkv, conv_w, wz, wg, gp)


def _unit_lower_inverse(l, c):
    eye = (lax.broadcasted_iota(I32, (c, c), 0) == lax.broadcasted_iota(I32, (c, c), 1)).astype(F32)
    p = eye - l
    m = _dot(l, l, HIGHEST)
    k = 2
    while k < c:
        p = p + _dot(p, m, HIGHEST)
        k *= 2
        if k < c:
            m = _dot(m, m, HIGHEST)
    return p


def _scan_kernel(q_ref, k_ref, v_ref, gate_ref, s0_ref, *out_refs, c, with_output):
    if with_output:
        o_ref, s_ref = out_refs
    else:
        (s_ref,) = out_refs
    direction = pl.program_id(1)
    step = pl.program_id(2)

    @pl.when(step == 0)
    def _():
        s_ref[...] = s0_ref[...]

    gates = gate_ref[0]
    ri = lax.broadcasted_iota(I32, (c, c), 0)
    ci = lax.broadcasted_iota(I32, (c, c), 1)
    order = (ri - ci) * jnp.where(direction == 0, 1, -1)
    incl = order >= 0
    strict = order > 0
    inclf = incl.astype(F32)
    gc_cols = _dot(inclf, gates, HIGHEST)
    sel = (lax.broadcasted_iota(I32, (N_HEADS, LANES), 1)
           == lax.broadcasted_iota(I32, (N_HEADS, LANES), 0) + N_HEADS).astype(F32)
    g_rows = _dot_nt(sel, gates, HIGHEST)
    gc_rows = _dot_nt(g_rows, inclf, HIGHEST)
    tot_row = jnp.where(direction == 0, gc_cols[c - 1:c, :], gc_cols[0:1, :])

    for h in range(N_HEADS):
        sl = slice(h * HEAD_DIM, (h + 1) * HEAD_DIM)
        q = q_ref[0, :, sl].astype(F32)
        k = k_ref[0, :, sl].astype(F32)
        v = v_ref[0, :, sl].astype(F32)
        beta = gates[:, h:h + 1]
        gcc = gc_cols[:, N_HEADS + h:N_HEADS + h + 1]
        gcr = gc_rows[h:h + 1, :]
        tot = tot_row[:, N_HEADS + h:N_HEADS + h + 1]
        decay = jnp.exp(jnp.where(incl, gcc - gcr, -jnp.inf))
        kb = k * beta
        k16 = k.astype(BF16)
        lmat = jnp.where(strict, _dot_nt(kb.astype(BF16), k16) * decay, 0.0)
        qk = _dot_nt(q.astype(BF16), k16) * decay
        tinv = _unit_lower_inverse(lmat, c)
        e = jnp.exp(gcc)
        rhs = jnp.concatenate([v * beta, kb * e], axis=1)
        sol = _dot(tinv, rhs, HIGHEST)
        u, w = sol[:, :HEAD_DIM], sol[:, HEAD_DIM:]
        k_dec = k * jnp.exp(tot - gcc)
        s = s_ref[0, 0, h]
        s16 = s.astype(BF16)
        v_new = u - _dot(w.astype(BF16), s16)
        v16 = v_new.astype(BF16)
        if with_output:
            o = _dot((q * e).astype(BF16), s16) + _dot(qk.astype(BF16), v16)
            o_ref[0, 0, :, sl] = o.astype(o_ref.dtype)
        s_ref[0, 0, h] = s * jnp.exp(tot) + _dot_tn(k_dec.astype(BF16), v16)


def _delta_scan(q, k, v, gates, s0, c, with_output):
    b, s, d = q.shape
    nc = s // c

    def chunk(bb, dd, i):
        return (bb, jnp.where(dd == 0, i, nc - 1 - i), 0)

    def gchunk(bb, dd, i):
        return (bb, jnp.where(dd == 0, i, nc - 1 - i), dd)

    state_spec = pl.BlockSpec((1, 1, N_HEADS, HEAD_DIM, HEAD_DIM), lambda bb, dd, i: (bb, dd, 0, 0, 0))
    state_shape = jax.ShapeDtypeStruct((b, 2, N_HEADS, HEAD_DIM, HEAD_DIM), F32)
    out_specs, out_shape = [state_spec], [state_shape]
    if with_output:
        out_specs = [pl.BlockSpec((1, 1, c, d), lambda bb, dd, i: (dd, bb, jnp.where(dd == 0, i, nc - 1 - i), 0))] + out_specs
        out_shape = [jax.ShapeDtypeStruct((2, b, s, d), F32)] + out_shape
    return pl.pallas_call(
        functools.partial(_scan_kernel, c=c, with_output=with_output),
        grid=(b, 2, nc),
        in_specs=[pl.BlockSpec((1, c, d), chunk)] * 3 + [pl.BlockSpec((1, c, LANES), gchunk), state_spec],
        out_specs=out_specs,
        out_shape=out_shape,
        compiler_params=_params("parallel", "parallel", "arbitrary"),
    )(q, k, v, gates, s0)


def _ffn_pre(x1, gf, shf, scf, wr_ref, x1_ref, hf_ref, rt_ref):
    x1_ref[...] = x1
    hf = _rms_mod(x1, gf, shf, scf)
    hf_ref[...] = hf
    hi = hf.astype(BF16)
    lo = (hf - hi.astype(F32)).astype(BF16)
    a = _dot(hi, wr_ref[...])
    logits = a[:, :LANES] + a[:, LANES:] + _dot(lo, wr_ref[:, :LANES])
    lane = lax.broadcasted_iota(I32, logits.shape, 1)
    lanef = lane.astype(F32)
    big = float(LANES)
    gl = jnp.where(lane < N_GROUPS, logits, -jnp.inf)
    gmax = jnp.max(gl, axis=-1, keepdims=True)
    gi = jnp.min(jnp.where(gl == gmax, lanef, big), axis=-1, keepdims=True)
    pg_sel = 1.0 / jnp.sum(jnp.exp(gl - gmax), axis=-1, keepdims=True)
    rel = lanef - (N_GROUPS + EXPERTS_PER_GROUP * gi)
    el = jnp.where((rel >= 0.0) & (rel < float(EXPERTS_PER_GROUP)), logits, -jnp.inf)
    m1 = jnp.max(el, axis=-1, keepdims=True)
    i1 = jnp.min(jnp.where(el == m1, lanef, big), axis=-1, keepdims=True)
    el2 = jnp.where(lanef == i1, -jnp.inf, el)
    m2 = jnp.max(el2, axis=-1, keepdims=True)
    i2 = jnp.min(jnp.where(el2 == m2, lanef, big), axis=-1, keepdims=True)
    r2 = jnp.exp(m2 - m1)
    w1 = pg_sel / (1.0 + r2)
    w2 = pg_sel * r2 / (1.0 + r2)
    rt_ref[...] = jnp.where(lane == 0, w1, jnp.where(lane == 1, w2, jnp.where(
        lane == 2, i1 - N_GROUPS, jnp.where(lane == 3, i2 - N_GROUPS, 0.0))))


def _delta_out_kernel(of_ref, ob_ref, z_ref, x_ref, gt_ref, on_ref, wo_ref, gf_ref, shf_ref, scf_ref, wr_ref,
                      x1_ref, hf_ref, rt_ref):
    o = of_ref[0, 0] + ob_ref[0, 0]
    z = z_ref[0].astype(F32)
    onorm = on_ref[...]
    parts = []
    for h in range(N_HEADS):
        oh = o[:, h * HEAD_DIM:(h + 1) * HEAD_DIM]
        parts.append(oh * lax.rsqrt(jnp.mean(oh * oh, axis=-1, keepdims=True) + EPS) * onorm)
    y = jnp.concatenate(parts, axis=1) * _silu(z)
    x1 = x_ref[0] + gt_ref[0] * _dot(y.astype(BF16), wo_ref[...])
    _ffn_pre(x1, gf_ref[...], shf_ref[0], scf_ref[0], wr_ref, x1_ref, hf_ref, rt_ref)


def _delta_out(o2, z, x, gt, onorm, wo, gf, shf, scf, wr, tm):
    b, s, d = x.shape
    t = b * s
    nt = s // tm
    row = lambda bb, i: (bb, i, 0)
    vec = lambda bb, i: (bb, 0, 0)
    full = lambda bb, i: (0, 0)
    flat = lambda bb, i: (bb * nt + i, 0)
    return pl.pallas_call(
        _delta_out_kernel,
        grid=(b, nt),
        in_specs=[pl.BlockSpec((1, 1, tm, d), lambda bb, i: (0, bb, i, 0)),
                  pl.BlockSpec((1, 1, tm, d), lambda bb, i: (1, bb, i, 0)),
                  pl.BlockSpec((1, tm, d), row), pl.BlockSpec((1, tm, d), row),
                  pl.BlockSpec((1, 1, d), vec), pl.BlockSpec((1, HEAD_DIM), full),
                  pl.BlockSpec((d, d), full), pl.BlockSpec((1, d), full),
                  pl.BlockSpec((1, 1, d), vec), pl.BlockSpec((1, 1, d), vec),
                  pl.BlockSpec((d, 2 * LANES), full)],
        out_specs=[pl.BlockSpec((tm, d), flat), pl.BlockSpec((tm, d), flat), pl.BlockSpec((tm, LANES), flat)],
        out_shape=[jax.ShapeDtypeStruct((t, d), F32), jax.ShapeDtypeStruct((t, d), F32),
                   jax.ShapeDtypeStruct((t, LANES), F32)],
        compiler_params=_params("parallel", "parallel"),
    )(o2, o2, z, x, gt, onorm, wo, gf, shf, scf, wr)


def _sconv_kernel(xm_ref, xp_ref, xn_ref, sh_ref, sc_ref, g_ref, win_ref, cw_ref, wo_ref, gt_ref,
                  gf_ref, shf_ref, scf_ref, wr_ref, x1_ref, hf_ref, rt_ref, *, tm, ck):
    i = pl.program_id(1)
    n = pl.num_programs(1)
    g, sh, sc = g_ref[...], sh_ref[0], sc_ref[0]
    xm = xm_ref[0]
    hm = _rms_mod(xm, g, sh, sc)
    hp = _rms_mod(xp_ref[0], g, sh, sc) * (i > 0).astype(F32)
    hn = _rms_mod(xn_ref[0], g, sh, sc) * (i < n - 1).astype(F32)
    hm16 = hm.astype(BF16)
    hext = jnp.concatenate([hp, hm, hn], axis=0).astype(BF16)
    d = xm.shape[1]
    col = lax.broadcasted_iota(I32, (tm, 1), 0) % GRID_W
    acc = jnp.zeros((tm, d), F32)
    for j in range(d // ck):
        c0 = j * ck
        gate_b = _dot(hm16, win_ref[:, c0:c0 + ck])
        cw = cw_ref[:, c0:c0 + ck]
        if c0 < d // 2:
            u = _dot(hm16, win_ref[:, d + c0:d + c0 + ck]) * _dot(hm16, win_ref[:, 2 * d + c0:2 * d + c0 + ck])
            left = jnp.where(col == 0, 0.0, pltpu.roll(u, 1, 0))
            right = jnp.where(col == GRID_W - 1, 0.0, pltpu.roll(u, tm - 1, 0))
            y = cw[0:1] * left + cw[1:2] * u + cw[2:3] * right
        else:
            u = _dot(hext, win_ref[:, d + c0:d + c0 + ck]) * _dot(hext, win_ref[:, 2 * d + c0:2 * d + c0 + ck])
            y = cw[0:1] * u[0:tm] + cw[1:2] * u[GRID_W:GRID_W + tm] + cw[2:3] * u[2 * GRID_W:2 * GRID_W + tm]
        acc = acc + _dot((gate_b * y).astype(BF16), wo_ref[c0:c0 + ck, :])
    x1 = xm + gt_ref[0] * acc
    _ffn_pre(x1, gf_ref[...], shf_ref[0], scf_ref[0], wr_ref, x1_ref, hf_ref, rt_ref)


def _sconv(x, sh, sc, gnorm, win, conv_w, wo, gt, gf, shf, scf, wr, tm):
    b, s, d = x.shape
    t = b * s
    nt = s // tm
    hb = tm // GRID_W
    last = s // GRID_W - 1
    row = lambda bb, i: (bb, i, 0)
    vec = lambda bb, i: (bb, 0, 0)
    full = lambda bb, i: (0, 0)
    flat = lambda bb, i: (bb * nt + i, 0)
    return pl.pallas_call(
        functools.partial(_sconv_kernel, tm=tm, ck=256),
        grid=(b, nt),
        in_specs=[pl.BlockSpec((1, tm, d), row),
                  pl.BlockSpec((1, GRID_W, d), lambda bb, i: (bb, jnp.maximum(i * hb - 1, 0), 0)),
                  pl.BlockSpec((1, GRID_W, d), lambda bb, i: (bb, jnp.minimum((i + 1) * hb, last), 0)),
                  pl.BlockSpec((1, 1, d), vec), pl.BlockSpec((1, 1, d), vec), pl.BlockSpec((1, d), full),
                  pl.BlockSpec((d, 3 * d), full), pl.BlockSpec((3, d), full), pl.BlockSpec((d, d), full),
                  pl.BlockSpec((1, 1, d), vec), pl.BlockSpec((1, d), full),
                  pl.BlockSpec((1, 1, d), vec), pl.BlockSpec((1, 1, d), vec),
                  pl.BlockSpec((d, 2 * LANES), full)],
        out_specs=[pl.BlockSpec((tm, d), flat), pl.BlockSpec((tm, d), flat), pl.BlockSpec((tm, LANES), flat)],
        out_shape=[jax.ShapeDtypeStruct((t, d), F32), jax.ShapeDtypeStruct((t, d), F32),
                   jax.ShapeDtypeStruct((t, LANES), F32)],
        compiler_params=_params("parallel", "parallel"),
    )(x, x, x, sh, sc, gnorm, win, conv_w, wo, gt, gf, shf, scf, wr)


def _rank_kernel(e_ref, rank_ref, cnt_ref, carry_ref, *, tm):
    i = pl.program_id(0)

    @pl.when(i == 0)
    def _():
        carry_ref[...] = jnp.zeros_like(carry_ref)

    e = e_ref[...]
    sub = lax.broadcasted_iota(I32, (N_EXPERTS, tm), 0)
    earlier = (lax.broadcasted_iota(I32, (tm, tm), 0) < lax.broadcasted_iota(I32, (tm, tm), 1)).astype(BF16)
    carry = carry_ref[:, 0:1]
    ranks = []
    for kk in range(2):
        onehot = sub == e[kk:kk + 1, :]
        ohf = onehot.astype(F32)
        before = _dot(onehot.astype(BF16), earlier) + carry
        ranks.append(jnp.sum(ohf * before, axis=0, keepdims=True))
        carry = carry + jnp.sum(ohf, axis=1, keepdims=True)
    carry_ref[...] = jnp.broadcast_to(carry, carry_ref.shape)
    rank_ref[...] = jnp.concatenate(ranks + [jnp.zeros((6, tm), F32)], axis=0).astype(I32)
    cnt_ref[...] = jnp.broadcast_to(carry, cnt_ref.shape).astype(I32)


def _rank(e8, tm):
    t = e8.shape[1]
    return pl.pallas_call(
        functools.partial(_rank_kernel, tm=tm),
        grid=(t // tm,),
        in_specs=[pl.BlockSpec((8, tm), lambda i: (0, i))],
        out_specs=[pl.BlockSpec((8, tm), lambda i: (0, i)), pl.BlockSpec((N_EXPERTS, LANES), lambda i: (0, 0))],
        out_shape=[jax.ShapeDtypeStruct((8, t), I32), jax.ShapeDtypeStruct((N_EXPERTS, LANES), I32)],
        scratch_shapes=[pltpu.VMEM((N_EXPERTS, LANES), F32)],
        compiler_params=_params("arbitrary"),
    )(e8)


def _row_copy(src_ref, src_row, dst_ref, dst_row, sem):
    return pltpu.make_async_copy(src_ref.at[pl.ds(src_row, 1), :], dst_ref.at[pl.ds(dst_row, 1), :], sem)


def _dispatch_kernel(dest_ref, hf_ref, xb_in_ref, xb_ref, sem, *, td):
    del xb_in_ref

    def issue(r, carry):
        _row_copy(hf_ref, r, xb_ref, dest_ref[0, 0, r], sem).start()
        _row_copy(hf_ref, r, xb_ref, dest_ref[0, 0, td + r], sem).start()
        return carry

    lax.fori_loop(0, td, issue, 0)

    def drain(r, carry):
        _row_copy(hf_ref, 0, xb_ref, 0, sem).wait()
        _row_copy(hf_ref, 0, xb_ref, 0, sem).wait()
        return carry

    lax.fori_loop(0, td, drain, 0)


def _dispatch(dest3, hf, cap, td):
    t, d = hf.shape
    xb0 = jnp.zeros((cap, d), hf.dtype)
    return pl.pallas_call(
        functools.partial(_dispatch_kernel, td=td),
        grid=(t // td,),
        in_specs=[pl.BlockSpec((1, 1, 2 * td), lambda i: (i, 0, 0), memory_space=pltpu.SMEM),
                  pl.BlockSpec((td, d), lambda i: (i, 0)),
                  pl.BlockSpec(memory_space=pl.ANY)],
        out_specs=pl.BlockSpec(memory_space=pl.ANY),
        out_shape=jax.ShapeDtypeStruct((cap, d), hf.dtype),
        scratch_shapes=[pltpu.SemaphoreType.DMA],
        input_output_aliases={2: 0},
        compiler_params=_params("arbitrary"),
    )(dest3, hf, xb0)


def _expert_kernel(be_ref, bs_ref, nu_ref, x_ref, w1_ref, w3_ref, w2_ref, y_ref, w1b, w3b, w2b):
    i = pl.program_id(0)
    changed = jnp.logical_or(i == 0, be_ref[i] != be_ref[jnp.maximum(i - 1, 0)])

    @pl.when(changed)
    def _():
        w1b[...] = w1_ref[0].astype(BF16)
        w3b[...] = w3_ref[0].astype(BF16)
        w2b[...] = w2_ref[0].astype(BF16)

    @pl.when(i < nu_ref[0])
    def _():
        x = x_ref[...].astype(BF16)
        a = _dot(x, w1b[...])
        g = _dot(x, w3b[...])
        y_ref[...] = _dot((_silu(a) * g).astype(BF16), w2b[...])

    @pl.when(i >= nu_ref[0])
    def _():
        y_ref[...] = jnp.zeros_like(y_ref)


def _experts(block_e, block_src, n_used, xb, w1, w3, w2, bm):
    cap, d = xb.shape
    de = w1.shape[2]
    grid_spec = pltpu.PrefetchScalarGridSpec(
        num_scalar_prefetch=3,
        grid=(cap // bm,),
        in_specs=[pl.BlockSpec((bm, d), lambda i, be, bs, nu: (bs[i], 0)),
                  pl.BlockSpec((1, d, de), lambda i, be, bs, nu: (be[i], 0, 0)),
                  pl.BlockSpec((1, d, de), lambda i, be, bs, nu: (be[i], 0, 0)),
                  pl.BlockSpec((1, de, d), lambda i, be, bs, nu: (be[i], 0, 0))],
        out_specs=pl.BlockSpec((bm, d), lambda i, be, bs, nu: (i, 0)),
        scratch_shapes=[pltpu.VMEM((d, de), BF16), pltpu.VMEM((d, de), BF16), pltpu.VMEM((de, d), BF16)],
    )
    return pl.pallas_call(
        _expert_kernel,
        grid_spec=grid_spec,
        out_shape=jax.ShapeDtypeStruct((cap, d), F32),
        compiler_params=_params("arbitrary"),
    )(block_e, block_src, n_used, xb, w1, w3, w2)


def _combine_kernel(dest_ref, rt_ref, x1_ref, gt_ref, fn_ref, yb_ref, o_ref, buf, sem, *, tc, final):
    def issue(r, carry):
        _row_copy(yb_ref, dest_ref[0, 0, r], buf.at[0], r, sem).start()
        _row_copy(yb_ref, dest_ref[0, 0, tc + r], buf.at[1], r, sem).start()
        return carry

    lax.fori_loop(0, tc, issue, 0)

    def drain(r, carry):
        _row_copy(yb_ref, 0, buf.at[0], 0, sem).wait()
        _row_copy(yb_ref, 0, buf.at[1], 0, sem).wait()
        return carry

    lax.fori_loop(0, tc, drain, 0)
    rt = rt_ref[...]
    moe = rt[:, 0:1] * buf[0] + rt[:, 1:2] * buf[1]
    x2 = x1_ref[...] + gt_ref[0] * moe
    if final:
        ms = jnp.mean(x2 * x2, axis=-1, keepdims=True)
        x2 = x2 * lax.rsqrt(ms + EPS) * fn_ref[...]
    o_ref[...] = x2


def _combine(dest3, rt, x1, gt, fnorm, yb, s, tc, final):
    t, d = x1.shape
    per_b = s // tc
    return pl.pallas_call(
        functools.partial(_combine_kernel, tc=tc, final=final),
        grid=(t // tc,),
        in_specs=[pl.BlockSpec((1, 1, 2 * tc), lambda i: (i, 0, 0), memory_space=pltpu.SMEM),
                  pl.BlockSpec((tc, LANES), lambda i: (i, 0)),
                  pl.BlockSpec((tc, d), lambda i: (i, 0)),
                  pl.BlockSpec((1, 1, d), lambda i: (i // per_b, 0, 0)),
                  pl.BlockSpec((1, d), lambda i: (0, 0)),
                  pl.BlockSpec(memory_space=pl.ANY)],
        out_specs=pl.BlockSpec((tc, d), lambda i: (i, 0)),
        out_shape=jax.ShapeDtypeStruct((t, d), F32),
        scratch_shapes=[pltpu.VMEM((2, tc, d), F32), pltpu.SemaphoreType.DMA],
        compiler_params=_params("arbitrary"),
    )(dest3, rt, x1, gt, fnorm, yb)


def _moe(hf, rt, x1, gt, fnorm, w1, w3, w2, s, final, bm=256, tr=512, td=256):
    t, d = hf.shape
    e8 = jnp.zeros((8, t), I32).at[0:2].set(rt[:, 2:4].astype(I32).T)
    rank8, cnt = _rank(e8, tr)
    counts = cnt[:, 0]
    padded = (counts + bm - 1) // bm * bm
    ends = jnp.cumsum(padded)
    starts = ends - padded
    dest = starts[e8[0:2]] + rank8[0:2]
    dest3 = dest.reshape(2, t // td, td).transpose(1, 0, 2).reshape(t // td, 1, 2 * td)
    n_blocks = -(-(2 * t) // bm) + N_EXPERTS
    cap = n_blocks * bm
    blk0 = jnp.arange(n_blocks, dtype=I32) * bm
    n_used = (ends[-1] // bm).astype(I32)
    block_src = jnp.minimum(jnp.arange(n_blocks, dtype=I32), n_used - 1)
    block_e = jnp.minimum(jnp.searchsorted(ends, block_src * bm, side='right'), N_EXPERTS - 1).astype(I32)
    del blk0
    xb = _dispatch(dest3, hf, cap, td)
    yb = _experts(block_e, block_src, n_used.reshape(1), xb, w1, w3, w2, bm)
    return _combine(dest3, rt, x1, gt, fnorm, yb, s, td, final)


def _router_weights(router_g, router_e):
    d = router_g.shape[0]
    w = jnp.zeros((d, LANES), F32).at[:, :N_GROUPS].set(router_g).at[:, N_GROUPS:N_GROUPS + N_EXPERTS].set(router_e)
    hi = w.astype(BF16)
    lo = (w - hi.astype(F32)).astype(BF16)
    return jnp.concatenate([hi, lo], axis=1)


def _gate_weights(w_in, a_log, dt_bias):
    d = w_in.shape[0]
    base = 4 * d
    wg = jnp.zeros((d, 2 * LANES), F32)
    gp = jnp.zeros((2, 2 * LANES), F32)
    for direction in range(2):
        o = direction * LANES
        wg = wg.at[:, o:o + N_HEADS].set(w_in[:, base + direction * N_HEADS:base + (direction + 1) * N_HEADS])
        wg = wg.at[:, o + N_HEADS:o + 2 * N_HEADS].set(
            w_in[:, base + (2 + direction) * N_HEADS:base + (3 + direction) * N_HEADS])
        gp = gp.at[0, o + N_HEADS:o + 2 * N_HEADS].set(a_log[direction])
        gp = gp.at[1, o + N_HEADS:o + 2 * N_HEADS].set(dt_bias[direction])
    return wg.astype(BF16), gp


def kernel(x, c, ctx, c_ctx, ada_w, ada_b, norm_mix, norm_ffn, w_in_a, conv_a, a_log_a, dt_bias_a, onorm_a, w_out_a,
           w_in_b, conv_b, w_out_b, router_g, router_e, w1, w3, w2, final_norm):
    b, s, d = x.shape
    chunk = 64
    tm = min(512, s)

    cond = jnp.zeros((16, d), F32).at[:b].set(c).at[b].set(c_ctx)
    mod = _ada_params(cond, ada_w, ada_b)

    def mods(layer, rows):
        m = mod[layer, rows]
        return [jnp.broadcast_to(m[:, None, k * d:(k + 1) * d], (b, 1, d)) for k in range(N_MOD)]

    sh_m, sc_m, gt_m, sh_f, sc_f, gt_f = mods(0, slice(0, b))
    csh_m, csc_m = mods(0, slice(b, b + 1))[:2]
    w_in = w_in_a[0]
    wqkv, wz = w_in[:, :3 * d].astype(BF16), w_in[:, 3 * d:4 * d].astype(BF16)
    wg, gp = _gate_weights(w_in, a_log_a[0], dt_bias_a[0])
    gn = norm_mix[0].reshape(1, d)
    qc, kc, vc, _, gc = _delta_in(ctx, csh_m, csc_m, gn, wqkv, conv_a[0], wz, wg, gp, tm)
    s_zero = jnp.zeros((b, 2, N_HEADS, HEAD_DIM, HEAD_DIM), F32)
    (s_ctx,) = _delta_scan(qc, kc, vc, gc, s_zero, chunk, False)
    ql, kl, vl, zl, gl = _delta_in(x, sh_m, sc_m, gn, wqkv, conv_a[0], wz, wg, gp, tm)
    o2, _ = _delta_scan(ql, kl, vl, gl, s_ctx, chunk, True)
    wr = _router_weights(router_g[0], router_e[0])
    x1, hf, rt = _delta_out(o2, zl, x, gt_m, onorm_a[0].reshape(1, HEAD_DIM), w_out_a[0].astype(BF16),
                            norm_ffn[0].reshape(1, d), sh_f, sc_f, wr, tm)
    fn = final_norm.reshape(1, d)
    x2 = _moe(hf, rt, x1, gt_f, fn, w1[0], w3[0], w2[0], s, False).reshape(b, s, d)

    sh_m, sc_m, gt_m, sh_f, sc_f, gt_f = mods(1, slice(0, b))
    wr = _router_weights(router_g[1], router_e[1])
    x1, hf, rt = _sconv(x2, sh_m, sc_m, norm_mix[1].reshape(1, d), w_in_b[0].astype(BF16), conv_b[0],
                        w_out_b[0].astype(BF16), gt_m, norm_ffn[1].reshape(1, d), sh_f, sc_f, wr, tm)
    out = _moe(hf, rt, x1, gt_f, fn, w1[1], w3[1], w2[1], s, True)
    return out.reshape(b, s, d)
```

```python
import functools

import jax
import jax.numpy as jnp
from jax import lax
from jax.experimental import pallas as pl
from jax.experimental.pallas import tpu as pltpu

F32 = jnp.float32
BF16 = jnp.bfloat16
I32 = jnp.int32
HIGHEST = lax.Precision.HIGHEST

EPS = 1e-6
N_HEADS = 8
HEAD_DIM = 128
GRID_W = 64
N_GROUPS = 4
EXPERTS_PER_GROUP = 8
N_EXPERTS = N_GROUPS * EXPERTS_PER_GROUP
PAIRS_PER_GROUP = EXPERTS_PER_GROUP * (EXPERTS_PER_GROUP - 1) // 2
N_CLASSES = N_GROUPS * PAIRS_PER_GROUP
_PAIRS = [(a, b) for a in range(EXPERTS_PER_GROUP) for b in range(a + 1, EXPERTS_PER_GROUP)]
CLASS_EXPERT_A = [g * EXPERTS_PER_GROUP + a for g in range(N_GROUPS) for a, _ in _PAIRS]
CLASS_EXPERT_B = [g * EXPERTS_PER_GROUP + b for g in range(N_GROUPS) for _, b in _PAIRS]
N_MOD = 6
LANES = 128
SCAN_CHUNK = 64
SEQ_HALO = 8
VMEM_LIMIT = 48 * 1024 * 1024


def _params(*sem):
    return pltpu.CompilerParams(dimension_semantics=sem, vmem_limit_bytes=VMEM_LIMIT)


def _dot(a, b, precision=None):
    return jnp.dot(a, b, preferred_element_type=F32, precision=precision)


def _dot_nt(a, b, precision=None):
    return lax.dot_general(a, b, (((1,), (1,)), ((), ())), preferred_element_type=F32, precision=precision)


def _dot_tn(a, b, precision=None):
    return lax.dot_general(a, b, (((0,), (0,)), ((), ())), preferred_element_type=F32, precision=precision)


def _silu(x):
    return x * jax.nn.sigmoid(x)


def _rms_mod(x, g, shift, scale):
    ms = jnp.mean(x * x, axis=-1, keepdims=True)
    return x * lax.rsqrt(ms + EPS) * g * (1.0 + scale) + shift


def _ada_kernel(c_ref, w_ref, b_ref, o_ref):
    o_ref[0] = _dot(_silu(c_ref[...]), w_ref[0], HIGHEST) + b_ref[0]


def _ada_params(cond, ada_w, ada_b):
    depth, d, n = ada_w.shape
    tn = 512
    r = cond.shape[0]
    return pl.pallas_call(
        _ada_kernel,
        grid=(depth, n // tn),
        in_specs=[pl.BlockSpec((r, d), lambda l, j: (0, 0)),
                  pl.BlockSpec((1, d, tn), lambda l, j: (l, 0, j)),
                  pl.BlockSpec((1, 1, tn), lambda l, j: (l, 0, j))],
        out_specs=pl.BlockSpec((1, r, tn), lambda l, j: (l, 0, j)),
        out_shape=jax.ShapeDtypeStruct((depth, r, n), F32),
        compiler_params=_params("parallel", "parallel"),
    )(cond, ada_w, ada_b.reshape(depth, 1, n))


def _delta_in_kernel(xm_ref, xp_ref, xn_ref, sh_ref, sc_ref, g_ref, wqkv_ref, cw_ref, wz_ref, wg_ref, gp_ref,
                     q_ref, k_ref, v_ref, z_ref, gate_ref, *, tm, ck):
    i = pl.program_id(1)
    n = pl.num_programs(1)
    g, sh, sc = g_ref[...], sh_ref[0], sc_ref[0]
    hm = _rms_mod(xm_ref[0], g, sh, sc)
    hp = _rms_mod(xp_ref[0], g, sh, sc) * (i > 0).astype(F32)
    hn = _rms_mod(xn_ref[0], g, sh, sc) * (i < n - 1).astype(F32)
    hm16 = hm.astype(BF16)
    hext = jnp.concatenate([hp, hm, hn], axis=0).astype(BF16)
    rows = tm + 2 * SEQ_HALO
    d = hm.shape[1]
    outs = (q_ref, k_ref, v_ref)
    for j in range(3 * d // ck):
        p = _dot(hext, wqkv_ref[:, j * ck:(j + 1) * ck])
        cw = cw_ref[:, j * ck:(j + 1) * ck]
        y = cw[0:1] * pltpu.roll(p, 1, 0) + cw[1:2] * p + cw[2:3] * pltpu.roll(p, rows - 1, 0)
        y = _silu(y[SEQ_HALO:SEQ_HALO + tm])
        which, col0 = (j * ck) // d, (j * ck) % d
        for hh in range(ck // HEAD_DIM):
            yh = y[:, hh * HEAD_DIM:(hh + 1) * HEAD_DIM]
            if which < 2:
                yh = yh * lax.rsqrt(jnp.sum(yh * yh, axis=-1, keepdims=True) + EPS)
            if which == 0:
                yh = yh * HEAD_DIM ** -0.5
            c0 = col0 + hh * HEAD_DIM
            outs[which][0, :, c0:c0 + HEAD_DIM] = yh.astype(q_ref.dtype)
    z_ref[0] = _dot(hm16, wz_ref[...]).astype(z_ref.dtype)
    pg = _dot(hm16, wg_ref[...])
    lane = lax.broadcasted_iota(I32, pg.shape, 1) % LANES
    xg = pg + gp_ref[1:2]
    softplus = jnp.maximum(xg, 0.0) + jnp.log1p(jnp.exp(-jnp.abs(xg)))
    gval = -jnp.exp(gp_ref[0:1]) * softplus
    ri = lax.broadcasted_iota(I32, (tm, tm), 0)
    ci = lax.broadcasted_iota(I32, (tm, tm), 1)
    same = (ri // SCAN_CHUNK) == (ci // SCAN_CHUNK)
    g_hi = gval.astype(BF16)
    r1 = gval - g_hi.astype(F32)
    g_mid = r1.astype(BF16)
    g_lo = (r1 - g_mid.astype(F32)).astype(BF16)
    cums = []
    for direction in range(2):
        tri = (same & ((ri >= ci) if direction == 0 else (ri <= ci))).astype(BF16)
        sl = slice(direction * LANES, (direction + 1) * LANES)
        terms = jnp.concatenate([g_hi[:, sl], g_mid[:, sl], g_lo[:, sl]], axis=1)
        cs = _dot(tri, terms)
        cums.append(cs[:, :LANES] + cs[:, LANES:2 * LANES] + cs[:, 2 * LANES:])
    gcum = jnp.concatenate(cums, axis=1)
    gate_ref[0] = jnp.where(lane < N_HEADS, jax.nn.sigmoid(pg), jnp.where(
        lane < 2 * N_HEADS, gval, jnp.where(lane < 3 * N_HEADS, gcum, 0.0)))


def _delta_in(x, sh, sc, gnorm, wqkv, conv_w, wz, wg, gp, tm):
    b, s, d = x.shape
    tm = min(tm, s)
    hb = tm // SEQ_HALO
    last = s // SEQ_HALO - 1
    row = lambda bb, i: (bb, i, 0)
    vec = lambda bb, i: (bb, 0, 0)
    full = lambda bb, i: (0, 0)
    outs = [jax.ShapeDtypeStruct((b, s, d), BF16)] * 4 + [jax.ShapeDtypeStruct((b, s, 2 * LANES), F32)]
    return pl.pallas_call(
        functools.partial(_delta_in_kernel, tm=tm, ck=512),
        grid=(b, s // tm),
        in_specs=[pl.BlockSpec((1, tm, d), row),
                  pl.BlockSpec((1, SEQ_HALO, d), lambda bb, i: (bb, jnp.maximum(i * hb - 1, 0), 0)),
                  pl.BlockSpec((1, SEQ_HALO, d), lambda bb, i: (bb, jnp.minimum((i + 1) * hb, last), 0)),
                  pl.BlockSpec((1, 1, d), vec), pl.BlockSpec((1, 1, d), vec),
                  pl.BlockSpec((1, d), full),
                  pl.BlockSpec((d, 3 * d), full), pl.BlockSpec((3, 3 * d), full),
                  pl.BlockSpec((d, d), full), pl.BlockSpec((d, 2 * LANES), full),
                  pl.BlockSpec((2, 2 * LANES), full)],
        out_specs=[pl.BlockSpec((1, tm, d), row)] * 4 + [pl.BlockSpec((1, tm, 2 * LANES), row)],
        out_shape=outs,
        compiler_params=_params("parallel", "parallel"),
    )(x, x, x, sh, sc, gnorm, wqkv, conv_w, wz, wg, gp)


def _scan_kernel(q_ref, k_ref, v_ref, gate_ref, s0_ref, *out_refs, c, with_output):
    if with_output:
        o_ref, s_ref = out_refs
    else:
        (s_ref,) = out_refs
    direction = pl.program_id(1)
    step = pl.program_id(2)

    @pl.when(step == 0)
    def _():
        s_ref[...] = s0_ref[...]

    heads = range(N_HEADS)
    hs = [slice(h * HEAD_DIM, (h + 1) * HEAD_DIM) for h in heads]
    gates = gate_ref[0]
    gates_t = jnp.concatenate([gates, jnp.zeros((LANES - c, LANES), F32)], axis=0).T[:, :c]
    ri = lax.broadcasted_iota(I32, (c, c), 0)
    ci = lax.broadcasted_iota(I32, (c, c), 1)
    order = (ri - ci) * jnp.where(direction == 0, 1, -1)
    incl = order >= 0
    strict = order > 0
    eye = (ri == ci).astype(F32)
    tot_row = jnp.where(direction == 0, gates[c - 1:c, :], gates[0:1, :])

    k16 = [k_ref[0, :, sl] for sl in hs]
    q16 = [q_ref[0, :, sl] for sl in hs]
    kf = [x.astype(F32) for x in k16]
    qf = [x.astype(F32) for x in q16]
    vf = [v_ref[0, :, sl].astype(F32) for sl in hs]
    s_old = [s_ref[0, 0, h] for h in heads]
    beta = [gates[:, h:h + 1] for h in heads]
    gcc = [gates[:, 2 * N_HEADS + h:2 * N_HEADS + h + 1] for h in heads]
    gcr = [gates_t[2 * N_HEADS + h:2 * N_HEADS + h + 1, :] for h in heads]
    tot = [tot_row[:, 2 * N_HEADS + h:2 * N_HEADS + h + 1] for h in heads]
    decay = [jnp.exp(jnp.where(incl, gcc[h] - gcr[h], -jnp.inf)) for h in heads]
    e = [jnp.exp(gcc[h]) for h in heads]
    kb = [kf[h] * beta[h] for h in heads]
    a = [_dot_nt(jnp.concatenate([kb[h].astype(BF16), q16[h]], axis=0), k16[h]) for h in heads]
    lmat = [jnp.where(strict, a[h][:c] * decay[h], 0.0) for h in heads]
    qk = [(a[h][c:] * decay[h]).astype(BF16) for h in heads]
    p = [eye - lmat[h] for h in heads]
    l16 = [lmat[h].astype(BF16) for h in heads]
    m = [_dot(l16[h], l16[h]) for h in heads]
    span = 2
    while span < c:
        span *= 2
        m16 = [m[h].astype(BF16) for h in heads]
        if span < c:
            pm = [_dot(jnp.concatenate([p[h], m[h]], axis=0).astype(BF16), m16[h]) for h in heads]
            p = [p[h] + pm[h][:c] for h in heads]
            m = [pm[h][c:] for h in heads]
        else:
            p = [p[h] + _dot(p[h].astype(BF16), m16[h]) for h in heads]
    rhs = [jnp.concatenate([vf[h] * beta[h], kb[h] * e[h]], axis=1).astype(BF16) for h in heads]
    sol = [_dot(p[h].astype(BF16), rhs[h]) for h in heads]
    s16 = [s_old[h].astype(BF16) for h in heads]
    ws = [_dot(jnp.concatenate([sol[h][:, HEAD_DIM:], qf[h] * e[h]], axis=0).astype(BF16), s16[h]) for h in heads]
    v16 = [(sol[h][:, :HEAD_DIM] - ws[h][:c]).astype(BF16) for h in heads]
    k_dec = [(kf[h] * jnp.exp(tot[h] - gcc[h])).astype(BF16) for h in heads]
    s_new = [s_old[h] * jnp.exp(tot[h]) + _dot_tn(k_dec[h], v16[h]) for h in heads]
    if with_output:
        o = [ws[h][c:] + _dot(qk[h], v16[h]) for h in heads]
        for h in heads:
            o_ref[0, 0, :, hs[h]] = o[h].astype(o_ref.dtype)
    for h in heads:
        s_ref[0, 0, h] = s_new[h]


def _delta_scan(q, k, v, gates, s0, c, with_output):
    b, s, d = q.shape
    nc = s // c

    def chunk(bb, dd, i):
        return (bb, jnp.where(dd == 0, i, nc - 1 - i), 0)

    def gchunk(bb, dd, i):
        return (bb, jnp.where(dd == 0, i, nc - 1 - i), dd)

    state_spec = pl.BlockSpec((1, 1, N_HEADS, HEAD_DIM, HEAD_DIM), lambda bb, dd, i: (bb, dd, 0, 0, 0))
    state_shape = jax.ShapeDtypeStruct((b, 2, N_HEADS, HEAD_DIM, HEAD_DIM), F32)
    out_specs, out_shape = [state_spec], [state_shape]
    if with_output:
        out_specs = [pl.BlockSpec((1, 1, c, d), lambda bb, dd, i: (dd, bb, jnp.where(dd == 0, i, nc - 1 - i), 0))] + out_specs
        out_shape = [jax.ShapeDtypeStruct((2, b, s, d), F32)] + out_shape
    return pl.pallas_call(
        functools.partial(_scan_kernel, c=c, with_output=with_output),
        grid=(b, 2, nc),
        in_specs=[pl.BlockSpec((1, c, d), chunk)] * 3 + [pl.BlockSpec((1, c, LANES), gchunk), state_spec],
        out_specs=out_specs,
        out_shape=out_shape,
        compiler_params=_params("parallel", "parallel", "arbitrary"),
    )(q, k, v, gates, s0)


def _ffn_pre(x1, gf, shf, scf, wr_ref, x1_ref, hf_ref, rt_ref):
    x1_ref[...] = x1
    hf = _rms_mod(x1, gf, shf, scf)
    d = hf.shape[1]
    hf_ref[:, :d] = hf
    hi = hf.astype(BF16)
    lo = (hf - hi.astype(F32)).astype(BF16)
    a = _dot(hi, wr_ref[...])
    logits = a[:, :LANES] + a[:, LANES:] + _dot(lo, wr_ref[:, :LANES])
    lane = lax.broadcasted_iota(I32, logits.shape, 1)
    lanef = lane.astype(F32)
    big = float(LANES)
    gl = jnp.where(lane < N_GROUPS, logits, -jnp.inf)
    gmax = jnp.max(gl, axis=-1, keepdims=True)
    gi = jnp.min(jnp.where(gl == gmax, lanef, big), axis=-1, keepdims=True)
    pg_sel = 1.0 / jnp.sum(jnp.exp(gl - gmax), axis=-1, keepdims=True)
    rel = lanef - (N_GROUPS + EXPERTS_PER_GROUP * gi)
    el = jnp.where((rel >= 0.0) & (rel < float(EXPERTS_PER_GROUP)), logits, -jnp.inf)
    m1 = jnp.max(el, axis=-1, keepdims=True)
    i1 = jnp.min(jnp.where(el == m1, lanef, big), axis=-1, keepdims=True)
    el2 = jnp.where(lanef == i1, -jnp.inf, el)
    m2 = jnp.max(el2, axis=-1, keepdims=True)
    i2 = jnp.min(jnp.where(el2 == m2, lanef, big), axis=-1, keepdims=True)
    r2 = jnp.exp(m2 - m1)
    w1 = pg_sel / (1.0 + r2)
    w2 = pg_sel * r2 / (1.0 + r2)
    first_low = i1 < i2
    a_loc = jnp.minimum(i1, i2) - (N_GROUPS + EXPERTS_PER_GROUP * gi)
    b_loc = jnp.maximum(i1, i2) - (N_GROUPS + EXPERTS_PER_GROUP * gi)
    pair = a_loc * (EXPERTS_PER_GROUP - 1) - a_loc * (a_loc - 1.0) * 0.5 + (b_loc - a_loc - 1.0)
    cls = gi * float(PAIRS_PER_GROUP) + pair
    rt = jnp.where(lane == 0, jnp.where(first_low, w1, w2), jnp.where(
        lane == 1, jnp.where(first_low, w2, w1), jnp.where(lane == 2, cls, 0.0)))
    rt_ref[...] = rt
    hf_ref[:, d:] = rt


def _delta_out_kernel(of_ref, ob_ref, z_ref, x_ref, gt_ref, on_ref, wo_ref, gf_ref, shf_ref, scf_ref, wr_ref,
                      x1_ref, hf_ref, rt_ref):
    o = of_ref[0, 0] + ob_ref[0, 0]
    z = z_ref[0].astype(F32)
    onorm = on_ref[...]
    parts = []
    for h in range(N_HEADS):
        oh = o[:, h * HEAD_DIM:(h + 1) * HEAD_DIM]
        parts.append(oh * lax.rsqrt(jnp.mean(oh * oh, axis=-1, keepdims=True) + EPS) * onorm)
    y = jnp.concatenate(parts, axis=1) * _silu(z)
    x1 = x_ref[0] + gt_ref[0] * _dot(y.astype(BF16), wo_ref[...])
    _ffn_pre(x1, gf_ref[...], shf_ref[0], scf_ref[0], wr_ref, x1_ref, hf_ref, rt_ref)


def _delta_out(o2, z, x, gt, onorm, wo, gf, shf, scf, wr, tm):
    b, s, d = x.shape
    t = b * s
    nt = s // tm
    row = lambda bb, i: (bb, i, 0)
    vec = lambda bb, i: (bb, 0, 0)
    full = lambda bb, i: (0, 0)
    flat = lambda bb, i: (bb * nt + i, 0)
    return pl.pallas_call(
        _delta_out_kernel,
        grid=(b, nt),
        in_specs=[pl.BlockSpec((1, 1, tm, d), lambda bb, i: (0, bb, i, 0)),
                  pl.BlockSpec((1, 1, tm, d), lambda bb, i: (1, bb, i, 0)),
                  pl.BlockSpec((1, tm, d), row), pl.BlockSpec((1, tm, d), row),
                  pl.BlockSpec((1, 1, d), vec), pl.BlockSpec((1, HEAD_DIM), full),
                  pl.BlockSpec((d, d), full), pl.BlockSpec((1, d), full),
                  pl.BlockSpec((1, 1, d), vec), pl.BlockSpec((1, 1, d), vec),
                  pl.BlockSpec((d, 2 * LANES), full)],
        out_specs=[pl.BlockSpec((tm, d), flat), pl.BlockSpec((tm, d + LANES), flat), pl.BlockSpec((tm, LANES), flat)],
        out_shape=[jax.ShapeDtypeStruct((t, d), F32), jax.ShapeDtypeStruct((t, d + LANES), F32),
                   jax.ShapeDtypeStruct((t, LANES), F32)],
        compiler_params=_params("parallel", "parallel"),
    )(o2, o2, z, x, gt, onorm, wo, gf, shf, scf, wr)


def _sconv_kernel(xm_ref, xp_ref, xn_ref, sh_ref, sc_ref, g_ref, win_ref, cw_ref, wo_ref, gt_ref,
                  gf_ref, shf_ref, scf_ref, wr_ref, x1_ref, hf_ref, rt_ref, *, tm, ck):
    i = pl.program_id(1)
    n = pl.num_programs(1)
    g, sh, sc = g_ref[...], sh_ref[0], sc_ref[0]
    xm = xm_ref[0]
    hm = _rms_mod(xm, g, sh, sc)
    hp = _rms_mod(xp_ref[0], g, sh, sc) * (i > 0).astype(F32)
    hn = _rms_mod(xn_ref[0], g, sh, sc) * (i < n - 1).astype(F32)
    hm16 = hm.astype(BF16)
    hext = jnp.concatenate([hp, hm, hn], axis=0).astype(BF16)
    d = xm.shape[1]
    col = lax.broadcasted_iota(I32, (tm, 1), 0) % GRID_W
    acc = jnp.zeros((tm, d), F32)
    for j in range(d // ck):
        c0 = j * ck
        gate_b = _dot(hm16, win_ref[:, c0:c0 + ck])
        cw = cw_ref[:, c0:c0 + ck]
        if c0 < d // 2:
            u = _dot(hm16, win_ref[:, d + c0:d + c0 + ck]) * _dot(hm16, win_ref[:, 2 * d + c0:2 * d + c0 + ck])
            left = jnp.where(col == 0, 0.0, pltpu.roll(u, 1, 0))
            right = jnp.where(col == GRID_W - 1, 0.0, pltpu.roll(u, tm - 1, 0))
            y = cw[0:1] * left + cw[1:2] * u + cw[2:3] * right
        else:
            u = _dot(hext, win_ref[:, d + c0:d + c0 + ck]) * _dot(hext, win_ref[:, 2 * d + c0:2 * d + c0 + ck])
            y = cw[0:1] * u[0:tm] + cw[1:2] * u[GRID_W:GRID_W + tm] + cw[2:3] * u[2 * GRID_W:2 * GRID_W + tm]
        acc = acc + _dot((gate_b * y).astype(BF16), wo_ref[c0:c0 + ck, :])
    x1 = xm + gt_ref[0] * acc
    _ffn_pre(x1, gf_ref[...], shf_ref[0], scf_ref[0], wr_ref, x1_ref, hf_ref, rt_ref)


def _sconv(x, sh, sc, gnorm, win, conv_w, wo, gt, gf, shf, scf, wr, tm):
    b, s, d = x.shape
    t = b * s
    nt = s // tm
    hb = tm // GRID_W
    last = s // GRID_W - 1
    row = lambda bb, i: (bb, i, 0)
    vec = lambda bb, i: (bb, 0, 0)
    full = lambda bb, i: (0, 0)
    flat = lambda bb, i: (bb * nt + i, 0)
    return pl.pallas_call(
        functools.partial(_sconv_kernel, tm=tm, ck=256),
        grid=(b, nt),
        in_specs=[pl.BlockSpec((1, tm, d), row),
                  pl.BlockSpec((1, GRID_W, d), lambda bb, i: (bb, jnp.maximum(i * hb - 1, 0), 0)),
                  pl.BlockSpec((1, GRID_W, d), lambda bb, i: (bb, jnp.minimum((i + 1) * hb, last), 0)),
                  pl.BlockSpec((1, 1, d), vec), pl.BlockSpec((1, 1, d), vec), pl.BlockSpec((1, d), full),
                  pl.BlockSpec((d, 3 * d), full), pl.BlockSpec((3, d), full), pl.BlockSpec((d, d), full),
                  pl.BlockSpec((1, 1, d), vec), pl.BlockSpec((1, d), full),
                  pl.BlockSpec((1, 1, d), vec), pl.BlockSpec((1, 1, d), vec),
                  pl.BlockSpec((d, 2 * LANES), full)],
        out_specs=[pl.BlockSpec((tm, d), flat), pl.BlockSpec((tm, d + LANES), flat), pl.BlockSpec((tm, LANES), flat)],
        out_shape=[jax.ShapeDtypeStruct((t, d), F32), jax.ShapeDtypeStruct((t, d + LANES), F32),
                   jax.ShapeDtypeStruct((t, LANES), F32)],
        compiler_params=_params("parallel", "parallel"),
    )(x, x, x, sh, sc, gnorm, win, conv_w, wo, gt, gf, shf, scf, wr)


def _rank_kernel(c_ref, dest_ref, ends_ref, cnt_ref, start_ref, carry_ref, *, tm, bm):
    p = pl.program_id(0)
    i = pl.program_id(1)
    onehot = lax.broadcasted_iota(I32, (LANES, tm), 0) == c_ref[0:1, :]
    ohf = onehot.astype(F32)
    per_class = jnp.broadcast_to(jnp.sum(ohf, axis=1, keepdims=True), (LANES, LANES))

    @pl.when(jnp.logical_and(p == 0, i == 0))
    def _():
        cnt_ref[...] = jnp.zeros_like(cnt_ref)

    @pl.when(p == 0)
    def _():
        cnt_ref[...] += per_class

    @pl.when(jnp.logical_and(p == 1, i == 0))
    def _():
        padded = ((cnt_ref[...].astype(I32) + (bm - 1)) & (-bm)).astype(F32)
        before = (lax.broadcasted_iota(I32, (LANES, LANES), 0) > lax.broadcasted_iota(I32, (LANES, LANES), 1))
        start = _dot(before.astype(F32), padded, HIGHEST)
        start_ref[...] = start
        ends_ref[...] = (start + padded).astype(I32)
        carry_ref[...] = jnp.zeros_like(carry_ref)

    @pl.when(p == 1)
    def _():
        earlier = (lax.broadcasted_iota(I32, (tm, tm), 0) < lax.broadcasted_iota(I32, (tm, tm), 1)).astype(BF16)
        slot = _dot(onehot.astype(BF16), earlier) + carry_ref[:, 0:1] + start_ref[:, 0:1]
        dest = jnp.sum(ohf * slot, axis=0, keepdims=True)
        dest_ref[...] = jnp.concatenate([dest, jnp.zeros((7, tm), F32)], axis=0).astype(I32)
        carry_ref[...] += per_class


def _rank(cls8, tm, bm):
    t = cls8.shape[1]
    return pl.pallas_call(
        functools.partial(_rank_kernel, tm=tm, bm=bm),
        grid=(2, t // tm),
        in_specs=[pl.BlockSpec((8, tm), lambda p, i: (0, i))],
        out_specs=[pl.BlockSpec((8, tm), lambda p, i: (0, i * p)), pl.BlockSpec((LANES, LANES), lambda p, i: (0, 0))],
        out_shape=[jax.ShapeDtypeStruct((8, t), I32), jax.ShapeDtypeStruct((LANES, LANES), I32)],
        scratch_shapes=[pltpu.VMEM((LANES, LANES), F32)] * 3,
        compiler_params=_params("arbitrary", "arbitrary"),
    )(cls8)


def _row_copy(src_ref, src_row, dst_ref, dst_row, sem):
    return pltpu.make_async_copy(src_ref.at[pl.ds(src_row, 1), :], dst_ref.at[pl.ds(dst_row, 1), :], sem)


def _dispatch_kernel(dest_ref, hf_ref, xb_in_ref, xb_ref, sem, *, td):
    del xb_in_ref

    def issue(r, carry):
        _row_copy(hf_ref, r, xb_ref, dest_ref[0, 0, r], sem).start()
        return carry

    lax.fori_loop(0, td, issue, 0, unroll=8)

    def drain(r, carry):
        _row_copy(hf_ref, 0, xb_ref, 0, sem).wait()
        return carry

    lax.fori_loop(0, td, drain, 0, unroll=8)


def _dispatch(dest3, hf, cap, td):
    t, d = hf.shape
    xb0 = jnp.zeros((cap, d), hf.dtype)
    return pl.pallas_call(
        functools.partial(_dispatch_kernel, td=td),
        grid=(t // td,),
        in_specs=[pl.BlockSpec((1, 1, td), lambda i: (i, 0, 0), memory_space=pltpu.SMEM),
                  pl.BlockSpec((td, d), lambda i: (i, 0)),
                  pl.BlockSpec(memory_space=pl.ANY)],
        out_specs=pl.BlockSpec(memory_space=pl.ANY),
        out_shape=jax.ShapeDtypeStruct((cap, d), hf.dtype),
        scratch_shapes=[pltpu.SemaphoreType.DMA],
        input_output_aliases={2: 0},
        compiler_params=_params("arbitrary"),
    )(dest3, hf, xb0)


def _expert_kernel(ea_ref, eb_ref, bs_ref, nu_ref, x_ref, w1a_ref, w3a_ref, w2a_ref, w1b_ref, w3b_ref, w2b_ref, y_ref):
    del ea_ref, eb_ref, bs_ref
    i = pl.program_id(0)
    d = y_ref.shape[1]

    @pl.when(i < nu_ref[0])
    def _():
        x = x_ref[:, :d].astype(BF16)

        def swiglu(w1_ref, w3_ref, w2_ref):
            h = _silu(_dot(x, w1_ref[0])) * _dot(x, w3_ref[0])
            return _dot(h.astype(BF16), w2_ref[0])

        y_ref[...] = (x_ref[:, d:d + 1] * swiglu(w1a_ref, w3a_ref, w2a_ref)
                      + x_ref[:, d + 1:d + 2] * swiglu(w1b_ref, w3b_ref, w2b_ref))

    @pl.when(i >= nu_ref[0])
    def _():
        y_ref[...] = jnp.zeros_like(y_ref)


def _experts(block_ea, block_eb, block_src, n_used, xb, w1, w3, w2, bm):
    cap, dx = xb.shape
    _, d, de = w1.shape
    wa = lambda i, ea, eb, bs, nu: (ea[i], 0, 0)
    wb = lambda i, ea, eb, bs, nu: (eb[i], 0, 0)
    grid_spec = pltpu.PrefetchScalarGridSpec(
        num_scalar_prefetch=4,
        grid=(cap // bm,),
        in_specs=[pl.BlockSpec((bm, dx), lambda i, ea, eb, bs, nu: (bs[i], 0)),
                  pl.BlockSpec((1, d, de), wa), pl.BlockSpec((1, d, de), wa), pl.BlockSpec((1, de, d), wa),
                  pl.BlockSpec((1, d, de), wb), pl.BlockSpec((1, d, de), wb), pl.BlockSpec((1, de, d), wb)],
        out_specs=pl.BlockSpec((bm, d), lambda i, ea, eb, bs, nu: (i, 0)),
    )
    return pl.pallas_call(
        _expert_kernel,
        grid_spec=grid_spec,
        out_shape=jax.ShapeDtypeStruct((cap, d), F32),
        compiler_params=_params("arbitrary"),
    )(block_ea, block_eb, block_src, n_used, xb, w1, w3, w2, w1, w3, w2)


def _combine_kernel(dest_ref, x1_ref, gt_ref, fn_ref, yb_ref, o_ref, buf, sem, *, tc, final):
    def issue(r, carry):
        _row_copy(yb_ref, dest_ref[0, 0, r], buf, r, sem).start()
        return carry

    lax.fori_loop(0, tc, issue, 0, unroll=8)

    def drain(r, carry):
        _row_copy(yb_ref, 0, buf, 0, sem).wait()
        return carry

    lax.fori_loop(0, tc, drain, 0, unroll=8)
    x2 = x1_ref[...] + gt_ref[0] * buf[...]
    if final:
        ms = jnp.mean(x2 * x2, axis=-1, keepdims=True)
        x2 = x2 * lax.rsqrt(ms + EPS) * fn_ref[...]
    o_ref[...] = x2


def _combine(dest3, x1, gt, fnorm, yb, s, tc, final):
    t, d = x1.shape
    per_b = s // tc
    return pl.pallas_call(
        functools.partial(_combine_kernel, tc=tc, final=final),
        grid=(t // tc,),
        in_specs=[pl.BlockSpec((1, 1, tc), lambda i: (i, 0, 0), memory_space=pltpu.SMEM),
                  pl.BlockSpec((tc, d), lambda i: (i, 0)),
                  pl.BlockSpec((1, 1, d), lambda i: (i // per_b, 0, 0)),
                  pl.BlockSpec((1, d), lambda i: (0, 0)),
                  pl.BlockSpec(memory_space=pl.ANY)],
        out_specs=pl.BlockSpec((tc, d), lambda i: (i, 0)),
        out_shape=jax.ShapeDtypeStruct((t, d), F32),
        scratch_shapes=[pltpu.VMEM((tc, d), F32), pltpu.SemaphoreType.DMA],
        compiler_params=_params("arbitrary"),
    )(dest3, x1, gt, fnorm, yb)


def _moe(hfx, rt, x1, gt, fnorm, w1, w3, w2, s, final, bm=128, tr=512, td=256):
    t = hfx.shape[0]
    cls8 = jnp.zeros((8, t), I32).at[0].set(rt[:, 2].astype(I32))
    dest8, ends2 = _rank(cls8, tr, bm)
    dest3 = dest8[0].reshape(t // td, 1, td)
    ends = ends2[:N_CLASSES, 0]
    n_blocks = -(-t // bm) + N_CLASSES
    cap = n_blocks * bm
    n_used = ends[-1] // bm
    block_src = jnp.minimum(jnp.arange(n_blocks, dtype=I32), n_used - 1)
    block_cls = jnp.minimum(jnp.sum((ends[None, :] <= (block_src * bm)[:, None]).astype(I32), axis=1), N_CLASSES - 1)
    block_ea = jnp.asarray(CLASS_EXPERT_A, I32)[block_cls]
    block_eb = jnp.asarray(CLASS_EXPERT_B, I32)[block_cls]
    xb = _dispatch(dest3, hfx, cap, td)
    yb = _experts(block_ea, block_eb, block_src, n_used.reshape(1), xb, w1, w3, w2, bm)
    return _combine(dest3, x1, gt, fnorm, yb, s, td, final)


def _router_weights(router_g, router_e):
    d = router_g.shape[0]
    w = jnp.zeros((d, LANES), F32).at[:, :N_GROUPS].set(router_g).at[:, N_GROUPS:N_GROUPS + N_EXPERTS].set(router_e)
    hi = w.astype(BF16)
    lo = (w - hi.astype(F32)).astype(BF16)
    return jnp.concatenate([hi, lo], axis=1)


def _gate_weights(w_in, a_log, dt_bias):
    d = w_in.shape[0]
    base = 4 * d
    wg = jnp.zeros((d, 2 * LANES), F32)
    gp = jnp.zeros((2, 2 * LANES), F32)
    for direction in range(2):
        o = direction * LANES
        wg = wg.at[:, o:o + N_HEADS].set(w_in[:, base + direction * N_HEADS:base + (direction + 1) * N_HEADS])
        wg = wg.at[:, o + N_HEADS:o + 2 * N_HEADS].set(
            w_in[:, base + (2 + direction) * N_HEADS:base + (3 + direction) * N_HEADS])
        wg = wg.at[:, o + 2 * N_HEADS:o + 3 * N_HEADS].set(
            w_in[:, base + (2 + direction) * N_HEADS:base + (3 + direction) * N_HEADS])
        for rep in (1, 2):
            gp = gp.at[0, o + rep * N_HEADS:o + (rep + 1) * N_HEADS].set(a_log[direction])
            gp = gp.at[1, o + rep * N_HEADS:o + (rep + 1) * N_HEADS].set(dt_bias[direction])
    return wg.astype(BF16), gp


def kernel(x, c, ctx, c_ctx, ada_w, ada_b, norm_mix, norm_ffn, w_in_a, conv_a, a_log_a, dt_bias_a, onorm_a, w_out_a,
           w_in_b, conv_b, w_out_b, router_g, router_e, w1, w3, w2, final_norm):
    b, s, d = x.shape
    chunk = SCAN_CHUNK
    tm = min(512, s)

    cond = jnp.zeros((16, d), F32).at[:b].set(c).at[b].set(c_ctx)
    mod = _ada_params(cond, ada_w, ada_b)

    def mods(layer, rows):
        m = mod[layer, rows]
        return [jnp.broadcast_to(m[:, None, k * d:(k + 1) * d], (b, 1, d)) for k in range(N_MOD)]

    sh_m, sc_m, gt_m, sh_f, sc_f, gt_f = mods(0, slice(0, b))
    csh_m, csc_m = mods(0, slice(b, b + 1))[:2]
    w_in = w_in_a[0]
    wqkv, wz = w_in[:, :3 * d].astype(BF16), w_in[:, 3 * d:4 * d].astype(BF16)
    wg, gp = _gate_weights(w_in, a_log_a[0], dt_bias_a[0])
    gn = norm_mix[0].reshape(1, d)
    qc, kc, vc, _, gc = _delta_in(ctx, csh_m, csc_m, gn, wqkv, conv_a[0], wz, wg, gp, tm)
    s_zero = jnp.zeros((b, 2, N_HEADS, HEAD_DIM, HEAD_DIM), F32)
    (s_ctx,) = _delta_scan(qc, kc, vc, gc, s_zero, chunk, False)
    ql, kl, vl, zl, gl = _delta_in(x, sh_m, sc_m, gn, wqkv, conv_a[0], wz, wg, gp, tm)
    o2, _ = _delta_scan(ql, kl, vl, gl, s_ctx, chunk, True)
    wr = _router_weights(router_g[0], router_e[0])
    x1, hf, rt = _delta_out(o2, zl, x, gt_m, onorm_a[0].reshape(1, HEAD_DIM), w_out_a[0].astype(BF16),
                            norm_ffn[0].reshape(1, d), sh_f, sc_f, wr, tm)
    fn = final_norm.reshape(1, d)
    x2 = _moe(hf, rt, x1, gt_f, fn, w1[0].astype(BF16), w3[0].astype(BF16), w2[0].astype(BF16), s, False)
    x2 = x2.reshape(b, s, d)

    sh_m, sc_m, gt_m, sh_f, sc_f, gt_f = mods(1, slice(0, b))
    wr = _router_weights(router_g[1], router_e[1])
    x1, hf, rt = _sconv(x2, sh_m, sc_m, norm_mix[1].reshape(1, d), w_in_b[0].astype(BF16), conv_b[0],
                        w_out_b[0].astype(BF16), gt_m, norm_ffn[1].reshape(1, d), sh_f, sc_f, wr, tm)
    out = _moe(hf, rt, x1, gt_f, fn, w1[1].astype(BF16), w3[1].astype(BF16), w2[1].astype(BF16), s, True)
    return out.reshape(b, s, d)
```

```python
import functools

import jax
import jax.numpy as jnp
from jax import lax
from jax.experimental import pallas as pl
from jax.experimental.pallas import tpu as pltpu

F32 = jnp.float32
BF16 = jnp.bfloat16
I32 = jnp.int32
HIGHEST = lax.Precision.HIGHEST

EPS = 1e-6
N_HEADS = 8
HEAD_DIM = 128
GRID_W = 64
N_GROUPS = 4
EXPERTS_PER_GROUP = 8
N_EXPERTS = N_GROUPS * EXPERTS_PER_GROUP
PAIRS_PER_GROUP = EXPERTS_PER_GROUP * (EXPERTS_PER_GROUP - 1) // 2
N_CLASSES = N_GROUPS * PAIRS_PER_GROUP
_PAIRS = [(a, b) for a in range(EXPERTS_PER_GROUP) for b in range(a + 1, EXPERTS_PER_GROUP)]
CLASS_EXPERT_A = [g * EXPERTS_PER_GROUP + a for g in range(N_GROUPS) for a, _ in _PAIRS]
CLASS_EXPERT_B = [g * EXPERTS_PER_GROUP + b for g in range(N_GROUPS) for _, b in _PAIRS]
N_MOD = 6
LANES = 128
SCAN_CHUNK = 64
SCAN_CHUNKS_PER_STEP = 2
SEQ_HALO = 8
VMEM_LIMIT = 48 * 1024 * 1024


def _params(*sem):
    return pltpu.CompilerParams(dimension_semantics=sem, vmem_limit_bytes=VMEM_LIMIT)


def _dot(a, b, precision=None):
    return jnp.dot(a, b, preferred_element_type=F32, precision=precision)


def _dot_nt(a, b, precision=None):
    return lax.dot_general(a, b, (((1,), (1,)), ((), ())), preferred_element_type=F32, precision=precision)


def _dot_tn(a, b, precision=None):
    return lax.dot_general(a, b, (((0,), (0,)), ((), ())), preferred_element_type=F32, precision=precision)


def _silu(x):
    return x * jax.nn.sigmoid(x)


def _rms_mod(x, g, shift, scale):
    ms = jnp.mean(x * x, axis=-1, keepdims=True)
    return x * lax.rsqrt(ms + EPS) * g * (1.0 + scale) + shift


def _ada_kernel(c_ref, w_ref, b_ref, o_ref):
    o_ref[0] = _dot(_silu(c_ref[...]), w_ref[0], HIGHEST) + b_ref[0]


def _ada_params(cond, ada_w, ada_b):
    depth, d, n = ada_w.shape
    tn = 512
    r = cond.shape[0]
    return pl.pallas_call(
        _ada_kernel,
        grid=(depth, n // tn),
        in_specs=[pl.BlockSpec((r, d), lambda l, j: (0, 0)),
                  pl.BlockSpec((1, d, tn), lambda l, j: (l, 0, j)),
                  pl.BlockSpec((1, 1, tn), lambda l, j: (l, 0, j))],
        out_specs=pl.BlockSpec((1, r, tn), lambda l, j: (l, 0, j)),
        out_shape=jax.ShapeDtypeStruct((depth, r, n), F32),
        compiler_params=_params("parallel", "parallel"),
    )(cond, ada_w, ada_b.reshape(depth, 1, n))


def _delta_in_kernel(xm_ref, xp_ref, xn_ref, sh_ref, sc_ref, g_ref, wqkv_ref, cw_ref, wz_ref, wg_ref, gp_ref,
                     q_ref, k_ref, v_ref, z_ref, gate_ref, *, tm, ck):
    i = pl.program_id(1)
    n = pl.num_programs(1)
    g, sh, sc = g_ref[...], sh_ref[0], sc_ref[0]
    hm = _rms_mod(xm_ref[0], g, sh, sc)
    hp = _rms_mod(xp_ref[0], g, sh, sc) * (i > 0).astype(F32)
    hn = _rms_mod(xn_ref[0], g, sh, sc) * (i < n - 1).astype(F32)
    hm16 = hm.astype(BF16)
    hext = jnp.concatenate([hp, hm, hn], axis=0).astype(BF16)
    rows = tm + 2 * SEQ_HALO
    d = hm.shape[1]
    outs = (q_ref, k_ref, v_ref)
    for j in range(3 * d // ck):
        p = _dot(hext, wqkv_ref[:, j * ck:(j + 1) * ck])
        cw = cw_ref[:, j * ck:(j + 1) * ck]
        y = cw[0:1] * pltpu.roll(p, 1, 0) + cw[1:2] * p + cw[2:3] * pltpu.roll(p, rows - 1, 0)
        y = _silu(y[SEQ_HALO:SEQ_HALO + tm])
        which, col0 = (j * ck) // d, (j * ck) % d
        for hh in range(ck // HEAD_DIM):
            yh = y[:, hh * HEAD_DIM:(hh + 1) * HEAD_DIM]
            if which < 2:
                yh = yh * lax.rsqrt(jnp.sum(yh * yh, axis=-1, keepdims=True) + EPS)
            if which == 0:
                yh = yh * HEAD_DIM ** -0.5
            c0 = col0 + hh * HEAD_DIM
            outs[which][0, :, c0:c0 + HEAD_DIM] = yh.astype(q_ref.dtype)
    z_ref[0] = _dot(hm16, wz_ref[...]).astype(z_ref.dtype)
    pg = _dot(hm16, wg_ref[...])
    lane = lax.broadcasted_iota(I32, pg.shape, 1) % LANES
    xg = pg + gp_ref[1:2]
    softplus = jnp.maximum(xg, 0.0) + jnp.log1p(jnp.exp(-jnp.abs(xg)))
    gval = -jnp.exp(gp_ref[0:1]) * softplus
    ri = lax.broadcasted_iota(I32, (tm, tm), 0)
    ci = lax.broadcasted_iota(I32, (tm, tm), 1)
    same = (ri // SCAN_CHUNK) == (ci // SCAN_CHUNK)
    g_hi = gval.astype(BF16)
    r1 = gval - g_hi.astype(F32)
    g_mid = r1.astype(BF16)
    g_lo = (r1 - g_mid.astype(F32)).astype(BF16)
    cums = []
    for direction in range(2):
        tri = (same & ((ri >= ci) if direction == 0 else (ri <= ci))).astype(BF16)
        sl = slice(direction * LANES, (direction + 1) * LANES)
        terms = jnp.concatenate([g_hi[:, sl], g_mid[:, sl], g_lo[:, sl]], axis=1)
        cs = _dot(tri, terms)
        cums.append(cs[:, :LANES] + cs[:, LANES:2 * LANES] + cs[:, 2 * LANES:])
    gcum = jnp.concatenate(cums, axis=1)
    gate_ref[0] = jnp.where(lane < N_HEADS, jax.nn.sigmoid(pg), jnp.where(
        lane < 2 * N_HEADS, gval, jnp.where(lane < 3 * N_HEADS, gcum, 0.0)))


def _delta_in(x, sh, sc, gnorm, wqkv, conv_w, wz, wg, gp, tm):
    b, s, d = x.shape
    tm = min(tm, s)
    hb = tm // SEQ_HALO
    last = s // SEQ_HALO - 1
    row = lambda bb, i: (bb, i, 0)
    vec = lambda bb, i: (bb, 0, 0)
    full = lambda bb, i: (0, 0)
    outs = [jax.ShapeDtypeStruct((b, s, d), BF16)] * 4 + [jax.ShapeDtypeStruct((b, s, 2 * LANES), F32)]
    return pl.pallas_call(
        functools.partial(_delta_in_kernel, tm=tm, ck=512),
        grid=(b, s // tm),
        in_specs=[pl.BlockSpec((1, tm, d), row),
                  pl.BlockSpec((1, SEQ_HALO, d), lambda bb, i: (bb, jnp.maximum(i * hb - 1, 0), 0)),
                  pl.BlockSpec((1, SEQ_HALO, d), lambda bb, i: (bb, jnp.minimum((i + 1) * hb, last), 0)),
                  pl.BlockSpec((1, 1, d), vec), pl.BlockSpec((1, 1, d), vec),
                  pl.BlockSpec((1, d), full),
                  pl.BlockSpec((d, 3 * d), full), pl.BlockSpec((3, 3 * d), full),
                  pl.BlockSpec((d, d), full), pl.BlockSpec((d, 2 * LANES), full),
                  pl.BlockSpec((2, 2 * LANES), full)],
        out_specs=[pl.BlockSpec((1, tm, d), row)] * 4 + [pl.BlockSpec((1, tm, 2 * LANES), row)],
        out_shape=outs,
        compiler_params=_params("parallel", "parallel"),
    )(x, x, x, sh, sc, gnorm, wqkv, conv_w, wz, wg, gp)


def _scan_kernel(qf_ref, kf_ref, vf_ref, gf_ref, qb_ref, kb_ref, vb_ref, gb_ref, s0_ref, *out_refs, c, cps,
                 with_output):
    if with_output:
        of_ref, ob_ref, s_ref = out_refs
    else:
        (s_ref,) = out_refs
        of_ref = ob_ref = None
    step = pl.program_id(1)

    @pl.when(step == 0)
    def _():
        s_ref[...] = s0_ref[...]

    in_refs = ((qf_ref, kf_ref, vf_ref, gf_ref, of_ref), (qb_ref, kb_ref, vb_ref, gb_ref, ob_ref))
    heads = range(N_HEADS)
    ri = lax.broadcasted_iota(I32, (c, c), 0)
    ci = lax.broadcasted_iota(I32, (c, c), 1)
    eye = (ri == ci).astype(F32)
    incl = (ri >= ci, ri <= ci)
    strict = (ri > ci, ri < ci)

    def rows(dd, j):
        r0 = (j if dd == 0 else cps - 1 - j) * c
        return slice(r0, r0 + c)

    def lanes(h):
        return slice(h * HEAD_DIM, (h + 1) * HEAD_DIM)

    chunks = [(dd, j) for dd in range(2) for j in range(cps)]
    chains = [(dd, j, h) for dd, j in chunks for h in heads]
    gates = {dj: in_refs[dj[0]][3][0, rows(*dj), :] for dj in chunks}
    gates_t = {dj: jnp.concatenate([gates[dj], jnp.zeros((LANES - c, LANES), F32)], axis=0).T[:, :c]
               for dj in chunks}
    tot_row = {dj: gates[dj][c - 1:c, :] if dj[0] == 0 else gates[dj][0:1, :] for dj in chunks}

    def load(which, key):
        dd, j, h = key
        return in_refs[dd][which][0, rows(dd, j), lanes(h)]

    q16 = {key: load(0, key) for key in chains}
    k16 = {key: load(1, key) for key in chains}
    qf = {key: q16[key].astype(F32) for key in chains}
    kf = {key: k16[key].astype(F32) for key in chains}
    vf = {key: load(2, key).astype(F32) for key in chains}
    cum = 2 * N_HEADS
    beta = {(dd, j, h): gates[(dd, j)][:, h:h + 1] for dd, j, h in chains}
    gcc = {(dd, j, h): gates[(dd, j)][:, cum + h:cum + h + 1] for dd, j, h in chains}
    gcr = {(dd, j, h): gates_t[(dd, j)][cum + h:cum + h + 1, :] for dd, j, h in chains}
    tot = {(dd, j, h): tot_row[(dd, j)][:, cum + h:cum + h + 1] for dd, j, h in chains}
    decay = {key: jnp.exp(jnp.where(incl[key[0]], gcc[key] - gcr[key], -jnp.inf)) for key in chains}
    e = {key: jnp.exp(gcc[key]) for key in chains}
    kb = {key: kf[key] * beta[key] for key in chains}
    a = {key: _dot_nt(jnp.concatenate([kb[key].astype(BF16), q16[key]], axis=0), k16[key]) for key in chains}
    lmat = {key: jnp.where(strict[key[0]], a[key][:c] * decay[key], 0.0) for key in chains}
    qk = {key: (a[key][c:] * decay[key]).astype(BF16) for key in chains}
    p = {key: eye - lmat[key] for key in chains}
    l16 = {key: lmat[key].astype(BF16) for key in chains}
    m = {key: _dot(l16[key], l16[key]) for key in chains}
    span = 2
    while span < c:
        span *= 2
        m16 = {key: m[key].astype(BF16) for key in chains}
        if span < c:
            pm = {key: _dot(jnp.concatenate([p[key], m[key]], axis=0).astype(BF16), m16[key]) for key in chains}
            p = {key: p[key] + pm[key][:c] for key in chains}
            m = {key: pm[key][c:] for key in chains}
        else:
            p = {key: p[key] + _dot(p[key].astype(BF16), m16[key]) for key in chains}
    rhs = {key: jnp.concatenate([vf[key] * beta[key], kb[key] * e[key]], axis=1).astype(BF16) for key in chains}
    sol = {key: _dot(p[key].astype(BF16), rhs[key]) for key in chains}
    wq = {key: jnp.concatenate([sol[key][:, HEAD_DIM:], qf[key] * e[key]], axis=0).astype(BF16) for key in chains}
    k_dec = {key: (kf[key] * jnp.exp(tot[key] - gcc[key])).astype(BF16) for key in chains}
    s_tot = {key: jnp.exp(tot[key]) for key in chains}

    state = {(dd, h): s_ref[0, dd, h] for dd in range(2) for h in heads}
    for j in range(cps):
        now = [(dd, j, h) for dd in range(2) for h in heads]
        s16 = {key: state[(key[0], key[2])].astype(BF16) for key in now}
        ws = {key: _dot(wq[key], s16[key]) for key in now}
        v16 = {key: (sol[key][:, :HEAD_DIM] - ws[key][:c]).astype(BF16) for key in now}
        for key in now:
            sk = (key[0], key[2])
            state[sk] = state[sk] * s_tot[key] + _dot_tn(k_dec[key], v16[key])
        if with_output:
            o = {key: ws[key][c:] + _dot(qk[key], v16[key]) for key in now}
            for dd, _, h in now:
                in_refs[dd][4][0, rows(dd, j), lanes(h)] = o[(dd, j, h)]
    for dd in range(2):
        for h in heads:
            s_ref[0, dd, h] = state[(dd, h)]


def _delta_scan(q, k, v, gates, s0, c, cps, with_output):
    b, s, d = q.shape
    rb = c * cps
    nb = s // rb
    fwd = lambda bb, i: (bb, i, 0)
    bwd = lambda bb, i: (bb, nb - 1 - i, 0)
    state_spec = pl.BlockSpec((1, 2, N_HEADS, HEAD_DIM, HEAD_DIM), lambda bb, i: (bb, 0, 0, 0, 0))
    state_shape = jax.ShapeDtypeStruct((b, 2, N_HEADS, HEAD_DIM, HEAD_DIM), F32)
    out_specs, out_shape = [state_spec], [state_shape]
    if with_output:
        out_specs = [pl.BlockSpec((1, rb, d), fwd), pl.BlockSpec((1, rb, d), bwd)] + out_specs
        out_shape = [jax.ShapeDtypeStruct((b, s, d), F32)] * 2 + out_shape
    return pl.pallas_call(
        functools.partial(_scan_kernel, c=c, cps=cps, with_output=with_output),
        grid=(b, nb),
        in_specs=([pl.BlockSpec((1, rb, d), fwd)] * 3 + [pl.BlockSpec((1, rb, LANES), fwd)]
                  + [pl.BlockSpec((1, rb, d), bwd)] * 3
                  + [pl.BlockSpec((1, rb, LANES), lambda bb, i: (bb, nb - 1 - i, 1)), state_spec]),
        out_specs=out_specs,
        out_shape=out_shape,
        compiler_params=_params("parallel", "arbitrary"),
    )(q, k, v, gates, q, k, v, gates, s0)


def _ffn_pre(x1, gf, shf, scf, wr_ref, x1_ref, hf_ref, rt_ref):
    x1_ref[...] = x1
    hf = _rms_mod(x1, gf, shf, scf)
    d = hf.shape[1]
    hf_ref[:, :d] = hf
    hi = hf.astype(BF16)
    lo = (hf - hi.astype(F32)).astype(BF16)
    a = _dot(hi, wr_ref[...])
    logits = a[:, :LANES] + a[:, LANES:] + _dot(lo, wr_ref[:, :LANES])
    lane = lax.broadcasted_iota(I32, logits.shape, 1)
    lanef = lane.astype(F32)
    big = float(LANES)
    gl = jnp.where(lane < N_GROUPS, logits, -jnp.inf)
    gmax = jnp.max(gl, axis=-1, keepdims=True)
    gi = jnp.min(jnp.where(gl == gmax, lanef, big), axis=-1, keepdims=True)
    pg_sel = 1.0 / jnp.sum(jnp.exp(gl - gmax), axis=-1, keepdims=True)
    rel = lanef - (N_GROUPS + EXPERTS_PER_GROUP * gi)
    el = jnp.where((rel >= 0.0) & (rel < float(EXPERTS_PER_GROUP)), logits, -jnp.inf)
    m1 = jnp.max(el, axis=-1, keepdims=True)
    i1 = jnp.min(jnp.where(el == m1, lanef, big), axis=-1, keepdims=True)
    el2 = jnp.where(lanef == i1, -jnp.inf, el)
    m2 = jnp.max(el2, axis=-1, keepdims=True)
    i2 = jnp.min(jnp.where(el2 == m2, lanef, big), axis=-1, keepdims=True)
    r2 = jnp.exp(m2 - m1)
    w1 = pg_sel / (1.0 + r2)
    w2 = pg_sel * r2 / (1.0 + r2)
    first_low = i1 < i2
    a_loc = jnp.minimum(i1, i2) - (N_GROUPS + EXPERTS_PER_GROUP * gi)
    b_loc = jnp.maximum(i1, i2) - (N_GROUPS + EXPERTS_PER_GROUP * gi)
    pair = a_loc * (EXPERTS_PER_GROUP - 1) - a_loc * (a_loc - 1.0) * 0.5 + (b_loc - a_loc - 1.0)
    cls = gi * float(PAIRS_PER_GROUP) + pair
    rt = jnp.where(lane == 0, jnp.where(first_low, w1, w2), jnp.where(
        lane == 1, jnp.where(first_low, w2, w1), jnp.where(lane == 2, cls, 0.0)))
    rt_ref[...] = rt
    hf_ref[:, d:] = rt


def _delta_out_kernel(of_ref, ob_ref, z_ref, x_ref, gt_ref, on_ref, wo_ref, gf_ref, shf_ref, scf_ref, wr_ref,
                      x1_ref, hf_ref, rt_ref):
    o = of_ref[0] + ob_ref[0]
    z = z_ref[0].astype(F32)
    onorm = on_ref[...]
    parts = []
    for h in range(N_HEADS):
        oh = o[:, h * HEAD_DIM:(h + 1) * HEAD_DIM]
        parts.append(oh * lax.rsqrt(jnp.mean(oh * oh, axis=-1, keepdims=True) + EPS) * onorm)
    y = jnp.concatenate(parts, axis=1) * _silu(z)
    x1 = x_ref[0] + gt_ref[0] * _dot(y.astype(BF16), wo_ref[...])
    _ffn_pre(x1, gf_ref[...], shf_ref[0], scf_ref[0], wr_ref, x1_ref, hf_ref, rt_ref)


def _delta_out(o_f, o_b, z, x, gt, onorm, wo, gf, shf, scf, wr, tm):
    b, s, d = x.shape
    t = b * s
    nt = s // tm
    row = lambda bb, i: (bb, i, 0)
    vec = lambda bb, i: (bb, 0, 0)
    full = lambda bb, i: (0, 0)
    flat = lambda bb, i: (bb * nt + i, 0)
    return pl.pallas_call(
        _delta_out_kernel,
        grid=(b, nt),
        in_specs=[pl.BlockSpec((1, tm, d), row), pl.BlockSpec((1, tm, d), row),
                  pl.BlockSpec((1, tm, d), row), pl.BlockSpec((1, tm, d), row),
                  pl.BlockSpec((1, 1, d), vec), pl.BlockSpec((1, HEAD_DIM), full),
                  pl.BlockSpec((d, d), full), pl.BlockSpec((1, d), full),
                  pl.BlockSpec((1, 1, d), vec), pl.BlockSpec((1, 1, d), vec),
                  pl.BlockSpec((d, 2 * LANES), full)],
        out_specs=[pl.BlockSpec((tm, d), flat), pl.BlockSpec((tm, d + LANES), flat), pl.BlockSpec((tm, LANES), flat)],
        out_shape=[jax.ShapeDtypeStruct((t, d), F32), jax.ShapeDtypeStruct((t, d + LANES), F32),
                   jax.ShapeDtypeStruct((t, LANES), F32)],
        compiler_params=_params("parallel", "parallel"),
    )(o_f, o_b, z, x, gt, onorm, wo, gf, shf, scf, wr)


def _sconv_kernel(xm_ref, xp_ref, xn_ref, sh_ref, sc_ref, g_ref, win_ref, cw_ref, wo_ref, gt_ref,
                  gf_ref, shf_ref, scf_ref, wr_ref, x1_ref, hf_ref, rt_ref, *, tm, ck):
    i = pl.program_id(1)
    n = pl.num_programs(1)
    g, sh, sc = g_ref[...], sh_ref[0], sc_ref[0]
    xm = xm_ref[0]
    hm = _rms_mod(xm, g, sh, sc)
    hp = _rms_mod(xp_ref[0], g, sh, sc) * (i > 0).astype(F32)
    hn = _rms_mod(xn_ref[0], g, sh, sc) * (i < n - 1).astype(F32)
    hm16 = hm.astype(BF16)
    hext = jnp.concatenate([hp, hm, hn], axis=0).astype(BF16)
    d = xm.shape[1]
    col = lax.broadcasted_iota(I32, (tm, 1), 0) % GRID_W
    acc = jnp.zeros((tm, d), F32)
    for j in range(d // ck):
        c0 = j * ck
        gate_b = _dot(hm16, win_ref[:, c0:c0 + ck])
        cw = cw_ref[:, c0:c0 + ck]
        if c0 < d // 2:
            u = _dot(hm16, win_ref[:, d + c0:d + c0 + ck]) * _dot(hm16, win_ref[:, 2 * d + c0:2 * d + c0 + ck])
            left = jnp.where(col == 0, 0.0, pltpu.roll(u, 1, 0))
            right = jnp.where(col == GRID_W - 1, 0.0, pltpu.roll(u, tm - 1, 0))
            y = cw[0:1] * left + cw[1:2] * u + cw[2:3] * right
        else:
            u = _dot(hext, win_ref[:, d + c0:d + c0 + ck]) * _dot(hext, win_ref[:, 2 * d + c0:2 * d + c0 + ck])
            y = cw[0:1] * u[0:tm] + cw[1:2] * u[GRID_W:GRID_W + tm] + cw[2:3] * u[2 * GRID_W:2 * GRID_W + tm]
        acc = acc + _dot((gate_b * y).astype(BF16), wo_ref[c0:c0 + ck, :])
    x1 = xm + gt_ref[0] * acc
    _ffn_pre(x1, gf_ref[...], shf_ref[0], scf_ref[0], wr_ref, x1_ref, hf_ref, rt_ref)


def _sconv(x, sh, sc, gnorm, win, conv_w, wo, gt, gf, shf, scf, wr, tm):
    b, s, d = x.shape
    t = b * s
    nt = s // tm
    hb = tm // GRID_W
    last = s // GRID_W - 1
    row = lambda bb, i: (bb, i, 0)
    vec = lambda bb, i: (bb, 0, 0)
    full = lambda bb, i: (0, 0)
    flat = lambda bb, i: (bb * nt + i, 0)
    return pl.pallas_call(
        functools.partial(_sconv_kernel, tm=tm, ck=256),
        grid=(b, nt),
        in_specs=[pl.BlockSpec((1, tm, d), row),
                  pl.BlockSpec((1, GRID_W, d), lambda bb, i: (bb, jnp.maximum(i * hb - 1, 0), 0)),
                  pl.BlockSpec((1, GRID_W, d), lambda bb, i: (bb, jnp.minimum((i + 1) * hb, last), 0)),
                  pl.BlockSpec((1, 1, d), vec), pl.BlockSpec((1, 1, d), vec), pl.BlockSpec((1, d), full),
                  pl.BlockSpec((d, 3 * d), full), pl.BlockSpec((3, d), full), pl.BlockSpec((d, d), full),
                  pl.BlockSpec((1, 1, d), vec), pl.BlockSpec((1, d), full),
                  pl.BlockSpec((1, 1, d), vec), pl.BlockSpec((1, 1, d), vec),
                  pl.BlockSpec((d, 2 * LANES), full)],
        out_specs=[pl.BlockSpec((tm, d), flat), pl.BlockSpec((tm, d + LANES), flat), pl.BlockSpec((tm, LANES), flat)],
        out_shape=[jax.ShapeDtypeStruct((t, d), F32), jax.ShapeDtypeStruct((t, d + LANES), F32),
                   jax.ShapeDtypeStruct((t, LANES), F32)],
        compiler_params=_params("parallel", "parallel"),
    )(x, x, x, sh, sc, gnorm, win, conv_w, wo, gt, gf, shf, scf, wr)


def _rank_kernel(c_ref, dest_ref, ends_ref, cnt_ref, start_ref, carry_ref, *, tm, bm):
    p = pl.program_id(0)
    i = pl.program_id(1)
    onehot = lax.broadcasted_iota(I32, (LANES, tm), 0) == c_ref[0:1, :]
    ohf = onehot.astype(F32)
    per_class = jnp.broadcast_to(jnp.sum(ohf, axis=1, keepdims=True), (LANES, LANES))

    @pl.when(jnp.logical_and(p == 0, i == 0))
    def _():
        cnt_ref[...] = jnp.zeros_like(cnt_ref)

    @pl.when(p == 0)
    def _():
        cnt_ref[...] += per_class

    @pl.when(jnp.logical_and(p == 1, i == 0))
    def _():
        padded = ((cnt_ref[...].astype(I32) + (bm - 1)) & (-bm)).astype(F32)
        before = (lax.broadcasted_iota(I32, (LANES, LANES), 0) > lax.broadcasted_iota(I32, (LANES, LANES), 1))
        start = _dot(before.astype(F32), padded, HIGHEST)
        start_ref[...] = start
        ends_ref[...] = (start + padded).astype(I32)
        carry_ref[...] = jnp.zeros_like(carry_ref)

    @pl.when(p == 1)
    def _():
        earlier = (lax.broadcasted_iota(I32, (tm, tm), 0) < lax.broadcasted_iota(I32, (tm, tm), 1)).astype(BF16)
        slot = _dot(onehot.astype(BF16), earlier) + carry_ref[:, 0:1] + start_ref[:, 0:1]
        dest = jnp.sum(ohf * slot, axis=0, keepdims=True)
        dest_ref[...] = jnp.concatenate([dest, jnp.zeros((7, tm), F32)], axis=0).astype(I32)
        carry_ref[...] += per_class


def _rank(cls8, tm, bm):
    t = cls8.shape[1]
    return pl.pallas_call(
        functools.partial(_rank_kernel, tm=tm, bm=bm),
        grid=(2, t // tm),
        in_specs=[pl.BlockSpec((8, tm), lambda p, i: (0, i))],
        out_specs=[pl.BlockSpec((8, tm), lambda p, i: (0, i * p)), pl.BlockSpec((LANES, LANES), lambda p, i: (0, 0))],
        out_shape=[jax.ShapeDtypeStruct((8, t), I32), jax.ShapeDtypeStruct((LANES, LANES), I32)],
        scratch_shapes=[pltpu.VMEM((LANES, LANES), F32)] * 3,
        compiler_params=_params("arbitrary", "arbitrary"),
    )(cls8)


def _row_copy(src_ref, src_row, dst_ref, dst_row, sem):
    return pltpu.make_async_copy(src_ref.at[pl.ds(src_row, 1), :], dst_ref.at[pl.ds(dst_row, 1), :], sem)


def _dispatch_kernel(dest_ref, hf_ref, xb_in_ref, xb_ref, sem, *, td):
    del xb_in_ref

    def issue(r, carry):
        _row_copy(hf_ref, r, xb_ref, dest_ref[0, 0, r], sem).start()
        return carry

    lax.fori_loop(0, td, issue, 0, unroll=8)

    def drain(r, carry):
        _row_copy(hf_ref, 0, xb_ref, 0, sem).wait()
        return carry

    lax.fori_loop(0, td, drain, 0, unroll=8)


def _dispatch(dest3, hf, cap, td):
    t, d = hf.shape
    xb0 = jnp.zeros((cap, d), hf.dtype)
    return pl.pallas_call(
        functools.partial(_dispatch_kernel, td=td),
        grid=(t // td,),
        in_specs=[pl.BlockSpec((1, 1, td), lambda i: (i, 0, 0), memory_space=pltpu.SMEM),
                  pl.BlockSpec((td, d), lambda i: (i, 0)),
                  pl.BlockSpec(memory_space=pl.ANY)],
        out_specs=pl.BlockSpec(memory_space=pl.ANY),
        out_shape=jax.ShapeDtypeStruct((cap, d), hf.dtype),
        scratch_shapes=[pltpu.SemaphoreType.DMA],
        input_output_aliases={2: 0},
        compiler_params=_params("arbitrary"),
    )(dest3, hf, xb0)


def _expert_kernel(ea_ref, eb_ref, bs_ref, nu_ref, x_ref, w1a_ref, w3a_ref, w2a_ref, w1b_ref, w3b_ref, w2b_ref, y_ref):
    del ea_ref, eb_ref, bs_ref
    i = pl.program_id(0)
    d = y_ref.shape[1]

    @pl.when(i < nu_ref[0])
    def _():
        x = x_ref[:, :d].astype(BF16)

        def swiglu(w1_ref, w3_ref, w2_ref):
            h = _silu(_dot(x, w1_ref[0])) * _dot(x, w3_ref[0])
            return _dot(h.astype(BF16), w2_ref[0])

        y_ref[...] = (x_ref[:, d:d + 1] * swiglu(w1a_ref, w3a_ref, w2a_ref)
                      + x_ref[:, d + 1:d + 2] * swiglu(w1b_ref, w3b_ref, w2b_ref))

    @pl.when(i >= nu_ref[0])
    def _():
        y_ref[...] = jnp.zeros_like(y_ref)


def _experts(block_ea, block_eb, block_src, n_used, xb, w1, w3, w2, bm):
    cap, dx = xb.shape
    _, d, de = w1.shape
    wa = lambda i, ea, eb, bs, nu: (ea[i], 0, 0)
    wb = lambda i, ea, eb, bs, nu: (eb[i], 0, 0)
    grid_spec = pltpu.PrefetchScalarGridSpec(
        num_scalar_prefetch=4,
        grid=(cap // bm,),
        in_specs=[pl.BlockSpec((bm, dx), lambda i, ea, eb, bs, nu: (bs[i], 0)),
                  pl.BlockSpec((1, d, de), wa), pl.BlockSpec((1, d, de), wa), pl.BlockSpec((1, de, d), wa),
                  pl.BlockSpec((1, d, de), wb), pl.BlockSpec((1, d, de), wb), pl.BlockSpec((1, de, d), wb)],
        out_specs=pl.BlockSpec((bm, d), lambda i, ea, eb, bs, nu: (i, 0)),
    )
    return pl.pallas_call(
        _expert_kernel,
        grid_spec=grid_spec,
        out_shape=jax.ShapeDtypeStruct((cap, d), F32),
        compiler_params=_params("arbitrary"),
    )(block_ea, block_eb, block_src, n_used, xb, w1, w3, w2, w1, w3, w2)


def _combine_kernel(dest_ref, x1_ref, gt_ref, fn_ref, yb_ref, o_ref, buf, sem, *, tc, final):
    def issue(r, carry):
        _row_copy(yb_ref, dest_ref[0, 0, r], buf, r, sem).start()
        return carry

    lax.fori_loop(0, tc, issue, 0, unroll=8)

    def drain(r, carry):
        _row_copy(yb_ref, 0, buf, 0, sem).wait()
        return carry

    lax.fori_loop(0, tc, drain, 0, unroll=8)
    x2 = x1_ref[...] + gt_ref[0] * buf[...]
    if final:
        ms = jnp.mean(x2 * x2, axis=-1, keepdims=True)
        x2 = x2 * lax.rsqrt(ms + EPS) * fn_ref[...]
    o_ref[...] = x2


def _combine(dest3, x1, gt, fnorm, yb, s, tc, final):
    t, d = x1.shape
    per_b = s // tc
    return pl.pallas_call(
        functools.partial(_combine_kernel, tc=tc, final=final),
        grid=(t // tc,),
        in_specs=[pl.BlockSpec((1, 1, tc), lambda i: (i, 0, 0), memory_space=pltpu.SMEM),
                  pl.BlockSpec((tc, d), lambda i: (i, 0)),
                  pl.BlockSpec((1, 1, d), lambda i: (i // per_b, 0, 0)),
                  pl.BlockSpec((1, d), lambda i: (0, 0)),
                  pl.BlockSpec(memory_space=pl.ANY)],
        out_specs=pl.BlockSpec((tc, d), lambda i: (i, 0)),
        out_shape=jax.ShapeDtypeStruct((t, d), F32),
        scratch_shapes=[pltpu.VMEM((tc, d), F32), pltpu.SemaphoreType.DMA],
        compiler_params=_params("arbitrary"),
    )(dest3, x1, gt, fnorm, yb)


def _moe(hfx, rt, x1, gt, fnorm, w1, w3, w2, s, final, bm=256, tr=512, td=256):
    t = hfx.shape[0]
    cls8 = jnp.zeros((8, t), I32).at[0].set(rt[:, 2].astype(I32))
    dest8, ends2 = _rank(cls8, tr, bm)
    dest3 = dest8[0].reshape(t // td, 1, td)
    ends = ends2[:N_CLASSES, 0]
    n_blocks = -(-t // bm) + N_CLASSES
    cap = n_blocks * bm
    n_used = ends[-1] // bm
    block_src = jnp.maximum(jnp.minimum(jnp.arange(n_blocks, dtype=I32), n_used - 1), 0)
    block_cls = jnp.minimum(jnp.sum((ends[None, :] <= (block_src * bm)[:, None]).astype(I32), axis=1), N_CLASSES - 1)
    block_ea = jnp.asarray(CLASS_EXPERT_A, I32)[block_cls]
    block_eb = jnp.asarray(CLASS_EXPERT_B, I32)[block_cls]
    xb = _dispatch(dest3, hfx, cap, td)
    yb = _experts(block_ea, block_eb, block_src, n_used.reshape(1), xb, w1, w3, w2, bm)
    return _combine(dest3, x1, gt, fnorm, yb, s, td, final)


def _router_weights(router_g, router_e):
    d = router_g.shape[0]
    w = jnp.zeros((d, LANES), F32).at[:, :N_GROUPS].set(router_g).at[:, N_GROUPS:N_GROUPS + N_EXPERTS].set(router_e)
    hi = w.astype(BF16)
    lo = (w - hi.astype(F32)).astype(BF16)
    return jnp.concatenate([hi, lo], axis=1)


def _gate_weights(w_in, a_log, dt_bias):
    d = w_in.shape[0]
    base = 4 * d
    wg = jnp.zeros((d, 2 * LANES), F32)
    gp = jnp.zeros((2, 2 * LANES), F32)
    for direction in range(2):
        o = direction * LANES
        wg = wg.at[:, o:o + N_HEADS].set(w_in[:, base + direction * N_HEADS:base + (direction + 1) * N_HEADS])
        wg = wg.at[:, o + N_HEADS:o + 2 * N_HEADS].set(
            w_in[:, base + (2 + direction) * N_HEADS:base + (3 + direction) * N_HEADS])
        wg = wg.at[:, o + 2 * N_HEADS:o + 3 * N_HEADS].set(
            w_in[:, base + (2 + direction) * N_HEADS:base + (3 + direction) * N_HEADS])
        for rep in (1, 2):
            gp = gp.at[0, o + rep * N_HEADS:o + (rep + 1) * N_HEADS].set(a_log[direction])
            gp = gp.at[1, o + rep * N_HEADS:o + (rep + 1) * N_HEADS].set(dt_bias[direction])
    return wg.astype(BF16), gp


def kernel(x, c, ctx, c_ctx, ada_w, ada_b, norm_mix, norm_ffn, w_in_a, conv_a, a_log_a, dt_bias_a, onorm_a, w_out_a,
           w_in_b, conv_b, w_out_b, router_g, router_e, w1, w3, w2, final_norm):
    b, s, d = x.shape
    chunk = SCAN_CHUNK
    tm = min(512, s)

    cond = jnp.zeros((16, d), F32).at[:b].set(c).at[b].set(c_ctx)
    mod = _ada_params(cond, ada_w, ada_b)

    def mods(layer, rows):
        m = mod[layer, rows]
        return [jnp.broadcast_to(m[:, None, k * d:(k + 1) * d], (b, 1, d)) for k in range(N_MOD)]

    sh_m, sc_m, gt_m, sh_f, sc_f, gt_f = mods(0, slice(0, b))
    csh_m, csc_m = mods(0, slice(b, b + 1))[:2]
    w_in = w_in_a[0]
    wqkv, wz = w_in[:, :3 * d].astype(BF16), w_in[:, 3 * d:4 * d].astype(BF16)
    wg, gp = _gate_weights(w_in, a_log_a[0], dt_bias_a[0])
    gn = norm_mix[0].reshape(1, d)
    qc, kc, vc, _, gc = _delta_in(ctx, csh_m, csc_m, gn, wqkv, conv_a[0], wz, wg, gp, tm)
    s_zero = jnp.zeros((b, 2, N_HEADS, HEAD_DIM, HEAD_DIM), F32)
    (s_ctx,) = _delta_scan(qc, kc, vc, gc, s_zero, chunk, SCAN_CHUNKS_PER_STEP, False)
    ql, kl, vl, zl, gl = _delta_in(x, sh_m, sc_m, gn, wqkv, conv_a[0], wz, wg, gp, tm)
    o_f, o_b, _ = _delta_scan(ql, kl, vl, gl, s_ctx, chunk, SCAN_CHUNKS_PER_STEP, True)
    wr = _router_weights(router_g[0], router_e[0])
    x1, hf, rt = _delta_out(o_f, o_b, zl, x, gt_m, onorm_a[0].reshape(1, HEAD_DIM), w_out_a[0].astype(BF16),
                            norm_ffn[0].reshape(1, d), sh_f, sc_f, wr, tm)
    fn = final_norm.reshape(1, d)
    x2 = _moe(hf, rt, x1, gt_f, fn, w1[0].astype(BF16), w3[0].astype(BF16), w2[0].astype(BF16), s, False)
    x2 = x2.reshape(b, s, d)

    sh_m, sc_m, gt_m, sh_f, sc_f, gt_f = mods(1, slice(0, b))
    wr = _router_weights(router_g[1], router_e[1])
    x1, hf, rt = _sconv(x2, sh_m, sc_m, norm_mix[1].reshape(1, d), w_in_b[0].astype(BF16), conv_b[0],
                        w_out_b[0].astype(BF16), gt_m, norm_ffn[1].reshape(1, d), sh_f, sc_f, wr, tm)
    out = _moe(hf, rt, x1, gt_f, fn, w1[1].astype(BF16), w3[1].astype(BF16), w2[1].astype(BF16), s, True)
    return out.reshape(b, s, d)
```

```python
import functools

import jax
import jax.numpy as jnp
from jax import lax
from jax.experimental import pallas as pl
from jax.experimental.pallas import tpu as pltpu

F32 = jnp.float32
BF16 = jnp.bfloat16
I32 = jnp.int32
HIGHEST = lax.Precision.HIGHEST

EPS = 1e-6
N_HEADS = 8
HEAD_DIM = 128
GRID_W = 64
N_GROUPS = 4
EXPERTS_PER_GROUP = 8
N_EXPERTS = N_GROUPS * EXPERTS_PER_GROUP
PAIRS_PER_GROUP = EXPERTS_PER_GROUP * (EXPERTS_PER_GROUP - 1) // 2
N_CLASSES = N_GROUPS * PAIRS_PER_GROUP
_PAIRS = [(a, b) for a in range(EXPERTS_PER_GROUP) for b in range(a + 1, EXPERTS_PER_GROUP)]
CLASS_EXPERT_A = [g * EXPERTS_PER_GROUP + a for g in range(N_GROUPS) for a, _ in _PAIRS]
CLASS_EXPERT_B = [g * EXPERTS_PER_GROUP + b for g in range(N_GROUPS) for _, b in _PAIRS]
N_MOD = 6
LANES = 128
SCAN_CHUNK = 64
SCAN_CHUNKS_PER_STEP = 2
SEQ_HALO = 8
VMEM_LIMIT = 48 * 1024 * 1024


def _params(*sem):
    return pltpu.CompilerParams(dimension_semantics=sem, vmem_limit_bytes=VMEM_LIMIT)


def _dot(a, b, precision=None):
    return jnp.dot(a, b, preferred_element_type=F32, precision=precision)


def _dot_nt(a, b, precision=None):
    return lax.dot_general(a, b, (((1,), (1,)), ((), ())), preferred_element_type=F32, precision=precision)


def _dot_tn(a, b, precision=None):
    return lax.dot_general(a, b, (((0,), (0,)), ((), ())), preferred_element_type=F32, precision=precision)


def _silu(x):
    return x * jax.nn.sigmoid(x)


def _rms_mod(x, g, shift, scale):
    ms = jnp.mean(x * x, axis=-1, keepdims=True)
    return x * lax.rsqrt(ms + EPS) * g * (1.0 + scale) + shift


def _ada_kernel(c_ref, w_ref, b_ref, o_ref):
    o_ref[0] = _dot(_silu(c_ref[...]), w_ref[0], HIGHEST) + b_ref[0]


def _ada_params(cond, ada_w, ada_b):
    depth, d, n = ada_w.shape
    tn = 512
    r = cond.shape[0]
    return pl.pallas_call(
        _ada_kernel,
        grid=(depth, n // tn),
        in_specs=[pl.BlockSpec((r, d), lambda l, j: (0, 0)),
                  pl.BlockSpec((1, d, tn), lambda l, j: (l, 0, j)),
                  pl.BlockSpec((1, 1, tn), lambda l, j: (l, 0, j))],
        out_specs=pl.BlockSpec((1, r, tn), lambda l, j: (l, 0, j)),
        out_shape=jax.ShapeDtypeStruct((depth, r, n), F32),
        compiler_params=_params("parallel", "parallel"),
    )(cond, ada_w, ada_b.reshape(depth, 1, n))


def _delta_in_kernel(xm_ref, xp_ref, xn_ref, sh_ref, sc_ref, g_ref, wqkv_ref, cw_ref, wz_ref, wg_ref, gp_ref,
                     q_ref, k_ref, v_ref, z_ref, gate_ref, *, tm, ck):
    i = pl.program_id(1)
    n = pl.num_programs(1)
    g, sh, sc = g_ref[...], sh_ref[0], sc_ref[0]
    hm = _rms_mod(xm_ref[0], g, sh, sc)
    hp = _rms_mod(xp_ref[0], g, sh, sc) * (i > 0).astype(F32)
    hn = _rms_mod(xn_ref[0], g, sh, sc) * (i < n - 1).astype(F32)
    hm16 = hm.astype(BF16)
    hext = jnp.concatenate([hp, hm, hn], axis=0).astype(BF16)
    rows = tm + 2 * SEQ_HALO
    d = hm.shape[1]
    outs = (q_ref, k_ref, v_ref)
    for j in range(3 * d // ck):
        p = _dot(hext, wqkv_ref[:, j * ck:(j + 1) * ck])
        cw = cw_ref[:, j * ck:(j + 1) * ck]
        y = cw[0:1] * pltpu.roll(p, 1, 0) + cw[1:2] * p + cw[2:3] * pltpu.roll(p, rows - 1, 0)
        y = _silu(y[SEQ_HALO:SEQ_HALO + tm])
        which, col0 = (j * ck) // d, (j * ck) % d
        for hh in range(ck // HEAD_DIM):
            yh = y[:, hh * HEAD_DIM:(hh + 1) * HEAD_DIM]
            if which < 2:
                yh = yh * lax.rsqrt(jnp.sum(yh * yh, axis=-1, keepdims=True) + EPS)
            if which == 0:
                yh = yh * HEAD_DIM ** -0.5
            c0 = col0 + hh * HEAD_DIM
            outs[which][0, :, c0:c0 + HEAD_DIM] = yh.astype(q_ref.dtype)
    z_ref[0] = _dot(hm16, wz_ref[...]).astype(z_ref.dtype)
    pg = _dot(hm16, wg_ref[...])
    lane = lax.broadcasted_iota(I32, pg.shape, 1) % LANES
    xg = pg + gp_ref[1:2]
    softplus = jnp.maximum(xg, 0.0) + jnp.log1p(jnp.exp(-jnp.abs(xg)))
    gval = -jnp.exp(gp_ref[0:1]) * softplus
    ri = lax.broadcasted_iota(I32, (tm, tm), 0)
    ci = lax.broadcasted_iota(I32, (tm, tm), 1)
    same = (ri // SCAN_CHUNK) == (ci // SCAN_CHUNK)
    g_hi = gval.astype(BF16)
    r1 = gval - g_hi.astype(F32)
    g_mid = r1.astype(BF16)
    g_lo = (r1 - g_mid.astype(F32)).astype(BF16)
    cums = []
    for direction in range(2):
        tri = (same & ((ri >= ci) if direction == 0 else (ri <= ci))).astype(BF16)
        sl = slice(direction * LANES, (direction + 1) * LANES)
        terms = jnp.concatenate([g_hi[:, sl], g_mid[:, sl], g_lo[:, sl]], axis=1)
        cs = _dot(tri, terms)
        cums.append(cs[:, :LANES] + cs[:, LANES:2 * LANES] + cs[:, 2 * LANES:])
    gcum = jnp.concatenate(cums, axis=1)
    gate_ref[0] = jnp.where(lane < N_HEADS, jax.nn.sigmoid(pg), jnp.where(
        lane < 2 * N_HEADS, gval, jnp.where(lane < 3 * N_HEADS, gcum, 0.0)))


def _delta_in(x, sh, sc, gnorm, wqkv, conv_w, wz, wg, gp, tm):
    b, s, d = x.shape
    tm = min(tm, s)
    hb = tm // SEQ_HALO
    last = s // SEQ_HALO - 1
    row = lambda bb, i: (bb, i, 0)
    vec = lambda bb, i: (bb, 0, 0)
    full = lambda bb, i: (0, 0)
    outs = [jax.ShapeDtypeStruct((b, s, d), BF16)] * 4 + [jax.ShapeDtypeStruct((b, s, 2 * LANES), F32)]
    return pl.pallas_call(
        functools.partial(_delta_in_kernel, tm=tm, ck=512),
        grid=(b, s // tm),
        in_specs=[pl.BlockSpec((1, tm, d), row),
                  pl.BlockSpec((1, SEQ_HALO, d), lambda bb, i: (bb, jnp.maximum(i * hb - 1, 0), 0)),
                  pl.BlockSpec((1, SEQ_HALO, d), lambda bb, i: (bb, jnp.minimum((i + 1) * hb, last), 0)),
                  pl.BlockSpec((1, 1, d), vec), pl.BlockSpec((1, 1, d), vec),
                  pl.BlockSpec((1, d), full),
                  pl.BlockSpec((d, 3 * d), full), pl.BlockSpec((3, 3 * d), full),
                  pl.BlockSpec((d, d), full), pl.BlockSpec((d, 2 * LANES), full),
                  pl.BlockSpec((2, 2 * LANES), full)],
        out_specs=[pl.BlockSpec((1, tm, d), row)] * 4 + [pl.BlockSpec((1, tm, 2 * LANES), row)],
        out_shape=outs,
        compiler_params=_params("parallel", "parallel"),
    )(x, x, x, sh, sc, gnorm, wqkv, conv_w, wz, wg, gp)


def _scan_kernel(qf_ref, kf_ref, vf_ref, gf_ref, qb_ref, kb_ref, vb_ref, gb_ref, s0_ref, *out_refs, c, cps,
                 with_output):
    if with_output:
        of_ref, ob_ref, s_ref = out_refs
    else:
        (s_ref,) = out_refs
        of_ref = ob_ref = None
    step = pl.program_id(1)

    @pl.when(step == 0)
    def _():
        s_ref[...] = s0_ref[...]

    in_refs = ((qf_ref, kf_ref, vf_ref, gf_ref, of_ref), (qb_ref, kb_ref, vb_ref, gb_ref, ob_ref))
    heads = range(N_HEADS)
    ri = lax.broadcasted_iota(I32, (c, c), 0)
    ci = lax.broadcasted_iota(I32, (c, c), 1)
    eye = (ri == ci).astype(F32)
    incl = (ri >= ci, ri <= ci)
    strict = (ri > ci, ri < ci)

    def rows(dd, j):
        r0 = (j if dd == 0 else cps - 1 - j) * c
        return slice(r0, r0 + c)

    def lanes(h):
        return slice(h * HEAD_DIM, (h + 1) * HEAD_DIM)

    chunks = [(dd, j) for dd in range(2) for j in range(cps)]
    chains = [(dd, j, h) for dd, j in chunks for h in heads]
    gates = {dj: in_refs[dj[0]][3][0, rows(*dj), :] for dj in chunks}
    gates_t = {dj: jnp.concatenate([gates[dj], jnp.zeros((LANES - c, LANES), F32)], axis=0).T[:, :c]
               for dj in chunks}
    tot_row = {dj: gates[dj][c - 1:c, :] if dj[0] == 0 else gates[dj][0:1, :] for dj in chunks}

    def load(which, key):
        dd, j, h = key
        return in_refs[dd][which][0, rows(dd, j), lanes(h)]

    q16 = {key: load(0, key) for key in chains}
    k16 = {key: load(1, key) for key in chains}
    qf = {key: q16[key].astype(F32) for key in chains}
    kf = {key: k16[key].astype(F32) for key in chains}
    vf = {key: load(2, key).astype(F32) for key in chains}
    cum = 2 * N_HEADS
    beta = {(dd, j, h): gates[(dd, j)][:, h:h + 1] for dd, j, h in chains}
    gcc = {(dd, j, h): gates[(dd, j)][:, cum + h:cum + h + 1] for dd, j, h in chains}
    gcr = {(dd, j, h): gates_t[(dd, j)][cum + h:cum + h + 1, :] for dd, j, h in chains}
    tot = {(dd, j, h): tot_row[(dd, j)][:, cum + h:cum + h + 1] for dd, j, h in chains}
    decay = {key: jnp.exp(jnp.where(incl[key[0]], gcc[key] - gcr[key], -jnp.inf)) for key in chains}
    e = {key: jnp.exp(gcc[key]) for key in chains}
    kb = {key: kf[key] * beta[key] for key in chains}
    a = {key: _dot_nt(jnp.concatenate([kb[key].astype(BF16), q16[key]], axis=0), k16[key]) for key in chains}
    lmat = {key: jnp.where(strict[key[0]], a[key][:c] * decay[key], 0.0) for key in chains}
    qk = {key: (a[key][c:] * decay[key]).astype(BF16) for key in chains}
    p = {key: eye - lmat[key] for key in chains}
    l16 = {key: lmat[key].astype(BF16) for key in chains}
    m = {key: _dot(l16[key], l16[key]) for key in chains}
    span = 2
    while span < c:
        span *= 2
        m16 = {key: m[key].astype(BF16) for key in chains}
        if span < c:
            pm = {key: _dot(jnp.concatenate([p[key], m[key]], axis=0).astype(BF16), m16[key]) for key in chains}
            p = {key: p[key] + pm[key][:c] for key in chains}
            m = {key: pm[key][c:] for key in chains}
        else:
            p = {key: p[key] + _dot(p[key].astype(BF16), m16[key]) for key in chains}
    rhs = {key: jnp.concatenate([vf[key] * beta[key], kb[key] * e[key]], axis=1).astype(BF16) for key in chains}
    sol = {key: _dot(p[key].astype(BF16), rhs[key]) for key in chains}
    wq = {key: jnp.concatenate([sol[key][:, HEAD_DIM:], qf[key] * e[key]], axis=0).astype(BF16) for key in chains}
    k_dec = {key: (kf[key] * jnp.exp(tot[key] - gcc[key])).astype(BF16) for key in chains}
    s_tot = {key: jnp.exp(tot[key]) for key in chains}

    state = {(dd, h): s_ref[0, dd, h] for dd in range(2) for h in heads}
    for j in range(cps):
        now = [(dd, j, h) for dd in range(2) for h in heads]
        s16 = {key: state[(key[0], key[2])].astype(BF16) for key in now}
        ws = {key: _dot(wq[key], s16[key]) for key in now}
        v16 = {key: (sol[key][:, :HEAD_DIM] - ws[key][:c]).astype(BF16) for key in now}
        for key in now:
            sk = (key[0], key[2])
            state[sk] = state[sk] * s_tot[key] + _dot_tn(k_dec[key], v16[key])
        if with_output:
            o = {key: ws[key][c:] + _dot(qk[key], v16[key]) for key in now}
            for dd, _, h in now:
                in_refs[dd][4][0, rows(dd, j), lanes(h)] = o[(dd, j, h)].astype(of_ref.dtype)
    for dd in range(2):
        for h in heads:
            s_ref[0, dd, h] = state[(dd, h)]


def _delta_scan(q, k, v, gates, s0, c, cps, with_output):
    b, s, d = q.shape
    rb = c * cps
    nb = s // rb
    fwd = lambda bb, i: (bb, i, 0)
    bwd = lambda bb, i: (bb, nb - 1 - i, 0)
    state_spec = pl.BlockSpec((1, 2, N_HEADS, HEAD_DIM, HEAD_DIM), lambda bb, i: (bb, 0, 0, 0, 0))
    state_shape = jax.ShapeDtypeStruct((b, 2, N_HEADS, HEAD_DIM, HEAD_DIM), F32)
    out_specs, out_shape = [state_spec], [state_shape]
    if with_output:
        out_specs = [pl.BlockSpec((1, rb, d), fwd), pl.BlockSpec((1, rb, d), bwd)] + out_specs
        out_shape = [jax.ShapeDtypeStruct((b, s, d), BF16)] * 2 + out_shape
    return pl.pallas_call(
        functools.partial(_scan_kernel, c=c, cps=cps, with_output=with_output),
        grid=(b, nb),
        in_specs=([pl.BlockSpec((1, rb, d), fwd)] * 3 + [pl.BlockSpec((1, rb, LANES), fwd)]
                  + [pl.BlockSpec((1, rb, d), bwd)] * 3
                  + [pl.BlockSpec((1, rb, LANES), lambda bb, i: (bb, nb - 1 - i, 1)), state_spec]),
        out_specs=out_specs,
        out_shape=out_shape,
        compiler_params=_params("parallel", "arbitrary"),
    )(q, k, v, gates, q, k, v, gates, s0)


def _ffn_pre(x1, gf, shf, scf, wr_ref, x1_ref, hf_ref, rt_ref):
    x1_ref[...] = x1
    hf = _rms_mod(x1, gf, shf, scf)
    d = hf.shape[1]
    hf_ref[:, :d] = hf
    hi = hf.astype(BF16)
    lo = (hf - hi.astype(F32)).astype(BF16)
    a = _dot(hi, wr_ref[...])
    logits = a[:, :LANES] + a[:, LANES:] + _dot(lo, wr_ref[:, :LANES])
    lane = lax.broadcasted_iota(I32, logits.shape, 1)
    lanef = lane.astype(F32)
    big = float(LANES)
    gl = jnp.where(lane < N_GROUPS, logits, -jnp.inf)
    gmax = jnp.max(gl, axis=-1, keepdims=True)
    gi = jnp.min(jnp.where(gl == gmax, lanef, big), axis=-1, keepdims=True)
    pg_sel = 1.0 / jnp.sum(jnp.exp(gl - gmax), axis=-1, keepdims=True)
    rel = lanef - (N_GROUPS + EXPERTS_PER_GROUP * gi)
    el = jnp.where((rel >= 0.0) & (rel < float(EXPERTS_PER_GROUP)), logits, -jnp.inf)
    m1 = jnp.max(el, axis=-1, keepdims=True)
    i1 = jnp.min(jnp.where(el == m1, lanef, big), axis=-1, keepdims=True)
    el2 = jnp.where(lanef == i1, -jnp.inf, el)
    m2 = jnp.max(el2, axis=-1, keepdims=True)
    i2 = jnp.min(jnp.where(el2 == m2, lanef, big), axis=-1, keepdims=True)
    r2 = jnp.exp(m2 - m1)
    w1 = pg_sel / (1.0 + r2)
    w2 = pg_sel * r2 / (1.0 + r2)
    first_low = i1 < i2
    a_loc = jnp.minimum(i1, i2) - (N_GROUPS + EXPERTS_PER_GROUP * gi)
    b_loc = jnp.maximum(i1, i2) - (N_GROUPS + EXPERTS_PER_GROUP * gi)
    pair = a_loc * (EXPERTS_PER_GROUP - 1) - a_loc * (a_loc - 1.0) * 0.5 + (b_loc - a_loc - 1.0)
    cls = gi * float(PAIRS_PER_GROUP) + pair
    rt = jnp.where(lane == 0, jnp.where(first_low, w1, w2), jnp.where(
        lane == 1, jnp.where(first_low, w2, w1), jnp.where(lane == 2, cls, 0.0)))
    rt_ref[...] = rt
    hf_ref[:, d:] = rt


def _delta_out_kernel(of_ref, ob_ref, z_ref, x_ref, gt_ref, on_ref, wo_ref, gf_ref, shf_ref, scf_ref, wr_ref,
                      x1_ref, hf_ref, rt_ref):
    o = of_ref[0].astype(F32) + ob_ref[0].astype(F32)
    z = z_ref[0].astype(F32)
    onorm = on_ref[...]
    parts = []
    for h in range(N_HEADS):
        oh = o[:, h * HEAD_DIM:(h + 1) * HEAD_DIM]
        parts.append(oh * lax.rsqrt(jnp.mean(oh * oh, axis=-1, keepdims=True) + EPS) * onorm)
    y = jnp.concatenate(parts, axis=1) * _silu(z)
    x1 = x_ref[0] + gt_ref[0] * _dot(y.astype(BF16), wo_ref[...])
    _ffn_pre(x1, gf_ref[...], shf_ref[0], scf_ref[0], wr_ref, x1_ref, hf_ref, rt_ref)


def _delta_out(o_f, o_b, z, x, gt, onorm, wo, gf, shf, scf, wr, tm):
    b, s, d = x.shape
    t = b * s
    nt = s // tm
    row = lambda bb, i: (bb, i, 0)
    vec = lambda bb, i: (bb, 0, 0)
    full = lambda bb, i: (0, 0)
    flat = lambda bb, i: (bb * nt + i, 0)
    return pl.pallas_call(
        _delta_out_kernel,
        grid=(b, nt),
        in_specs=[pl.BlockSpec((1, tm, d), row), pl.BlockSpec((1, tm, d), row),
                  pl.BlockSpec((1, tm, d), row), pl.BlockSpec((1, tm, d), row),
                  pl.BlockSpec((1, 1, d), vec), pl.BlockSpec((1, HEAD_DIM), full),
                  pl.BlockSpec((d, d), full), pl.BlockSpec((1, d), full),
                  pl.BlockSpec((1, 1, d), vec), pl.BlockSpec((1, 1, d), vec),
                  pl.BlockSpec((d, 2 * LANES), full)],
        out_specs=[pl.BlockSpec((tm, d), flat), pl.BlockSpec((tm, d + LANES), flat), pl.BlockSpec((tm, LANES), flat)],
        out_shape=[jax.ShapeDtypeStruct((t, d), F32), jax.ShapeDtypeStruct((t, d + LANES), F32),
                   jax.ShapeDtypeStruct((t, LANES), F32)],
        compiler_params=_params("parallel", "parallel"),
    )(o_f, o_b, z, x, gt, onorm, wo, gf, shf, scf, wr)


def _sconv_kernel(xm_ref, xp_ref, xn_ref, sh_ref, sc_ref, g_ref, win_ref, cw_ref, wo_ref, gt_ref,
                  gf_ref, shf_ref, scf_ref, wr_ref, x1_ref, hf_ref, rt_ref, *, tm, ck):
    i = pl.program_id(1)
    n = pl.num_programs(1)
    g, sh, sc = g_ref[...], sh_ref[0], sc_ref[0]
    xm = xm_ref[0]
    hm = _rms_mod(xm, g, sh, sc)
    hp = _rms_mod(xp_ref[0], g, sh, sc) * (i > 0).astype(F32)
    hn = _rms_mod(xn_ref[0], g, sh, sc) * (i < n - 1).astype(F32)
    hm16 = hm.astype(BF16)
    hext = jnp.concatenate([hp, hm, hn], axis=0).astype(BF16)
    d = xm.shape[1]
    col = lax.broadcasted_iota(I32, (tm, 1), 0) % GRID_W
    acc = jnp.zeros((tm, d), F32)
    for j in range(d // ck):
        c0 = j * ck
        gate_b = _dot(hm16, win_ref[:, c0:c0 + ck])
        cw = cw_ref[:, c0:c0 + ck]
        if c0 < d // 2:
            u = _dot(hm16, win_ref[:, d + c0:d + c0 + ck]) * _dot(hm16, win_ref[:, 2 * d + c0:2 * d + c0 + ck])
            left = jnp.where(col == 0, 0.0, pltpu.roll(u, 1, 0))
            right = jnp.where(col == GRID_W - 1, 0.0, pltpu.roll(u, tm - 1, 0))
            y = cw[0:1] * left + cw[1:2] * u + cw[2:3] * right
        else:
            u = _dot(hext, win_ref[:, d + c0:d + c0 + ck]) * _dot(hext, win_ref[:, 2 * d + c0:2 * d + c0 + ck])
            y = cw[0:1] * u[0:tm] + cw[1:2] * u[GRID_W:GRID_W + tm] + cw[2:3] * u[2 * GRID_W:2 * GRID_W + tm]
        acc = acc + _dot((gate_b * y).astype(BF16), wo_ref[c0:c0 + ck, :])
    x1 = xm + gt_ref[0] * acc
    _ffn_pre(x1, gf_ref[...], shf_ref[0], scf_ref[0], wr_ref, x1_ref, hf_ref, rt_ref)


def _sconv(x, sh, sc, gnorm, win, conv_w, wo, gt, gf, shf, scf, wr, tm):
    b, s, d = x.shape
    t = b * s
    nt = s // tm
    hb = tm // GRID_W
    last = s // GRID_W - 1
    row = lambda bb, i: (bb, i, 0)
    vec = lambda bb, i: (bb, 0, 0)
    full = lambda bb, i: (0, 0)
    flat = lambda bb, i: (bb * nt + i, 0)
    return pl.pallas_call(
        functools.partial(_sconv_kernel, tm=tm, ck=256),
        grid=(b, nt),
        in_specs=[pl.BlockSpec((1, tm, d), row),
                  pl.BlockSpec((1, GRID_W, d), lambda bb, i: (bb, jnp.maximum(i * hb - 1, 0), 0)),
                  pl.BlockSpec((1, GRID_W, d), lambda bb, i: (bb, jnp.minimum((i + 1) * hb, last), 0)),
                  pl.BlockSpec((1, 1, d), vec), pl.BlockSpec((1, 1, d), vec), pl.BlockSpec((1, d), full),
                  pl.BlockSpec((d, 3 * d), full), pl.BlockSpec((3, d), full), pl.BlockSpec((d, d), full),
                  pl.BlockSpec((1, 1, d), vec), pl.BlockSpec((1, d), full),
                  pl.BlockSpec((1, 1, d), vec), pl.BlockSpec((1, 1, d), vec),
                  pl.BlockSpec((d, 2 * LANES), full)],
        out_specs=[pl.BlockSpec((tm, d), flat), pl.BlockSpec((tm, d + LANES), flat), pl.BlockSpec((tm, LANES), flat)],
        out_shape=[jax.ShapeDtypeStruct((t, d), F32), jax.ShapeDtypeStruct((t, d + LANES), F32),
                   jax.ShapeDtypeStruct((t, LANES), F32)],
        compiler_params=_params("parallel", "parallel"),
    )(x, x, x, sh, sc, gnorm, win, conv_w, wo, gt, gf, shf, scf, wr)


def _rank_kernel(c_ref, dest_ref, ends_ref, cnt_ref, start_ref, carry_ref, *, tm, bm):
    p = pl.program_id(0)
    i = pl.program_id(1)
    onehot = lax.broadcasted_iota(I32, (LANES, tm), 0) == c_ref[0:1, :]
    ohf = onehot.astype(F32)
    per_class = jnp.broadcast_to(jnp.sum(ohf, axis=1, keepdims=True), (LANES, LANES))

    @pl.when(jnp.logical_and(p == 0, i == 0))
    def _():
        cnt_ref[...] = jnp.zeros_like(cnt_ref)

    @pl.when(p == 0)
    def _():
        cnt_ref[...] += per_class

    @pl.when(jnp.logical_and(p == 1, i == 0))
    def _():
        padded = ((cnt_ref[...].astype(I32) + (bm - 1)) & (-bm)).astype(F32)
        before = (lax.broadcasted_iota(I32, (LANES, LANES), 0) > lax.broadcasted_iota(I32, (LANES, LANES), 1))
        start = _dot(before.astype(F32), padded, HIGHEST)
        start_ref[...] = start
        ends_ref[...] = (start + padded).astype(I32)
        carry_ref[...] = jnp.zeros_like(carry_ref)

    @pl.when(p == 1)
    def _():
        earlier = (lax.broadcasted_iota(I32, (tm, tm), 0) < lax.broadcasted_iota(I32, (tm, tm), 1)).astype(BF16)
        slot = _dot(onehot.astype(BF16), earlier) + carry_ref[:, 0:1] + start_ref[:, 0:1]
        dest = jnp.sum(ohf * slot, axis=0, keepdims=True)
        dest_ref[...] = jnp.concatenate([dest, jnp.zeros((7, tm), F32)], axis=0).astype(I32)
        carry_ref[...] += per_class


def _rank(cls8, tm, bm):
    t = cls8.shape[1]
    return pl.pallas_call(
        functools.partial(_rank_kernel, tm=tm, bm=bm),
        grid=(2, t // tm),
        in_specs=[pl.BlockSpec((8, tm), lambda p, i: (0, i))],
        out_specs=[pl.BlockSpec((8, tm), lambda p, i: (0, i * p)), pl.BlockSpec((LANES, LANES), lambda p, i: (0, 0))],
        out_shape=[jax.ShapeDtypeStruct((8, t), I32), jax.ShapeDtypeStruct((LANES, LANES), I32)],
        scratch_shapes=[pltpu.VMEM((LANES, LANES), F32)] * 3,
        compiler_params=_params("arbitrary", "arbitrary"),
    )(cls8)


def _row_copy(src_ref, src_row, dst_ref, dst_row, sem):
    return pltpu.make_async_copy(src_ref.at[pl.ds(src_row, 1), :], dst_ref.at[pl.ds(dst_row, 1), :], sem)


def _dispatch_kernel(dest_ref, hf_ref, xb_in_ref, xb_ref, sem, *, td):
    del xb_in_ref

    def issue(r, carry):
        _row_copy(hf_ref, r, xb_ref, dest_ref[0, 0, r], sem).start()
        return carry

    for r in range(td):
        issue(r, 0)

    def drain(r, carry):
        _row_copy(hf_ref, 0, xb_ref, 0, sem).wait()
        return carry

    lax.fori_loop(0, td, drain, 0, unroll=8)


def _dispatch(dest3, hf, cap, td):
    t, d = hf.shape
    xb0 = jnp.zeros((cap, d), hf.dtype)
    return pl.pallas_call(
        functools.partial(_dispatch_kernel, td=td),
        grid=(t // td,),
        in_specs=[pl.BlockSpec((1, 1, td), lambda i: (i, 0, 0), memory_space=pltpu.SMEM),
                  pl.BlockSpec((td, d), lambda i: (i, 0)),
                  pl.BlockSpec(memory_space=pl.ANY)],
        out_specs=pl.BlockSpec(memory_space=pl.ANY),
        out_shape=jax.ShapeDtypeStruct((cap, d), hf.dtype),
        scratch_shapes=[pltpu.SemaphoreType.DMA],
        input_output_aliases={2: 0},
        compiler_params=_params("arbitrary"),
    )(dest3, hf, xb0)


def _expert_kernel(ea_ref, eb_ref, bs_ref, nu_ref, x_ref, w1a_ref, w3a_ref, w2a_ref, w1b_ref, w3b_ref, w2b_ref, y_ref,
                   *bf16_weights):
    del bs_ref
    i = pl.program_id(0)
    d = y_ref.shape[1]
    prev = jnp.maximum(i - 1, 0)
    for e_ref, srcs, dsts in ((ea_ref, (w1a_ref, w3a_ref, w2a_ref), bf16_weights[:3]),
                              (eb_ref, (w1b_ref, w3b_ref, w2b_ref), bf16_weights[3:])):
        @pl.when(jnp.logical_or(i == 0, e_ref[i] != e_ref[prev]))
        def _(srcs=srcs, dsts=dsts):
            for src, dst in zip(srcs, dsts):
                dst[...] = src[0].astype(BF16)

    @pl.when(i < nu_ref[0])
    def _():
        x = x_ref[:, :d].astype(BF16)

        def swiglu(w1, w3, w2):
            h = _silu(_dot(x, w1[...])) * _dot(x, w3[...])
            return _dot(h.astype(BF16), w2[...])

        y_ref[...] = (x_ref[:, d:d + 1] * swiglu(*bf16_weights[:3])
                      + x_ref[:, d + 1:d + 2] * swiglu(*bf16_weights[3:]))

    @pl.when(i >= nu_ref[0])
    def _():
        y_ref[...] = jnp.zeros_like(y_ref)


def _experts(block_ea, block_eb, block_src, n_used, xb, w1, w3, w2, bm):
    cap, dx = xb.shape
    _, d, de = w1.shape
    wa = lambda i, ea, eb, bs, nu: (ea[i], 0, 0)
    wb = lambda i, ea, eb, bs, nu: (eb[i], 0, 0)
    grid_spec = pltpu.PrefetchScalarGridSpec(
        num_scalar_prefetch=4,
        grid=(cap // bm,),
        in_specs=[pl.BlockSpec((bm, dx), lambda i, ea, eb, bs, nu: (bs[i], 0)),
                  pl.BlockSpec((1, d, de), wa), pl.BlockSpec((1, d, de), wa), pl.BlockSpec((1, de, d), wa),
                  pl.BlockSpec((1, d, de), wb), pl.BlockSpec((1, d, de), wb), pl.BlockSpec((1, de, d), wb)],
        out_specs=pl.BlockSpec((bm, d), lambda i, ea, eb, bs, nu: (i, 0)),
        scratch_shapes=[pltpu.VMEM((d, de), BF16), pltpu.VMEM((d, de), BF16), pltpu.VMEM((de, d), BF16)] * 2,
    )
    return pl.pallas_call(
        _expert_kernel,
        grid_spec=grid_spec,
        out_shape=jax.ShapeDtypeStruct((cap, d), F32),
        compiler_params=_params("arbitrary"),
    )(block_ea, block_eb, block_src, n_used, xb, w1, w3, w2, w1, w3, w2)


def _combine_kernel(dest_ref, x1_ref, gt_ref, fn_ref, yb_ref, o_ref, buf, sem, *, tc, final):
    def issue(r, carry):
        _row_copy(yb_ref, dest_ref[0, 0, r], buf, r, sem).start()
        return carry

    for r in range(tc):
        issue(r, 0)

    def drain(r, carry):
        _row_copy(yb_ref, 0, buf, 0, sem).wait()
        return carry

    lax.fori_loop(0, tc, drain, 0, unroll=8)
    x2 = x1_ref[...] + gt_ref[0] * buf[...]
    if final:
        ms = jnp.mean(x2 * x2, axis=-1, keepdims=True)
        x2 = x2 * lax.rsqrt(ms + EPS) * fn_ref[...]
    o_ref[...] = x2


def _combine(dest3, x1, gt, fnorm, yb, s, tc, final):
    t, d = x1.shape
    per_b = s // tc
    return pl.pallas_call(
        functools.partial(_combine_kernel, tc=tc, final=final),
        grid=(t // tc,),
        in_specs=[pl.BlockSpec((1, 1, tc), lambda i: (i, 0, 0), memory_space=pltpu.SMEM),
                  pl.BlockSpec((tc, d), lambda i: (i, 0)),
                  pl.BlockSpec((1, 1, d), lambda i: (i // per_b, 0, 0)),
                  pl.BlockSpec((1, d), lambda i: (0, 0)),
                  pl.BlockSpec(memory_space=pl.ANY)],
        out_specs=pl.BlockSpec((tc, d), lambda i: (i, 0)),
        out_shape=jax.ShapeDtypeStruct((t, d), F32),
        scratch_shapes=[pltpu.VMEM((tc, d), F32), pltpu.SemaphoreType.DMA],
        compiler_params=_params("arbitrary"),
    )(dest3, x1, gt, fnorm, yb)


def _moe(hfx, rt, x1, gt, fnorm, w1, w3, w2, s, final, bm=256, tr=512, td=1024):
    t = hfx.shape[0]
    cls8 = jnp.zeros((8, t), I32).at[0].set(rt[:, 2].astype(I32))
    dest8, ends2 = _rank(cls8, tr, bm)
    dest3 = dest8[0].reshape(t // td, 1, td)
    ends = ends2[:N_CLASSES, 0]
    n_blocks = -(-t // bm) + N_CLASSES
    cap = n_blocks * bm
    n_used = ends[-1] // bm
    block_src = jnp.maximum(jnp.minimum(jnp.arange(n_blocks, dtype=I32), n_used - 1), 0)
    block_cls = jnp.minimum(jnp.sum((ends[None, :] <= (block_src * bm)[:, None]).astype(I32), axis=1), N_CLASSES - 1)
    block_ea = jnp.asarray(CLASS_EXPERT_A, I32)[block_cls]
    block_eb = jnp.asarray(CLASS_EXPERT_B, I32)[block_cls]
    xb = _dispatch(dest3, hfx, cap, td)
    yb = _experts(block_ea, block_eb, block_src, n_used.reshape(1), xb, w1, w3, w2, bm)
    return _combine(dest3, x1, gt, fnorm, yb, s, td, final)


def _router_weights(router_g, router_e):
    d = router_g.shape[0]
    w = jnp.zeros((d, LANES), F32).at[:, :N_GROUPS].set(router_g).at[:, N_GROUPS:N_GROUPS + N_EXPERTS].set(router_e)
    hi = w.astype(BF16)
    lo = (w - hi.astype(F32)).astype(BF16)
    return jnp.concatenate([hi, lo], axis=1)


def _gate_weights(w_in, a_log, dt_bias):
    d = w_in.shape[0]
    base = 4 * d
    wg = jnp.zeros((d, 2 * LANES), F32)
    gp = jnp.zeros((2, 2 * LANES), F32)
    for direction in range(2):
        o = direction * LANES
        wg = wg.at[:, o:o + N_HEADS].set(w_in[:, base + direction * N_HEADS:base + (direction + 1) * N_HEADS])
        wg = wg.at[:, o + N_HEADS:o + 2 * N_HEADS].set(
            w_in[:, base + (2 + direction) * N_HEADS:base + (3 + direction) * N_HEADS])
        wg = wg.at[:, o + 2 * N_HEADS:o + 3 * N_HEADS].set(
            w_in[:, base + (2 + direction) * N_HEADS:base + (3 + direction) * N_HEADS])
        for rep in (1, 2):
            gp = gp.at[0, o + rep * N_HEADS:o + (rep + 1) * N_HEADS].set(a_log[direction])
            gp = gp.at[1, o + rep * N_HEADS:o + (rep + 1) * N_HEADS].set(dt_bias[direction])
    return wg.astype(BF16), gp


def kernel(x, c, ctx, c_ctx, ada_w, ada_b, norm_mix, norm_ffn, w_in_a, conv_a, a_log_a, dt_bias_a, onorm_a, w_out_a,
           w_in_b, conv_b, w_out_b, router_g, router_e, w1, w3, w2, final_norm):
    b, s, d = x.shape
    chunk = SCAN_CHUNK
    tm = min(512, s)

    cond = jnp.zeros((16, d), F32).at[:b].set(c).at[b].set(c_ctx)
    mod = _ada_params(cond, ada_w, ada_b)

    def mods(layer, rows):
        m = mod[layer, rows]
        return [jnp.broadcast_to(m[:, None, k * d:(k + 1) * d], (b, 1, d)) for k in range(N_MOD)]

    sh_m, sc_m, gt_m, sh_f, sc_f, gt_f = mods(0, slice(0, b))
    csh_m, csc_m = mods(0, slice(b, b + 1))[:2]
    w_in = w_in_a[0]
    wqkv, wz = w_in[:, :3 * d].astype(BF16), w_in[:, 3 * d:4 * d].astype(BF16)
    wg, gp = _gate_weights(w_in, a_log_a[0], dt_bias_a[0])
    gn = norm_mix[0].reshape(1, d)
    qc, kc, vc, _, gc = _delta_in(ctx, csh_m, csc_m, gn, wqkv, conv_a[0], wz, wg, gp, tm)
    s_zero = jnp.zeros((b, 2, N_HEADS, HEAD_DIM, HEAD_DIM), F32)
    (s_ctx,) = _delta_scan(qc, kc, vc, gc, s_zero, chunk, SCAN_CHUNKS_PER_STEP, False)
    ql, kl, vl, zl, gl = _delta_in(x, sh_m, sc_m, gn, wqkv, conv_a[0], wz, wg, gp, tm)
    o_f, o_b, _ = _delta_scan(ql, kl, vl, gl, s_ctx, chunk, SCAN_CHUNKS_PER_STEP, True)
    wr = _router_weights(router_g[0], router_e[0])
    x1, hf, rt = _delta_out(o_f, o_b, zl, x, gt_m, onorm_a[0].reshape(1, HEAD_DIM), w_out_a[0].astype(BF16),
                            norm_ffn[0].reshape(1, d), sh_f, sc_f, wr, tm)
    fn = final_norm.reshape(1, d)
    x2 = _moe(hf, rt, x1, gt_f, fn, w1[0], w3[0], w2[0], s, False)
    x2 = x2.reshape(b, s, d)

    sh_m, sc_m, gt_m, sh_f, sc_f, gt_f = mods(1, slice(0, b))
    wr = _router_weights(router_g[1], router_e[1])
    x1, hf, rt = _sconv(x2, sh_m, sc_m, norm_mix[1].reshape(1, d), w_in_b[0].astype(BF16), conv_b[0],
                        w_out_b[0].astype(BF16), gt_m, norm_ffn[1].reshape(1, d), sh_f, sc_f, wr, tm)
    out = _moe(hf, rt, x1, gt_f, fn, w1[1], w3[1], w2[1], s, True)
    return out.reshape(b, s, d)
```

```python
import functools

import jax
import jax.numpy as jnp
from jax import lax
from jax.experimental import pallas as pl
from jax.experimental.pallas import tpu as pltpu

F32 = jnp.float32
BF16 = jnp.bfloat16
I32 = jnp.int32
HIGHEST = lax.Precision.HIGHEST

EPS = 1e-6
N_HEADS = 8
HEAD_DIM = 128
GRID_W = 64
N_GROUPS = 4
EXPERTS_PER_GROUP = 8
N_EXPERTS = N_GROUPS * EXPERTS_PER_GROUP
PAIRS_PER_GROUP = EXPERTS_PER_GROUP * (EXPERTS_PER_GROUP - 1) // 2
N_CLASSES = N_GROUPS * PAIRS_PER_GROUP
_PAIRS = [(a, b) for a in range(EXPERTS_PER_GROUP) for b in range(a + 1, EXPERTS_PER_GROUP)]
CLASS_LOCAL_A = [a for _ in range(N_GROUPS) for a, _ in _PAIRS]
CLASS_LOCAL_B = [b for _ in range(N_GROUPS) for _, b in _PAIRS]
MOE_BLOCK = 256
N_MOD = 6
LANES = 128
SCAN_CHUNK = 64
SCAN_CHUNKS_PER_STEP = 2
SEQ_HALO = 8
VMEM_LIMIT = 48 * 1024 * 1024


def _params(*sem):
    return pltpu.CompilerParams(dimension_semantics=sem, vmem_limit_bytes=VMEM_LIMIT)


def _dot(a, b, precision=None):
    return jnp.dot(a, b, preferred_element_type=F32, precision=precision)


def _dot_nt(a, b, precision=None):
    return lax.dot_general(a, b, (((1,), (1,)), ((), ())), preferred_element_type=F32, precision=precision)


def _dot_tn(a, b, precision=None):
    return lax.dot_general(a, b, (((0,), (0,)), ((), ())), preferred_element_type=F32, precision=precision)


def _silu(x):
    return x * jax.nn.sigmoid(x)


def _rms_mod(x, g, shift, scale):
    ms = jnp.mean(x * x, axis=-1, keepdims=True)
    return x * lax.rsqrt(ms + EPS) * g * (1.0 + scale) + shift


def _ada_kernel(c_ref, w_ref, b_ref, o_ref):
    o_ref[0] = _dot(_silu(c_ref[...]), w_ref[0], HIGHEST) + b_ref[0]


def _ada_params(cond, ada_w, ada_b):
    depth, d, n = ada_w.shape
    tn = 512
    r = cond.shape[0]
    return pl.pallas_call(
        _ada_kernel,
        grid=(depth, n // tn),
        in_specs=[pl.BlockSpec((r, d), lambda l, j: (0, 0)),
                  pl.BlockSpec((1, d, tn), lambda l, j: (l, 0, j)),
                  pl.BlockSpec((1, 1, tn), lambda l, j: (l, 0, j))],
        out_specs=pl.BlockSpec((1, r, tn), lambda l, j: (l, 0, j)),
        out_shape=jax.ShapeDtypeStruct((depth, r, n), F32),
        compiler_params=_params("parallel", "parallel"),
    )(cond, ada_w, ada_b.reshape(depth, 1, n))


def _delta_in_kernel(xm_ref, xp_ref, xn_ref, sh_ref, sc_ref, g_ref, wqkv_ref, cw_ref, wz_ref, wg_ref, gp_ref, tri_ref,
                     q_ref, k_ref, v_ref, z_ref, gate_ref, *, tm, ck):
    i = pl.program_id(1)
    n = pl.num_programs(1)
    g, sh, sc = g_ref[...], sh_ref[0], sc_ref[0]
    hm = _rms_mod(xm_ref[0], g, sh, sc)
    hp = _rms_mod(xp_ref[0], g, sh, sc) * (i > 0).astype(F32)
    hn = _rms_mod(xn_ref[0], g, sh, sc) * (i < n - 1).astype(F32)
    hm16 = hm.astype(BF16)
    hext = jnp.concatenate([hp, hm, hn], axis=0).astype(BF16)
    rows = tm + 2 * SEQ_HALO
    d = hm.shape[1]
    outs = (q_ref, k_ref, v_ref)
    for j in range(3 * d // ck):
        p = _dot(hext, wqkv_ref[:, j * ck:(j + 1) * ck])
        cw = cw_ref[:, j * ck:(j + 1) * ck]
        y = cw[0:1] * pltpu.roll(p, 1, 0) + cw[1:2] * p + cw[2:3] * pltpu.roll(p, rows - 1, 0)
        y = _silu(y[SEQ_HALO:SEQ_HALO + tm])
        which, col0 = (j * ck) // d, (j * ck) % d
        for hh in range(ck // HEAD_DIM):
            yh = y[:, hh * HEAD_DIM:(hh + 1) * HEAD_DIM]
            if which < 2:
                yh = yh * lax.rsqrt(jnp.sum(yh * yh, axis=-1, keepdims=True) + EPS)
            if which == 0:
                yh = yh * HEAD_DIM ** -0.5
            c0 = col0 + hh * HEAD_DIM
            outs[which][0, :, c0:c0 + HEAD_DIM] = yh.astype(q_ref.dtype)
    z_ref[0] = _dot(hm16, wz_ref[...]).astype(z_ref.dtype)
    pg = _dot(hm16, wg_ref[...])
    lane = lax.broadcasted_iota(I32, pg.shape, 1) % LANES
    xg = pg + gp_ref[1:2]
    softplus = jnp.maximum(xg, 0.0) + jnp.log1p(jnp.exp(-jnp.abs(xg)))
    gval = -jnp.exp(gp_ref[0:1]) * softplus
    g_hi = gval.astype(BF16)
    r1 = gval - g_hi.astype(F32)
    g_mid = r1.astype(BF16)
    g_lo = (r1 - g_mid.astype(F32)).astype(BF16)
    cums = []
    for direction in range(2):
        sl = slice(direction * LANES, (direction + 1) * LANES)
        terms = jnp.concatenate([g_hi[:, sl], g_mid[:, sl], g_lo[:, sl]], axis=1)
        cs = _dot(tri_ref[direction], terms)
        cums.append(cs[:, :LANES] + cs[:, LANES:2 * LANES] + cs[:, 2 * LANES:])
    gcum = jnp.concatenate(cums, axis=1)
    gate_ref[0] = jnp.where(lane < N_HEADS, jax.nn.sigmoid(pg), jnp.where(
        lane < 2 * N_HEADS, gval, jnp.where(lane < 3 * N_HEADS, gcum, 0.0)))


def _delta_in(x, sh, sc, gnorm, wqkv, conv_w, wz, wg, gp, tm):
    b, s, d = x.shape
    tm = min(tm, s)
    hb = tm // SEQ_HALO
    last = s // SEQ_HALO - 1
    row = lambda bb, i: (bb, i, 0)
    vec = lambda bb, i: (bb, 0, 0)
    full = lambda bb, i: (0, 0)
    outs = [jax.ShapeDtypeStruct((b, s, d), BF16)] * 4 + [jax.ShapeDtypeStruct((b, s, 2 * LANES), F32)]
    ri = lax.broadcasted_iota(I32, (tm, tm), 0)
    ci = lax.broadcasted_iota(I32, (tm, tm), 1)
    same = (ri // SCAN_CHUNK) == (ci // SCAN_CHUNK)
    tri = jnp.stack([same & (ri >= ci), same & (ri <= ci)]).astype(BF16)
    return pl.pallas_call(
        functools.partial(_delta_in_kernel, tm=tm, ck=512),
        grid=(b, s // tm),
        in_specs=[pl.BlockSpec((1, tm, d), row),
                  pl.BlockSpec((1, SEQ_HALO, d), lambda bb, i: (bb, jnp.maximum(i * hb - 1, 0), 0)),
                  pl.BlockSpec((1, SEQ_HALO, d), lambda bb, i: (bb, jnp.minimum((i + 1) * hb, last), 0)),
                  pl.BlockSpec((1, 1, d), vec), pl.BlockSpec((1, 1, d), vec),
                  pl.BlockSpec((1, d), full),
                  pl.BlockSpec((d, 3 * d), full), pl.BlockSpec((3, 3 * d), full),
                  pl.BlockSpec((d, d), full), pl.BlockSpec((d, 2 * LANES), full),
                  pl.BlockSpec((2, 2 * LANES), full), pl.BlockSpec((2, tm, tm), lambda bb, i: (0, 0, 0))],
        out_specs=[pl.BlockSpec((1, tm, d), row)] * 4 + [pl.BlockSpec((1, tm, 2 * LANES), row)],
        out_shape=outs,
        compiler_params=_params("parallel", "parallel"),
    )(x, x, x, sh, sc, gnorm, wqkv, conv_w, wz, wg, gp, tri)


def _scan_kernel(qf_ref, kf_ref, vf_ref, gf_ref, qb_ref, kb_ref, vb_ref, gb_ref, s0_ref, *out_refs, c, cps,
                 with_output):
    if with_output:
        of_ref, ob_ref, s_ref = out_refs
    else:
        (s_ref,) = out_refs
        of_ref = ob_ref = None
    step = pl.program_id(1)

    @pl.when(step == 0)
    def _():
        s_ref[...] = s0_ref[...]

    in_refs = ((qf_ref, kf_ref, vf_ref, gf_ref, of_ref), (qb_ref, kb_ref, vb_ref, gb_ref, ob_ref))
    heads = range(N_HEADS)
    ri = lax.broadcasted_iota(I32, (c, c), 0)
    ci = lax.broadcasted_iota(I32, (c, c), 1)
    eye = (ri == ci).astype(F32)
    incl = (ri >= ci, ri <= ci)
    strict = (ri > ci, ri < ci)

    def rows(dd, j):
        r0 = (j if dd == 0 else cps - 1 - j) * c
        return slice(r0, r0 + c)

    def lanes(h):
        return slice(h * HEAD_DIM, (h + 1) * HEAD_DIM)

    chunks = [(dd, j) for dd in range(2) for j in range(cps)]
    chains = [(dd, j, h) for dd, j in chunks for h in heads]
    gates = {dj: in_refs[dj[0]][3][0, rows(*dj), :] for dj in chunks}
    gates_t = {dj: jnp.concatenate([gates[dj], jnp.zeros((LANES - c, LANES), F32)], axis=0).T[:, :c]
               for dj in chunks}
    tot_row = {dj: gates[dj][c - 1:c, :] if dj[0] == 0 else gates[dj][0:1, :] for dj in chunks}

    def load(which, key):
        dd, j, h = key
        return in_refs[dd][which][0, rows(dd, j), lanes(h)]

    q16 = {key: load(0, key) for key in chains}
    k16 = {key: load(1, key) for key in chains}
    qf = {key: q16[key].astype(F32) for key in chains}
    kf = {key: k16[key].astype(F32) for key in chains}
    vf = {key: load(2, key).astype(F32) for key in chains}
    cum = 2 * N_HEADS
    beta = {(dd, j, h): gates[(dd, j)][:, h:h + 1] for dd, j, h in chains}
    gcc = {(dd, j, h): gates[(dd, j)][:, cum + h:cum + h + 1] for dd, j, h in chains}
    gcr = {(dd, j, h): gates_t[(dd, j)][cum + h:cum + h + 1, :] for dd, j, h in chains}
    tot = {(dd, j, h): tot_row[(dd, j)][:, cum + h:cum + h + 1] for dd, j, h in chains}
    decay = {key: jnp.exp(jnp.where(incl[key[0]], gcc[key] - gcr[key], -jnp.inf)) for key in chains}
    e = {key: jnp.exp(gcc[key]) for key in chains}
    kb = {key: kf[key] * beta[key] for key in chains}
    a = {key: _dot_nt(jnp.concatenate([kb[key].astype(BF16), q16[key]], axis=0), k16[key]) for key in chains}
    lmat = {key: jnp.where(strict[key[0]], a[key][:c] * decay[key], 0.0) for key in chains}
    qk = {key: (a[key][c:] * decay[key]).astype(BF16) for key in chains}
    p = {key: eye - lmat[key] for key in chains}
    l16 = {key: lmat[key].astype(BF16) for key in chains}
    m = {key: _dot(l16[key], l16[key]) for key in chains}
    span = 2
    while span < c:
        span *= 2
        m16 = {key: m[key].astype(BF16) for key in chains}
        if span < c:
            pm = {key: _dot(jnp.concatenate([p[key], m[key]], axis=0).astype(BF16), m16[key]) for key in chains}
            p = {key: p[key] + pm[key][:c] for key in chains}
            m = {key: pm[key][c:] for key in chains}
        else:
            p = {key: p[key] + _dot(p[key].astype(BF16), m16[key]) for key in chains}
    rhs = {key: jnp.concatenate([vf[key] * beta[key], kb[key] * e[key]], axis=1).astype(BF16) for key in chains}
    sol = {key: _dot(p[key].astype(BF16), rhs[key]) for key in chains}
    wq = {key: jnp.concatenate([sol[key][:, HEAD_DIM:], qf[key] * e[key]], axis=0).astype(BF16) for key in chains}
    k_dec = {key: (kf[key] * jnp.exp(tot[key] - gcc[key])).astype(BF16) for key in chains}
    s_tot = {key: jnp.exp(tot[key]) for key in chains}

    state = {(dd, h): s_ref[0, dd, h] for dd in range(2) for h in heads}
    for j in range(cps):
        now = [(dd, j, h) for dd in range(2) for h in heads]
        s16 = {key: state[(key[0], key[2])].astype(BF16) for key in now}
        ws = {key: _dot(wq[key], s16[key]) for key in now}
        v16 = {key: (sol[key][:, :HEAD_DIM] - ws[key][:c]).astype(BF16) for key in now}
        for key in now:
            sk = (key[0], key[2])
            state[sk] = state[sk] * s_tot[key] + _dot_tn(k_dec[key], v16[key])
        if with_output:
            o = {key: ws[key][c:] + _dot(qk[key], v16[key]) for key in now}
            for dd, _, h in now:
                in_refs[dd][4][0, rows(dd, j), lanes(h)] = o[(dd, j, h)].astype(of_ref.dtype)
    for dd in range(2):
        for h in heads:
            s_ref[0, dd, h] = state[(dd, h)]


def _delta_scan(q, k, v, gates, s0, c, cps, with_output):
    b, s, d = q.shape
    rb = c * cps
    nb = s // rb
    fwd = lambda bb, i: (bb, i, 0)
    bwd = lambda bb, i: (bb, nb - 1 - i, 0)
    state_spec = pl.BlockSpec((1, 2, N_HEADS, HEAD_DIM, HEAD_DIM), lambda bb, i: (bb, 0, 0, 0, 0))
    state_shape = jax.ShapeDtypeStruct((b, 2, N_HEADS, HEAD_DIM, HEAD_DIM), F32)
    out_specs, out_shape = [state_spec], [state_shape]
    if with_output:
        out_specs = [pl.BlockSpec((1, rb, d), fwd), pl.BlockSpec((1, rb, d), bwd)] + out_specs
        out_shape = [jax.ShapeDtypeStruct((b, s, d), BF16)] * 2 + out_shape
    return pl.pallas_call(
        functools.partial(_scan_kernel, c=c, cps=cps, with_output=with_output),
        grid=(b, nb),
        in_specs=([pl.BlockSpec((1, rb, d), fwd)] * 3 + [pl.BlockSpec((1, rb, LANES), fwd)]
                  + [pl.BlockSpec((1, rb, d), bwd)] * 3
                  + [pl.BlockSpec((1, rb, LANES), lambda bb, i: (bb, nb - 1 - i, 1)), state_spec]),
        out_specs=out_specs,
        out_shape=out_shape,
        compiler_params=_params("parallel", "arbitrary"),
    )(q, k, v, gates, q, k, v, gates, s0)


def _ffn_pre(x1, gf, shf, scf, wr_ref, x1_ref, hf_ref, rt_ref):
    x1_ref[...] = x1
    hf = _rms_mod(x1, gf, shf, scf)
    d = hf.shape[1]
    hf_ref[:, :d] = hf
    hi = hf.astype(BF16)
    lo = (hf - hi.astype(F32)).astype(BF16)
    a = _dot(hi, wr_ref[...])
    logits = a[:, :LANES] + a[:, LANES:] + _dot(lo, wr_ref[:, :LANES])
    lane = lax.broadcasted_iota(I32, logits.shape, 1)
    lanef = lane.astype(F32)
    big = float(LANES)
    gl = jnp.where(lane < N_GROUPS, logits, -jnp.inf)
    gmax = jnp.max(gl, axis=-1, keepdims=True)
    gi = jnp.min(jnp.where(gl == gmax, lanef, big), axis=-1, keepdims=True)
    pg_sel = 1.0 / jnp.sum(jnp.exp(gl - gmax), axis=-1, keepdims=True)
    rel = lanef - (N_GROUPS + EXPERTS_PER_GROUP * gi)
    el = jnp.where((rel >= 0.0) & (rel < float(EXPERTS_PER_GROUP)), logits, -jnp.inf)
    m1 = jnp.max(el, axis=-1, keepdims=True)
    i1 = jnp.min(jnp.where(el == m1, lanef, big), axis=-1, keepdims=True)
    el2 = jnp.where(lanef == i1, -jnp.inf, el)
    m2 = jnp.max(el2, axis=-1, keepdims=True)
    i2 = jnp.min(jnp.where(el2 == m2, lanef, big), axis=-1, keepdims=True)
    r2 = jnp.exp(m2 - m1)
    w1 = pg_sel / (1.0 + r2)
    w2 = pg_sel * r2 / (1.0 + r2)
    first_low = i1 < i2
    a_loc = jnp.minimum(i1, i2) - (N_GROUPS + EXPERTS_PER_GROUP * gi)
    b_loc = jnp.maximum(i1, i2) - (N_GROUPS + EXPERTS_PER_GROUP * gi)
    pair = a_loc * (EXPERTS_PER_GROUP - 1) - a_loc * (a_loc - 1.0) * 0.5 + (b_loc - a_loc - 1.0)
    cls = gi * float(PAIRS_PER_GROUP) + pair
    rt = jnp.where(lane == 0, jnp.where(first_low, w1, w2), jnp.where(
        lane == 1, jnp.where(first_low, w2, w1), jnp.where(lane == 2, cls, 0.0)))
    rt_ref[...] = rt
    hf_ref[:, d:] = rt


def _delta_out_kernel(of_ref, ob_ref, z_ref, x_ref, gt_ref, on_ref, wo_ref, gf_ref, shf_ref, scf_ref, wr_ref,
                      x1_ref, hf_ref, rt_ref):
    o = of_ref[0].astype(F32) + ob_ref[0].astype(F32)
    z = z_ref[0].astype(F32)
    onorm = on_ref[...]
    parts = []
    for h in range(N_HEADS):
        oh = o[:, h * HEAD_DIM:(h + 1) * HEAD_DIM]
        parts.append(oh * lax.rsqrt(jnp.mean(oh * oh, axis=-1, keepdims=True) + EPS) * onorm)
    y = jnp.concatenate(parts, axis=1) * _silu(z)
    x1 = x_ref[0] + gt_ref[0] * _dot(y.astype(BF16), wo_ref[...])
    _ffn_pre(x1, gf_ref[...], shf_ref[0], scf_ref[0], wr_ref, x1_ref, hf_ref, rt_ref)


def _delta_out(o_f, o_b, z, x, gt, onorm, wo, gf, shf, scf, wr, tm):
    b, s, d = x.shape
    t = b * s
    nt = s // tm
    row = lambda bb, i: (bb, i, 0)
    vec = lambda bb, i: (bb, 0, 0)
    full = lambda bb, i: (0, 0)
    flat = lambda bb, i: (bb * nt + i, 0)
    return pl.pallas_call(
        _delta_out_kernel,
        grid=(b, nt),
        in_specs=[pl.BlockSpec((1, tm, d), row), pl.BlockSpec((1, tm, d), row),
                  pl.BlockSpec((1, tm, d), row), pl.BlockSpec((1, tm, d), row),
                  pl.BlockSpec((1, 1, d), vec), pl.BlockSpec((1, HEAD_DIM), full),
                  pl.BlockSpec((d, d), full), pl.BlockSpec((1, d), full),
                  pl.BlockSpec((1, 1, d), vec), pl.BlockSpec((1, 1, d), vec),
                  pl.BlockSpec((d, 2 * LANES), full)],
        out_specs=[pl.BlockSpec((tm, d), flat), pl.BlockSpec((tm, d + LANES), flat), pl.BlockSpec((tm, LANES), flat)],
        out_shape=[jax.ShapeDtypeStruct((t, d), F32), jax.ShapeDtypeStruct((t, d + LANES), F32),
                   jax.ShapeDtypeStruct((t, LANES), F32)],
        compiler_params=_params("parallel", "parallel"),
    )(o_f, o_b, z, x, gt, onorm, wo, gf, shf, scf, wr)


def _sconv_kernel(xm_ref, xp_ref, xn_ref, sh_ref, sc_ref, g_ref, win_ref, cw_ref, wo_ref, gt_ref,
                  gf_ref, shf_ref, scf_ref, wr_ref, x1_ref, hf_ref, rt_ref, *, tm, ck):
    i = pl.program_id(1)
    n = pl.num_programs(1)
    g, sh, sc = g_ref[...], sh_ref[0], sc_ref[0]
    xm = xm_ref[0]
    hm = _rms_mod(xm, g, sh, sc)
    hp = _rms_mod(xp_ref[0], g, sh, sc) * (i > 0).astype(F32)
    hn = _rms_mod(xn_ref[0], g, sh, sc) * (i < n - 1).astype(F32)
    hm16 = hm.astype(BF16)
    hext = jnp.concatenate([hp, hm, hn], axis=0).astype(BF16)
    d = xm.shape[1]
    col = lax.broadcasted_iota(I32, (tm, 1), 0) % GRID_W
    acc = jnp.zeros((tm, d), F32)
    for j in range(d // ck):
        c0 = j * ck
        gate_b = _dot(hm16, win_ref[:, c0:c0 + ck])
        cw = cw_ref[:, c0:c0 + ck]
        if c0 < d // 2:
            u = _dot(hm16, win_ref[:, d + c0:d + c0 + ck]) * _dot(hm16, win_ref[:, 2 * d + c0:2 * d + c0 + ck])
            left = jnp.where(col == 0, 0.0, pltpu.roll(u, 1, 0))
            right = jnp.where(col == GRID_W - 1, 0.0, pltpu.roll(u, tm - 1, 0))
            y = cw[0:1] * left + cw[1:2] * u + cw[2:3] * right
        else:
            u = _dot(hext, win_ref[:, d + c0:d + c0 + ck]) * _dot(hext, win_ref[:, 2 * d + c0:2 * d + c0 + ck])
            y = cw[0:1] * u[0:tm] + cw[1:2] * u[GRID_W:GRID_W + tm] + cw[2:3] * u[2 * GRID_W:2 * GRID_W + tm]
        acc = acc + _dot((gate_b * y).astype(BF16), wo_ref[c0:c0 + ck, :])
    x1 = xm + gt_ref[0] * acc
    _ffn_pre(x1, gf_ref[...], shf_ref[0], scf_ref[0], wr_ref, x1_ref, hf_ref, rt_ref)


def _sconv(x, sh, sc, gnorm, win, conv_w, wo, gt, gf, shf, scf, wr, tm):
    b, s, d = x.shape
    t = b * s
    nt = s // tm
    hb = tm // GRID_W
    last = s // GRID_W - 1
    row = lambda bb, i: (bb, i, 0)
    vec = lambda bb, i: (bb, 0, 0)
    full = lambda bb, i: (0, 0)
    flat = lambda bb, i: (bb * nt + i, 0)
    return pl.pallas_call(
        functools.partial(_sconv_kernel, tm=tm, ck=256),
        grid=(b, nt),
        in_specs=[pl.BlockSpec((1, tm, d), row),
                  pl.BlockSpec((1, GRID_W, d), lambda bb, i: (bb, jnp.maximum(i * hb - 1, 0), 0)),
                  pl.BlockSpec((1, GRID_W, d), lambda bb, i: (bb, jnp.minimum((i + 1) * hb, last), 0)),
                  pl.BlockSpec((1, 1, d), vec), pl.BlockSpec((1, 1, d), vec), pl.BlockSpec((1, d), full),
                  pl.BlockSpec((d, 3 * d), full), pl.BlockSpec((3, d), full), pl.BlockSpec((d, d), full),
                  pl.BlockSpec((1, 1, d), vec), pl.BlockSpec((1, d), full),
                  pl.BlockSpec((1, 1, d), vec), pl.BlockSpec((1, 1, d), vec),
                  pl.BlockSpec((d, 2 * LANES), full)],
        out_specs=[pl.BlockSpec((tm, d), flat), pl.BlockSpec((tm, d + LANES), flat), pl.BlockSpec((tm, LANES), flat)],
        out_shape=[jax.ShapeDtypeStruct((t, d), F32), jax.ShapeDtypeStruct((t, d + LANES), F32),
                   jax.ShapeDtypeStruct((t, LANES), F32)],
        compiler_params=_params("parallel", "parallel"),
    )(x, x, x, sh, sc, gnorm, win, conv_w, wo, gt, gf, shf, scf, wr)


def _rank_kernel(c_ref, dest_ref, ends_ref, cnt_ref, start_ref, carry_ref, *, tm, bm):
    p = pl.program_id(0)
    i = pl.program_id(1)
    onehot = lax.broadcasted_iota(I32, (LANES, tm), 0) == c_ref[0:1, :]
    ohf = onehot.astype(F32)
    per_class = jnp.broadcast_to(jnp.sum(ohf, axis=1, keepdims=True), (LANES, LANES))

    @pl.when(jnp.logical_and(p == 0, i == 0))
    def _():
        cnt_ref[...] = jnp.zeros_like(cnt_ref)

    @pl.when(p == 0)
    def _():
        cnt_ref[...] += per_class

    @pl.when(jnp.logical_and(p == 1, i == 0))
    def _():
        padded = ((cnt_ref[...].astype(I32) + (bm - 1)) & (-bm)).astype(F32)
        before = (lax.broadcasted_iota(I32, (LANES, LANES), 0) > lax.broadcasted_iota(I32, (LANES, LANES), 1))
        start = _dot(before.astype(F32), padded, HIGHEST)
        start_ref[...] = start
        ends_ref[...] = (start + padded).astype(I32)
        carry_ref[...] = jnp.zeros_like(carry_ref)

    @pl.when(p == 1)
    def _():
        earlier = (lax.broadcasted_iota(I32, (tm, tm), 0) < lax.broadcasted_iota(I32, (tm, tm), 1)).astype(BF16)
        slot = _dot(onehot.astype(BF16), earlier) + carry_ref[:, 0:1] + start_ref[:, 0:1]
        dest = jnp.sum(ohf * slot, axis=0, keepdims=True)
        dest_ref[...] = jnp.concatenate([dest, jnp.zeros((7, tm), F32)], axis=0).astype(I32)
        carry_ref[...] += per_class


def _rank(cls8, tm, bm):
    t = cls8.shape[1]
    return pl.pallas_call(
        functools.partial(_rank_kernel, tm=tm, bm=bm),
        grid=(2, t // tm),
        in_specs=[pl.BlockSpec((8, tm), lambda p, i: (0, i))],
        out_specs=[pl.BlockSpec((8, tm), lambda p, i: (0, i * p)), pl.BlockSpec((LANES, LANES), lambda p, i: (0, 0))],
        out_shape=[jax.ShapeDtypeStruct((8, t), I32), jax.ShapeDtypeStruct((LANES, LANES), I32)],
        scratch_shapes=[pltpu.VMEM((LANES, LANES), F32)] * 3,
        compiler_params=_params("arbitrary", "arbitrary"),
    )(cls8)


def _row_copy(src_ref, src_row, dst_ref, dst_row, sem):
    return pltpu.make_async_copy(src_ref.at[pl.ds(src_row, 1), :], dst_ref.at[pl.ds(dst_row, 1), :], sem)


def _dispatch_kernel(dest_ref, hf_ref, xb_in_ref, xb_ref, sem, *, td):
    del xb_in_ref

    def issue(r, carry):
        _row_copy(hf_ref, r, xb_ref, dest_ref[0, 0, r], sem).start()
        return carry

    for r in range(td):
        issue(r, 0)

    def drain(r, carry):
        _row_copy(hf_ref, 0, xb_ref, 0, sem).wait()
        return carry

    lax.fori_loop(0, td, drain, 0, unroll=8)


def _dispatch(dest3, hf, xb0, td):
    t, d = hf.shape
    cap = xb0.shape[0]
    return pl.pallas_call(
        functools.partial(_dispatch_kernel, td=td),
        grid=(t // td,),
        in_specs=[pl.BlockSpec((1, 1, td), lambda i: (i, 0, 0), memory_space=pltpu.SMEM),
                  pl.BlockSpec((td, d), lambda i: (i, 0)),
                  pl.BlockSpec(memory_space=pl.ANY)],
        out_specs=pl.BlockSpec(memory_space=pl.ANY),
        out_shape=jax.ShapeDtypeStruct((cap, d), hf.dtype),
        scratch_shapes=[pltpu.SemaphoreType.DMA],
        input_output_aliases={2: 0},
        compiler_params=_params("arbitrary"),
    )(dest3, hf, xb0)


def _expert_kernel(grp_ref, la_ref, lb_ref, bs_ref, nu_ref, x_ref, w1_hbm, w3_hbm, w2_hbm, y_ref,
                   res1, res3, res2, stage_up, stage_down, sems, *, layer):
    del bs_ref
    i = pl.program_id(0)
    d = y_ref.shape[1]
    grp = grp_ref[i]

    @pl.when(jnp.logical_or(i == 0, grp != grp_ref[jnp.maximum(i - 1, 0)]))
    def _():
        plan = []
        for e in range(EXPERTS_PER_GROUP):
            plan += [(w1_hbm, res1, stage_up.at[0], 0, e), (w3_hbm, res3, stage_up.at[1], 1, e),
                     (w2_hbm, res2, stage_down, 2, e)]

        def copy(step):
            src, _, stage, sem, e = plan[step]
            return pltpu.make_async_copy(src.at[layer, grp * EXPERTS_PER_GROUP + e], stage, sems.at[sem])

        copy(0).start()
        for step, (_, dst, stage, _, e) in enumerate(plan):
            if step + 1 < len(plan):
                copy(step + 1).start()
            copy(step).wait()
            dst[e] = stage[...].astype(BF16)

    @pl.when(i < nu_ref[0])
    def _():
        x = x_ref[:, :d].astype(BF16)

        def swiglu(e):
            h = _silu(_dot(x, res1[e])) * _dot(x, res3[e])
            return _dot(h.astype(BF16), res2[e])

        y_ref[...] = x_ref[:, d:d + 1] * swiglu(la_ref[i]) + x_ref[:, d + 1:d + 2] * swiglu(lb_ref[i])

    @pl.when(i >= nu_ref[0])
    def _():
        y_ref[...] = jnp.zeros_like(y_ref)


def _experts(block_grp, block_la, block_lb, block_src, n_used, xb, w1, w3, w2, layer, bm):
    cap, dx = xb.shape
    _, _, d, de = w1.shape
    grid_spec = pltpu.PrefetchScalarGridSpec(
        num_scalar_prefetch=5,
        grid=(cap // bm,),
        in_specs=[pl.BlockSpec((bm, dx), lambda i, g, la, lb, bs, nu: (bs[i], 0)),
                  pl.BlockSpec(memory_space=pl.ANY), pl.BlockSpec(memory_space=pl.ANY),
                  pl.BlockSpec(memory_space=pl.ANY)],
        out_specs=pl.BlockSpec((bm, d), lambda i, g, la, lb, bs, nu: (i, 0)),
        scratch_shapes=[pltpu.VMEM((EXPERTS_PER_GROUP, d, de), BF16), pltpu.VMEM((EXPERTS_PER_GROUP, d, de), BF16),
                        pltpu.VMEM((EXPERTS_PER_GROUP, de, d), BF16), pltpu.VMEM((2, d, de), F32),
                        pltpu.VMEM((de, d), F32), pltpu.SemaphoreType.DMA((3,))],
    )
    return pl.pallas_call(
        functools.partial(_expert_kernel, layer=layer),
        grid_spec=grid_spec,
        out_shape=jax.ShapeDtypeStruct((cap, d), F32),
        compiler_params=_params("arbitrary"),
    )(block_grp, block_la, block_lb, block_src, n_used, xb, w1, w3, w2)


def _combine_kernel(dest_ref, x1_ref, gt_ref, fn_ref, yb_ref, o_ref, buf, sem, *, tc, final):
    def issue(r, carry):
        _row_copy(yb_ref, dest_ref[0, 0, r], buf, r, sem).start()
        return carry

    for r in range(tc):
        issue(r, 0)

    def drain(r, carry):
        _row_copy(yb_ref, 0, buf, 0, sem).wait()
        return carry

    lax.fori_loop(0, tc, drain, 0, unroll=8)
    x2 = x1_ref[...] + gt_ref[0] * buf[...]
    if final:
        ms = jnp.mean(x2 * x2, axis=-1, keepdims=True)
        x2 = x2 * lax.rsqrt(ms + EPS) * fn_ref[...]
    o_ref[...] = x2


def _combine(dest3, x1, gt, fnorm, yb, s, tc, final):
    t, d = x1.shape
    per_b = s // tc
    return pl.pallas_call(
        functools.partial(_combine_kernel, tc=tc, final=final),
        grid=(t // tc,),
        in_specs=[pl.BlockSpec((1, 1, tc), lambda i: (i, 0, 0), memory_space=pltpu.SMEM),
                  pl.BlockSpec((tc, d), lambda i: (i, 0)),
                  pl.BlockSpec((1, 1, d), lambda i: (i // per_b, 0, 0)),
                  pl.BlockSpec((1, d), lambda i: (0, 0)),
                  pl.BlockSpec(memory_space=pl.ANY)],
        out_specs=pl.BlockSpec((tc, d), lambda i: (i, 0)),
        out_shape=jax.ShapeDtypeStruct((t, d), F32),
        scratch_shapes=[pltpu.VMEM((tc, d), F32), pltpu.SemaphoreType.DMA],
        compiler_params=_params("arbitrary"),
    )(dest3, x1, gt, fnorm, yb)


def _moe(hfx, rt, x1, gt, fnorm, w1, w3, w2, layer, xb0, s, final, bm, tr=512, td=1024):
    t = hfx.shape[0]
    cls8 = jnp.zeros((8, t), I32).at[0].set(rt[:, 2].astype(I32))
    dest8, ends2 = _rank(cls8, tr, bm)
    dest3 = dest8[0].reshape(t // td, 1, td)
    ends = ends2[:N_CLASSES, 0]
    n_blocks = xb0.shape[0] // bm
    n_used = ends[-1] // bm
    block_src = jnp.maximum(jnp.minimum(jnp.arange(n_blocks, dtype=I32), n_used - 1), 0)
    block_cls = jnp.minimum(jnp.sum((ends[None, :] <= (block_src * bm)[:, None]).astype(I32), axis=1), N_CLASSES - 1)
    block_la = jnp.asarray(CLASS_LOCAL_A, I32)[block_cls]
    block_lb = jnp.asarray(CLASS_LOCAL_B, I32)[block_cls]
    xb = _dispatch(dest3, hfx, xb0, td)
    yb = _experts(block_cls // PAIRS_PER_GROUP, block_la, block_lb, block_src, n_used.reshape(1), xb, w1, w3, w2,
                  layer, bm)
    return _combine(dest3, x1, gt, fnorm, yb, s, td, final), xb


def _router_weights(router_g, router_e):
    d = router_g.shape[0]
    w = jnp.zeros((d, LANES), F32).at[:, :N_GROUPS].set(router_g).at[:, N_GROUPS:N_GROUPS + N_EXPERTS].set(router_e)
    hi = w.astype(BF16)
    lo = (w - hi.astype(F32)).astype(BF16)
    return jnp.concatenate([hi, lo], axis=1)


def _gate_weights(w_in, a_log, dt_bias):
    d = w_in.shape[0]
    base = 4 * d
    wg = jnp.zeros((d, 2 * LANES), F32)
    gp = jnp.zeros((2, 2 * LANES), F32)
    for direction in range(2):
        o = direction * LANES
        wg = wg.at[:, o:o + N_HEADS].set(w_in[:, base + direction * N_HEADS:base + (direction + 1) * N_HEADS])
        wg = wg.at[:, o + N_HEADS:o + 2 * N_HEADS].set(
            w_in[:, base + (2 + direction) * N_HEADS:base + (3 + direction) * N_HEADS])
        wg = wg.at[:, o + 2 * N_HEADS:o + 3 * N_HEADS].set(
            w_in[:, base + (2 + direction) * N_HEADS:base + (3 + direction) * N_HEADS])
        for rep in (1, 2):
            gp = gp.at[0, o + rep * N_HEADS:o + (rep + 1) * N_HEADS].set(a_log[direction])
            gp = gp.at[1, o + rep * N_HEADS:o + (rep + 1) * N_HEADS].set(dt_bias[direction])
    return wg.astype(BF16), gp


def kernel(x, c, ctx, c_ctx, ada_w, ada_b, norm_mix, norm_ffn, w_in_a, conv_a, a_log_a, dt_bias_a, onorm_a, w_out_a,
           w_in_b, conv_b, w_out_b, router_g, router_e, w1, w3, w2, final_norm):
    b, s, d = x.shape
    chunk = SCAN_CHUNK
    tm = min(512, s)

    cond = jnp.zeros((16, d), F32).at[:b].set(c).at[b].set(c_ctx)
    mod = _ada_params(cond, ada_w, ada_b)

    def mods(layer, rows):
        m = mod[layer, rows]
        return [jnp.broadcast_to(m[:, None, k * d:(k + 1) * d], (b, 1, d)) for k in range(N_MOD)]

    sh_m, sc_m, gt_m, sh_f, sc_f, gt_f = mods(0, slice(0, b))
    csh_m, csc_m = mods(0, slice(b, b + 1))[:2]
    w_in = w_in_a[0]
    wqkv, wz = w_in[:, :3 * d].astype(BF16), w_in[:, 3 * d:4 * d].astype(BF16)
    wg, gp = _gate_weights(w_in, a_log_a[0], dt_bias_a[0])
    gn = norm_mix[0].reshape(1, d)
    qc, kc, vc, _, gc = _delta_in(ctx, csh_m, csc_m, gn, wqkv, conv_a[0], wz, wg, gp, tm)
    s_zero = jnp.zeros((b, 2, N_HEADS, HEAD_DIM, HEAD_DIM), F32)
    (s_ctx,) = _delta_scan(qc, kc, vc, gc, s_zero, chunk, SCAN_CHUNKS_PER_STEP, False)
    ql, kl, vl, zl, gl = _delta_in(x, sh_m, sc_m, gn, wqkv, conv_a[0], wz, wg, gp, tm)
    o_f, o_b, _ = _delta_scan(ql, kl, vl, gl, s_ctx, chunk, SCAN_CHUNKS_PER_STEP, True)
    wr = _router_weights(router_g[0], router_e[0])
    x1, hf, rt = _delta_out(o_f, o_b, zl, x, gt_m, onorm_a[0].reshape(1, HEAD_DIM), w_out_a[0].astype(BF16),
                            norm_ffn[0].reshape(1, d), sh_f, sc_f, wr, tm)
    fn = final_norm.reshape(1, d)
    xb0 = jnp.zeros(((-(-(b * s) // MOE_BLOCK) + N_CLASSES) * MOE_BLOCK, d + LANES), F32)
    x2, xb0 = _moe(hf, rt, x1, gt_f, fn, w1, w3, w2, 0, xb0, s, False, MOE_BLOCK)
    x2 = x2.reshape(b, s, d)

    sh_m, sc_m, gt_m, sh_f, sc_f, gt_f = mods(1, slice(0, b))
    wr = _router_weights(router_g[1], router_e[1])
    x1, hf, rt = _sconv(x2, sh_m, sc_m, norm_mix[1].reshape(1, d), w_in_b[0].astype(BF16), conv_b[0],
                        w_out_b[0].astype(BF16), gt_m, norm_ffn[1].reshape(1, d), sh_f, sc_f, wr, tm)
    out, _ = _moe(hf, rt, x1, gt_f, fn, w1, w3, w2, 1, xb0, s, True, MOE_BLOCK)
    return out.reshape(b, s, d)
```

```python
import functools

import jax
import jax.numpy as jnp
from jax import lax
from jax.experimental import pallas as pl
from jax.experimental.pallas import tpu as pltpu

F32 = jnp.float32
BF16 = jnp.bfloat16
I32 = jnp.int32
HIGHEST = lax.Precision.HIGHEST

EPS = 1e-6
N_HEADS = 8
HEAD_DIM = 128
GRID_W = 64
N_GROUPS = 4
EXPERTS_PER_GROUP = 8
N_EXPERTS = N_GROUPS * EXPERTS_PER_GROUP
PAIRS_PER_GROUP = EXPERTS_PER_GROUP * (EXPERTS_PER_GROUP - 1) // 2
N_CLASSES = N_GROUPS * PAIRS_PER_GROUP
_PAIRS = [(a, b) for a in range(EXPERTS_PER_GROUP) for b in range(a + 1, EXPERTS_PER_GROUP)]
CLASS_LOCAL_A = [a for _ in range(N_GROUPS) for a, _ in _PAIRS]
CLASS_LOCAL_B = [b for _ in range(N_GROUPS) for _, b in _PAIRS]
MOE_BLOCK = 256
N_MOD = 6
LANES = 128
SCAN_CHUNK = 64
SCAN_CHUNKS_PER_STEP = 2
SEQ_HALO = 8
VMEM_LIMIT = 48 * 1024 * 1024


def _params(*sem):
    return pltpu.CompilerParams(dimension_semantics=sem, vmem_limit_bytes=VMEM_LIMIT)


def _dot(a, b, precision=None):
    return jnp.dot(a, b, preferred_element_type=F32, precision=precision)


def _dot_nt(a, b, precision=None):
    return lax.dot_general(a, b, (((1,), (1,)), ((), ())), preferred_element_type=F32, precision=precision)


def _dot_tn(a, b, precision=None):
    return lax.dot_general(a, b, (((0,), (0,)), ((), ())), preferred_element_type=F32, precision=precision)


def _silu(x):
    return x * jax.nn.sigmoid(x)


def _rms_mod(x, g, shift, scale):
    ms = jnp.mean(x * x, axis=-1, keepdims=True)
    return x * lax.rsqrt(ms + EPS) * g * (1.0 + scale) + shift


def _ada_kernel(c_ref, w_ref, b_ref, o_ref):
    o_ref[0] = _dot(_silu(c_ref[...]), w_ref[0], HIGHEST) + b_ref[0]


def _ada_params(cond, ada_w, ada_b):
    depth, d, n = ada_w.shape
    tn = 512
    r = cond.shape[0]
    return pl.pallas_call(
        _ada_kernel,
        grid=(depth, n // tn),
        in_specs=[pl.BlockSpec((r, d), lambda l, j: (0, 0)),
                  pl.BlockSpec((1, d, tn), lambda l, j: (l, 0, j)),
                  pl.BlockSpec((1, 1, tn), lambda l, j: (l, 0, j))],
        out_specs=pl.BlockSpec((1, r, tn), lambda l, j: (l, 0, j)),
        out_shape=jax.ShapeDtypeStruct((depth, r, n), F32),
        compiler_params=_params("parallel", "parallel"),
    )(cond, ada_w, ada_b.reshape(depth, 1, n))


def _delta_in_kernel(xm_ref, xp_ref, xn_ref, sh_ref, sc_ref, g_ref, wqkv_ref, cw_ref, wz_ref, wg_ref, gp_ref, tri_ref,
                     q_ref, k_ref, v_ref, z_ref, gate_ref, *, tm, ck):
    i = pl.program_id(1)
    n = pl.num_programs(1)
    g, sh, sc = g_ref[...], sh_ref[0], sc_ref[0]
    hm = _rms_mod(xm_ref[0], g, sh, sc)
    hp = _rms_mod(xp_ref[0], g, sh, sc) * (i > 0).astype(F32)
    hn = _rms_mod(xn_ref[0], g, sh, sc) * (i < n - 1).astype(F32)
    hm16 = hm.astype(BF16)
    hext = jnp.concatenate([hp, hm, hn], axis=0).astype(BF16)
    rows = tm + 2 * SEQ_HALO
    d = hm.shape[1]
    outs = (q_ref, k_ref, v_ref)
    for j in range(3 * d // ck):
        p = _dot(hext, wqkv_ref[:, j * ck:(j + 1) * ck])
        cw = cw_ref[:, j * ck:(j + 1) * ck]
        y = cw[0:1] * pltpu.roll(p, 1, 0) + cw[1:2] * p + cw[2:3] * pltpu.roll(p, rows - 1, 0)
        y = _silu(y[SEQ_HALO:SEQ_HALO + tm])
        which, col0 = (j * ck) // d, (j * ck) % d
        for hh in range(ck // HEAD_DIM):
            yh = y[:, hh * HEAD_DIM:(hh + 1) * HEAD_DIM]
            if which < 2:
                inv = lax.rsqrt(jnp.sum(yh * yh, axis=-1, keepdims=True) + EPS)
                yh = yh * (inv * HEAD_DIM ** -0.5 if which == 0 else inv)
            c0 = col0 + hh * HEAD_DIM
            outs[which][0, :, c0:c0 + HEAD_DIM] = yh.astype(q_ref.dtype)
    z_ref[0] = _dot(hm16, wz_ref[...]).astype(z_ref.dtype)
    pg = _dot(hm16, wg_ref[...])
    lane = lax.broadcasted_iota(I32, pg.shape, 1) % LANES
    xg = pg + gp_ref[1:2]
    softplus = jnp.maximum(xg, 0.0) + jnp.log1p(jnp.exp(-jnp.abs(xg)))
    gval = -jnp.exp(gp_ref[0:1]) * softplus
    g_hi = gval.astype(BF16)
    r1 = gval - g_hi.astype(F32)
    g_mid = r1.astype(BF16)
    g_lo = (r1 - g_mid.astype(F32)).astype(BF16)
    cums = []
    for direction in range(2):
        sl = slice(direction * LANES, (direction + 1) * LANES)
        terms = jnp.concatenate([g_hi[:, sl], g_mid[:, sl], g_lo[:, sl]], axis=1)
        cs = _dot(tri_ref[direction], terms)
        cums.append(cs[:, :LANES] + cs[:, LANES:2 * LANES] + cs[:, 2 * LANES:])
    gcum = jnp.concatenate(cums, axis=1)
    gate_ref[0] = jnp.where(lane < N_HEADS, jax.nn.sigmoid(pg), jnp.where(
        lane < 2 * N_HEADS, gval, jnp.where(lane < 3 * N_HEADS, gcum, 0.0)))


def _delta_in(x, sh, sc, gnorm, wqkv, conv_w, wz, wg, gp, tm):
    b, s, d = x.shape
    tm = min(tm, s)
    hb = tm // SEQ_HALO
    last = s // SEQ_HALO - 1
    row = lambda bb, i: (bb, i, 0)
    vec = lambda bb, i: (bb, 0, 0)
    full = lambda bb, i: (0, 0)
    outs = [jax.ShapeDtypeStruct((b, s, d), BF16)] * 4 + [jax.ShapeDtypeStruct((b, s, 2 * LANES), F32)]
    ri = lax.broadcasted_iota(I32, (tm, tm), 0)
    ci = lax.broadcasted_iota(I32, (tm, tm), 1)
    same = (ri // SCAN_CHUNK) == (ci // SCAN_CHUNK)
    tri = jnp.stack([same & (ri >= ci), same & (ri <= ci)]).astype(BF16)
    return pl.pallas_call(
        functools.partial(_delta_in_kernel, tm=tm, ck=512),
        grid=(b, s // tm),
        in_specs=[pl.BlockSpec((1, tm, d), row),
                  pl.BlockSpec((1, SEQ_HALO, d), lambda bb, i: (bb, jnp.maximum(i * hb - 1, 0), 0)),
                  pl.BlockSpec((1, SEQ_HALO, d), lambda bb, i: (bb, jnp.minimum((i + 1) * hb, last), 0)),
                  pl.BlockSpec((1, 1, d), vec), pl.BlockSpec((1, 1, d), vec),
                  pl.BlockSpec((1, d), full),
                  pl.BlockSpec((d, 3 * d), full), pl.BlockSpec((3, 3 * d), full),
                  pl.BlockSpec((d, d), full), pl.BlockSpec((d, 2 * LANES), full),
                  pl.BlockSpec((2, 2 * LANES), full), pl.BlockSpec((2, tm, tm), lambda bb, i: (0, 0, 0))],
        out_specs=[pl.BlockSpec((1, tm, d), row)] * 4 + [pl.BlockSpec((1, tm, 2 * LANES), row)],
        out_shape=outs,
        compiler_params=_params("parallel", "parallel"),
    )(x, x, x, sh, sc, gnorm, wqkv, conv_w, wz, wg, gp, tri)


def _scan_kernel(qf_ref, kf_ref, vf_ref, gf_ref, qb_ref, kb_ref, vb_ref, gb_ref, s0_ref, *out_refs, c, cps,
                 with_output):
    if with_output:
        of_ref, ob_ref, s_ref = out_refs
    else:
        (s_ref,) = out_refs
        of_ref = ob_ref = None
    step = pl.program_id(1)

    @pl.when(step == 0)
    def _():
        s_ref[...] = s0_ref[...]

    in_refs = ((qf_ref, kf_ref, vf_ref, gf_ref, of_ref), (qb_ref, kb_ref, vb_ref, gb_ref, ob_ref))
    heads = range(N_HEADS)
    ri = lax.broadcasted_iota(I32, (c, c), 0)
    ci = lax.broadcasted_iota(I32, (c, c), 1)
    eye = (ri == ci).astype(F32)
    incl = (ri >= ci, ri <= ci)
    strict = (ri > ci, ri < ci)

    def rows(dd, j):
        r0 = (j if dd == 0 else cps - 1 - j) * c
        return slice(r0, r0 + c)

    def lanes(h):
        return slice(h * HEAD_DIM, (h + 1) * HEAD_DIM)

    chunks = [(dd, j) for dd in range(2) for j in range(cps)]
    chains = [(dd, j, h) for dd, j in chunks for h in heads]
    gates = {dj: in_refs[dj[0]][3][0, rows(*dj), :] for dj in chunks}
    gates_t = {dj: jnp.concatenate([gates[dj], jnp.zeros((LANES - c, LANES), F32)], axis=0).T[:, :c]
               for dj in chunks}
    tot_row = {dj: gates[dj][c - 1:c, :] if dj[0] == 0 else gates[dj][0:1, :] for dj in chunks}

    def load(which, key):
        dd, j, h = key
        return in_refs[dd][which][0, rows(dd, j), lanes(h)]

    q16 = {key: load(0, key) for key in chains}
    k16 = {key: load(1, key) for key in chains}
    qf = {key: q16[key].astype(F32) for key in chains}
    kf = {key: k16[key].astype(F32) for key in chains}
    vf = {key: load(2, key).astype(F32) for key in chains}
    cum = 2 * N_HEADS
    beta = {(dd, j, h): gates[(dd, j)][:, h:h + 1] for dd, j, h in chains}
    gcc = {(dd, j, h): gates[(dd, j)][:, cum + h:cum + h + 1] for dd, j, h in chains}
    gcr = {(dd, j, h): gates_t[(dd, j)][cum + h:cum + h + 1, :] for dd, j, h in chains}
    tot = {(dd, j, h): tot_row[(dd, j)][:, cum + h:cum + h + 1] for dd, j, h in chains}
    decay = {key: jnp.exp(jnp.where(incl[key[0]], gcc[key] - gcr[key], -jnp.inf)) for key in chains}
    e = {key: jnp.exp(gcc[key]) for key in chains}
    kb = {key: kf[key] * beta[key] for key in chains}
    a = {key: _dot_nt(jnp.concatenate([kb[key].astype(BF16), q16[key]], axis=0), k16[key]) for key in chains}
    lmat = {key: jnp.where(strict[key[0]], a[key][:c] * decay[key], 0.0) for key in chains}
    qk = {key: (a[key][c:] * decay[key]).astype(BF16) for key in chains}
    p = {key: eye - lmat[key] for key in chains}
    l16 = {key: lmat[key].astype(BF16) for key in chains}
    m = {key: _dot(l16[key], l16[key]) for key in chains}
    span = 2
    while span < c:
        span *= 2
        m16 = {key: m[key].astype(BF16) for key in chains}
        if span < c:
            pm = {key: _dot(jnp.concatenate([p[key], m[key]], axis=0).astype(BF16), m16[key]) for key in chains}
            p = {key: p[key] + pm[key][:c] for key in chains}
            m = {key: pm[key][c:] for key in chains}
        else:
            p = {key: p[key] + _dot(p[key].astype(BF16), m16[key]) for key in chains}
    rhs = {key: jnp.concatenate([vf[key] * beta[key], kb[key] * e[key]], axis=1).astype(BF16) for key in chains}
    sol = {key: _dot(p[key].astype(BF16), rhs[key]) for key in chains}
    wq = {key: jnp.concatenate([sol[key][:, HEAD_DIM:], qf[key] * e[key]], axis=0).astype(BF16) for key in chains}
    k_dec = {key: (kf[key] * jnp.exp(tot[key] - gcc[key])).astype(BF16) for key in chains}
    s_tot = {key: jnp.exp(tot[key]) for key in chains}

    state = {(dd, h): s_ref[0, dd, h] for dd in range(2) for h in heads}
    for j in range(cps):
        now = [(dd, j, h) for dd in range(2) for h in heads]
        s16 = {key: state[(key[0], key[2])].astype(BF16) for key in now}
        ws = {key: _dot(wq[key], s16[key]) for key in now}
        v16 = {key: (sol[key][:, :HEAD_DIM] - ws[key][:c]).astype(BF16) for key in now}
        for key in now:
            sk = (key[0], key[2])
            state[sk] = state[sk] * s_tot[key] + _dot_tn(k_dec[key], v16[key])
        if with_output:
            o = {key: ws[key][c:] + _dot(qk[key], v16[key]) for key in now}
            for dd, _, h in now:
                in_refs[dd][4][0, rows(dd, j), lanes(h)] = o[(dd, j, h)].astype(of_ref.dtype)
    for dd in range(2):
        for h in heads:
            s_ref[0, dd, h] = state[(dd, h)]


def _delta_scan(q, k, v, gates, s0, c, cps, with_output):
    b, s, d = q.shape
    rb = c * cps
    nb = s // rb
    fwd = lambda bb, i: (bb, i, 0)
    bwd = lambda bb, i: (bb, nb - 1 - i, 0)
    state_spec = pl.BlockSpec((1, 2, N_HEADS, HEAD_DIM, HEAD_DIM), lambda bb, i: (bb, 0, 0, 0, 0))
    state_shape = jax.ShapeDtypeStruct((b, 2, N_HEADS, HEAD_DIM, HEAD_DIM), F32)
    out_specs, out_shape = [state_spec], [state_shape]
    if with_output:
        out_specs = [pl.BlockSpec((1, rb, d), fwd), pl.BlockSpec((1, rb, d), bwd)] + out_specs
        out_shape = [jax.ShapeDtypeStruct((b, s, d), BF16)] * 2 + out_shape
    return pl.pallas_call(
        functools.partial(_scan_kernel, c=c, cps=cps, with_output=with_output),
        grid=(b, nb),
        in_specs=([pl.BlockSpec((1, rb, d), fwd)] * 3 + [pl.BlockSpec((1, rb, LANES), fwd)]
                  + [pl.BlockSpec((1, rb, d), bwd)] * 3
                  + [pl.BlockSpec((1, rb, LANES), lambda bb, i: (bb, nb - 1 - i, 1)), state_spec]),
        out_specs=out_specs,
        out_shape=out_shape,
        compiler_params=_params("parallel", "arbitrary"),
    )(q, k, v, gates, q, k, v, gates, s0)


def _ffn_pre(x1, gf, shf, scf, wr_ref, x1_ref, hf_ref, rt_ref):
    x1_ref[...] = x1
    hf = _rms_mod(x1, gf, shf, scf)
    d = hf.shape[1]
    hf_ref[:, :d] = hf
    hi = hf.astype(BF16)
    lo = (hf - hi.astype(F32)).astype(BF16)
    a = _dot(hi, wr_ref[...])
    logits = a[:, :LANES] + a[:, LANES:] + _dot(lo, wr_ref[:, :LANES])
    lane = lax.broadcasted_iota(I32, logits.shape, 1)
    lanef = lane.astype(F32)
    big = float(LANES)
    gl = jnp.where(lane < N_GROUPS, logits, -jnp.inf)
    gmax = jnp.max(gl, axis=-1, keepdims=True)
    gi = jnp.min(jnp.where(gl == gmax, lanef, big), axis=-1, keepdims=True)
    pg_sel = 1.0 / jnp.sum(jnp.exp(gl - gmax), axis=-1, keepdims=True)
    rel = lanef - (N_GROUPS + EXPERTS_PER_GROUP * gi)
    el = jnp.where((rel >= 0.0) & (rel < float(EXPERTS_PER_GROUP)), logits, -jnp.inf)
    m1 = jnp.max(el, axis=-1, keepdims=True)
    i1 = jnp.min(jnp.where(el == m1, lanef, big), axis=-1, keepdims=True)
    el2 = jnp.where(lanef == i1, -jnp.inf, el)
    m2 = jnp.max(el2, axis=-1, keepdims=True)
    i2 = jnp.min(jnp.where(el2 == m2, lanef, big), axis=-1, keepdims=True)
    r2 = jnp.exp(m2 - m1)
    w1 = pg_sel / (1.0 + r2)
    w2 = pg_sel * r2 / (1.0 + r2)
    first_low = i1 < i2
    a_loc = jnp.minimum(i1, i2) - (N_GROUPS + EXPERTS_PER_GROUP * gi)
    b_loc = jnp.maximum(i1, i2) - (N_GROUPS + EXPERTS_PER_GROUP * gi)
    pair = a_loc * (EXPERTS_PER_GROUP - 1) - a_loc * (a_loc - 1.0) * 0.5 + (b_loc - a_loc - 1.0)
    cls = gi * float(PAIRS_PER_GROUP) + pair
    rt = jnp.where(lane == 0, jnp.where(first_low, w1, w2), jnp.where(
        lane == 1, jnp.where(first_low, w2, w1), jnp.where(lane == 2, cls, 0.0)))
    rt_ref[...] = rt
    hf_ref[:, d:] = rt


def _delta_out_kernel(of_ref, ob_ref, z_ref, x_ref, gt_ref, on_ref, wo_ref, gf_ref, shf_ref, scf_ref, wr_ref,
                      x1_ref, hf_ref, rt_ref):
    o = of_ref[0].astype(F32) + ob_ref[0].astype(F32)
    z = z_ref[0].astype(F32)
    onorm = on_ref[...]
    parts = []
    for h in range(N_HEADS):
        oh = o[:, h * HEAD_DIM:(h + 1) * HEAD_DIM]
        parts.append(oh * lax.rsqrt(jnp.mean(oh * oh, axis=-1, keepdims=True) + EPS) * onorm)
    y = jnp.concatenate(parts, axis=1) * _silu(z)
    x1 = x_ref[0] + gt_ref[0] * _dot(y.astype(BF16), wo_ref[...])
    _ffn_pre(x1, gf_ref[...], shf_ref[0], scf_ref[0], wr_ref, x1_ref, hf_ref, rt_ref)


def _delta_out(o_f, o_b, z, x, gt, onorm, wo, gf, shf, scf, wr, tm):
    b, s, d = x.shape
    t = b * s
    nt = s // tm
    row = lambda bb, i: (bb, i, 0)
    vec = lambda bb, i: (bb, 0, 0)
    full = lambda bb, i: (0, 0)
    flat = lambda bb, i: (bb * nt + i, 0)
    return pl.pallas_call(
        _delta_out_kernel,
        grid=(b, nt),
        in_specs=[pl.BlockSpec((1, tm, d), row), pl.BlockSpec((1, tm, d), row),
                  pl.BlockSpec((1, tm, d), row), pl.BlockSpec((1, tm, d), row),
                  pl.BlockSpec((1, 1, d), vec), pl.BlockSpec((1, HEAD_DIM), full),
                  pl.BlockSpec((d, d), full), pl.BlockSpec((1, d), full),
                  pl.BlockSpec((1, 1, d), vec), pl.BlockSpec((1, 1, d), vec),
                  pl.BlockSpec((d, 2 * LANES), full)],
        out_specs=[pl.BlockSpec((tm, d), flat), pl.BlockSpec((tm, d + LANES), flat), pl.BlockSpec((tm, LANES), flat)],
        out_shape=[jax.ShapeDtypeStruct((t, d), F32), jax.ShapeDtypeStruct((t, d + LANES), F32),
                   jax.ShapeDtypeStruct((t, LANES), F32)],
        compiler_params=_params("parallel", "parallel"),
    )(o_f, o_b, z, x, gt, onorm, wo, gf, shf, scf, wr)


def _sconv_kernel(xm_ref, xp_ref, xn_ref, sh_ref, sc_ref, g_ref, win_ref, cw_ref, wo_ref, gt_ref,
                  gf_ref, shf_ref, scf_ref, wr_ref, x1_ref, hf_ref, rt_ref, *, tm, ck):
    i = pl.program_id(1)
    n = pl.num_programs(1)
    g, sh, sc = g_ref[...], sh_ref[0], sc_ref[0]
    xm = xm_ref[0]
    hm = _rms_mod(xm, g, sh, sc)
    hp = _rms_mod(xp_ref[0], g, sh, sc) * (i > 0).astype(F32)
    hn = _rms_mod(xn_ref[0], g, sh, sc) * (i < n - 1).astype(F32)
    hm16 = hm.astype(BF16)
    hext = jnp.concatenate([hp, hm, hn], axis=0).astype(BF16)
    d = xm.shape[1]
    col = lax.broadcasted_iota(I32, (tm, 1), 0) % GRID_W
    acc = jnp.zeros((tm, d), F32)
    for j in range(d // ck):
        c0 = j * ck
        gate_b = _dot(hm16, win_ref[:, c0:c0 + ck])
        cw = cw_ref[:, c0:c0 + ck]
        if c0 < d // 2:
            u = _dot(hm16, win_ref[:, d + c0:d + c0 + ck]) * _dot(hm16, win_ref[:, 2 * d + c0:2 * d + c0 + ck])
            left = jnp.where(col == 0, 0.0, pltpu.roll(u, 1, 0))
            right = jnp.where(col == GRID_W - 1, 0.0, pltpu.roll(u, tm - 1, 0))
            y = cw[0:1] * left + cw[1:2] * u + cw[2:3] * right
        else:
            u = _dot(hext, win_ref[:, d + c0:d + c0 + ck]) * _dot(hext, win_ref[:, 2 * d + c0:2 * d + c0 + ck])
            y = cw[0:1] * u[0:tm] + cw[1:2] * u[GRID_W:GRID_W + tm] + cw[2:3] * u[2 * GRID_W:2 * GRID_W + tm]
        acc = acc + _dot((gate_b * y).astype(BF16), wo_ref[c0:c0 + ck, :])
    x1 = xm + gt_ref[0] * acc
    _ffn_pre(x1, gf_ref[...], shf_ref[0], scf_ref[0], wr_ref, x1_ref, hf_ref, rt_ref)


def _sconv(x, sh, sc, gnorm, win, conv_w, wo, gt, gf, shf, scf, wr, tm):
    b, s, d = x.shape
    t = b * s
    nt = s // tm
    hb = tm // GRID_W
    last = s // GRID_W - 1
    row = lambda bb, i: (bb, i, 0)
    vec = lambda bb, i: (bb, 0, 0)
    full = lambda bb, i: (0, 0)
    flat = lambda bb, i: (bb * nt + i, 0)
    return pl.pallas_call(
        functools.partial(_sconv_kernel, tm=tm, ck=512),
        grid=(b, nt),
        in_specs=[pl.BlockSpec((1, tm, d), row),
                  pl.BlockSpec((1, GRID_W, d), lambda bb, i: (bb, jnp.maximum(i * hb - 1, 0), 0)),
                  pl.BlockSpec((1, GRID_W, d), lambda bb, i: (bb, jnp.minimum((i + 1) * hb, last), 0)),
                  pl.BlockSpec((1, 1, d), vec), pl.BlockSpec((1, 1, d), vec), pl.BlockSpec((1, d), full),
                  pl.BlockSpec((d, 3 * d), full), pl.BlockSpec((3, d), full), pl.BlockSpec((d, d), full),
                  pl.BlockSpec((1, 1, d), vec), pl.BlockSpec((1, d), full),
                  pl.BlockSpec((1, 1, d), vec), pl.BlockSpec((1, 1, d), vec),
                  pl.BlockSpec((d, 2 * LANES), full)],
        out_specs=[pl.BlockSpec((tm, d), flat), pl.BlockSpec((tm, d + LANES), flat), pl.BlockSpec((tm, LANES), flat)],
        out_shape=[jax.ShapeDtypeStruct((t, d), F32), jax.ShapeDtypeStruct((t, d + LANES), F32),
                   jax.ShapeDtypeStruct((t, LANES), F32)],
        compiler_params=_params("parallel", "parallel"),
    )(x, x, x, sh, sc, gnorm, win, conv_w, wo, gt, gf, shf, scf, wr)


def _rank_kernel(c_ref, dest_ref, ends_ref, cnt_ref, start_ref, carry_ref, *, tm, bm):
    p = pl.program_id(0)
    i = pl.program_id(1)
    onehot = lax.broadcasted_iota(I32, (LANES, tm), 0) == c_ref[0:1, :]
    ohf = onehot.astype(F32)
    per_class = jnp.broadcast_to(jnp.sum(ohf, axis=1, keepdims=True), (LANES, LANES))

    @pl.when(jnp.logical_and(p == 0, i == 0))
    def _():
        cnt_ref[...] = jnp.zeros_like(cnt_ref)

    @pl.when(p == 0)
    def _():
        cnt_ref[...] += per_class

    @pl.when(jnp.logical_and(p == 1, i == 0))
    def _():
        padded = ((cnt_ref[...].astype(I32) + (bm - 1)) & (-bm)).astype(F32)
        before = (lax.broadcasted_iota(I32, (LANES, LANES), 0) > lax.broadcasted_iota(I32, (LANES, LANES), 1))
        start = _dot(before.astype(F32), padded, HIGHEST)
        start_ref[...] = start
        ends_ref[...] = (start + padded).astype(I32)
        carry_ref[...] = jnp.zeros_like(carry_ref)

    @pl.when(p == 1)
    def _():
        earlier = (lax.broadcasted_iota(I32, (tm, tm), 0) < lax.broadcasted_iota(I32, (tm, tm), 1)).astype(BF16)
        slot = _dot(onehot.astype(BF16), earlier) + carry_ref[:, 0:1] + start_ref[:, 0:1]
        dest = jnp.sum(ohf * slot, axis=0, keepdims=True)
        dest_ref[...] = jnp.concatenate([dest, jnp.zeros((7, tm), F32)], axis=0).astype(I32)
        carry_ref[...] += per_class


def _rank(cls8, tm, bm):
    t = cls8.shape[1]
    return pl.pallas_call(
        functools.partial(_rank_kernel, tm=tm, bm=bm),
        grid=(2, t // tm),
        in_specs=[pl.BlockSpec((8, tm), lambda p, i: (0, i))],
        out_specs=[pl.BlockSpec((8, tm), lambda p, i: (0, i * p)), pl.BlockSpec((LANES, LANES), lambda p, i: (0, 0))],
        out_shape=[jax.ShapeDtypeStruct((8, t), I32), jax.ShapeDtypeStruct((LANES, LANES), I32)],
        scratch_shapes=[pltpu.VMEM((LANES, LANES), F32)] * 3,
        compiler_params=_params("arbitrary", "arbitrary"),
    )(cls8)


def _row_copy(src_ref, src_row, dst_ref, dst_row, sem):
    return pltpu.make_async_copy(src_ref.at[pl.ds(src_row, 1), :], dst_ref.at[pl.ds(dst_row, 1), :], sem)


def _dispatch_kernel(dest_ref, hf_ref, xb_in_ref, xb_ref, sem, *, td):
    del xb_in_ref

    def issue(r, carry):
        _row_copy(hf_ref, r, xb_ref, dest_ref[0, 0, r], sem).start()
        return carry

    for r in range(td):
        issue(r, 0)

    def drain(r, carry):
        _row_copy(hf_ref, 0, xb_ref, 0, sem).wait()
        return carry

    lax.fori_loop(0, td, drain, 0, unroll=8)


def _dispatch(dest3, hf, xb0, td):
    t, d = hf.shape
    cap = xb0.shape[0]
    return pl.pallas_call(
        functools.partial(_dispatch_kernel, td=td),
        grid=(t // td,),
        in_specs=[pl.BlockSpec((1, 1, td), lambda i: (i, 0, 0), memory_space=pltpu.SMEM),
                  pl.BlockSpec((td, d), lambda i: (i, 0)),
                  pl.BlockSpec(memory_space=pl.ANY)],
        out_specs=pl.BlockSpec(memory_space=pl.ANY),
        out_shape=jax.ShapeDtypeStruct((cap, d), hf.dtype),
        scratch_shapes=[pltpu.SemaphoreType.DMA],
        input_output_aliases={2: 0},
        compiler_params=_params("arbitrary"),
    )(dest3, hf, xb0)


def _expert_kernel(grp_ref, la_ref, lb_ref, bs_ref, nu_ref, x_ref, w1_hbm, w3_hbm, w2_hbm, y_ref,
                   res1, res3, res2, stage_up, stage_down, sems, *, layer):
    del bs_ref
    i = pl.program_id(0)
    d = y_ref.shape[1]
    grp = grp_ref[i]

    @pl.when(jnp.logical_or(i == 0, grp != grp_ref[jnp.maximum(i - 1, 0)]))
    def _():
        plan = []
        for e in range(EXPERTS_PER_GROUP):
            plan += [(w1_hbm, res1, stage_up.at[0], 0, e), (w3_hbm, res3, stage_up.at[1], 1, e),
                     (w2_hbm, res2, stage_down, 2, e)]

        def copy(step):
            src, _, stage, sem, e = plan[step]
            return pltpu.make_async_copy(src.at[layer, grp * EXPERTS_PER_GROUP + e], stage, sems.at[sem])

        copy(0).start()
        for step, (_, dst, stage, _, e) in enumerate(plan):
            if step + 1 < len(plan):
                copy(step + 1).start()
            copy(step).wait()
            dst[e] = stage[...].astype(BF16)

    @pl.when(i < nu_ref[0])
    def _():
        x = x_ref[:, :d].astype(BF16)

        def swiglu(e):
            h = _silu(_dot(x, res1[e])) * _dot(x, res3[e])
            return _dot(h.astype(BF16), res2[e])

        y_ref[...] = x_ref[:, d:d + 1] * swiglu(la_ref[i]) + x_ref[:, d + 1:d + 2] * swiglu(lb_ref[i])

    @pl.when(i >= nu_ref[0])
    def _():
        y_ref[...] = jnp.zeros_like(y_ref)


def _experts(block_grp, block_la, block_lb, block_src, n_used, xb, w1, w3, w2, layer, bm):
    cap, dx = xb.shape
    _, _, d, de = w1.shape
    grid_spec = pltpu.PrefetchScalarGridSpec(
        num_scalar_prefetch=5,
        grid=(cap // bm,),
        in_specs=[pl.BlockSpec((bm, dx), lambda i, g, la, lb, bs, nu: (bs[i], 0)),
                  pl.BlockSpec(memory_space=pl.ANY), pl.BlockSpec(memory_space=pl.ANY),
                  pl.BlockSpec(memory_space=pl.ANY)],
        out_specs=pl.BlockSpec((bm, d), lambda i, g, la, lb, bs, nu: (i, 0)),
        scratch_shapes=[pltpu.VMEM((EXPERTS_PER_GROUP, d, de), BF16), pltpu.VMEM((EXPERTS_PER_GROUP, d, de), BF16),
                        pltpu.VMEM((EXPERTS_PER_GROUP, de, d), BF16), pltpu.VMEM((2, d, de), F32),
                        pltpu.VMEM((de, d), F32), pltpu.SemaphoreType.DMA((3,))],
    )
    return pl.pallas_call(
        functools.partial(_expert_kernel, layer=layer),
        grid_spec=grid_spec,
        out_shape=jax.ShapeDtypeStruct((cap, d), F32),
        compiler_params=_params("arbitrary"),
    )(block_grp, block_la, block_lb, block_src, n_used, xb, w1, w3, w2)


def _combine_kernel(dest_ref, x1_ref, gt_ref, fn_ref, yb_ref, o_ref, buf, sems, *, tc, final):
    i = pl.program_id(0)
    n_tiles = pl.num_programs(0) - 1

    @pl.when(i < n_tiles)
    def _():
        slot = i % 2
        for r in range(tc):
            _row_copy(yb_ref, dest_ref[0, 0, r], buf.at[slot], r, sems.at[slot]).start()

    @pl.when(i > 0)
    def _():
        slot = (i - 1) % 2

        def drain(r, carry):
            _row_copy(yb_ref, 0, buf.at[slot], 0, sems.at[slot]).wait()
            return carry

        lax.fori_loop(0, tc, drain, 0, unroll=8)
        x2 = x1_ref[...] + gt_ref[0] * buf[slot]
        if final:
            ms = jnp.mean(x2 * x2, axis=-1, keepdims=True)
            x2 = x2 * lax.rsqrt(ms + EPS) * fn_ref[...]
        o_ref[...] = x2


def _combine(dest3, x1, gt, fnorm, yb, s, tc, final):
    t, d = x1.shape
    per_b = s // tc
    n_tiles = t // tc
    done = lambda i: jnp.maximum(i - 1, 0)
    return pl.pallas_call(
        functools.partial(_combine_kernel, tc=tc, final=final),
        grid=(n_tiles + 1,),
        in_specs=[pl.BlockSpec((1, 1, tc), lambda i: (jnp.minimum(i, n_tiles - 1), 0, 0), memory_space=pltpu.SMEM),
                  pl.BlockSpec((tc, d), lambda i: (done(i), 0)),
                  pl.BlockSpec((1, 1, d), lambda i: (done(i) // per_b, 0, 0)),
                  pl.BlockSpec((1, d), lambda i: (0, 0)),
                  pl.BlockSpec(memory_space=pl.ANY)],
        out_specs=pl.BlockSpec((tc, d), lambda i: (done(i), 0)),
        out_shape=jax.ShapeDtypeStruct((t, d), F32),
        scratch_shapes=[pltpu.VMEM((2, tc, d), F32), pltpu.SemaphoreType.DMA((2,))],
        compiler_params=_params("arbitrary"),
    )(dest3, x1, gt, fnorm, yb)


def _moe(hfx, rt, x1, gt, fnorm, w1, w3, w2, layer, xb0, s, final, bm, tr=512, td=1024):
    t = hfx.shape[0]
    cls8 = jnp.zeros((8, t), I32).at[0].set(rt[:, 2].astype(I32))
    dest8, ends2 = _rank(cls8, tr, bm)
    dest3 = dest8[0].reshape(t // td, 1, td)
    ends = ends2[:N_CLASSES, 0]
    n_blocks = xb0.shape[0] // bm
    n_used = ends[-1] // bm
    block_src = jnp.maximum(jnp.minimum(jnp.arange(n_blocks, dtype=I32), n_used - 1), 0)
    block_cls = jnp.minimum(jnp.sum((ends[None, :] <= (block_src * bm)[:, None]).astype(I32), axis=1), N_CLASSES - 1)
    block_la = jnp.asarray(CLASS_LOCAL_A, I32)[block_cls]
    block_lb = jnp.asarray(CLASS_LOCAL_B, I32)[block_cls]
    xb = _dispatch(dest3, hfx, xb0, td)
    yb = _experts(block_cls // PAIRS_PER_GROUP, block_la, block_lb, block_src, n_used.reshape(1), xb, w1, w3, w2,
                  layer, bm)
    return _combine(dest3, x1, gt, fnorm, yb, s, td, final), xb


def _router_weights(router_g, router_e):
    d = router_g.shape[0]
    w = jnp.zeros((d, LANES), F32).at[:, :N_GROUPS].set(router_g).at[:, N_GROUPS:N_GROUPS + N_EXPERTS].set(router_e)
    hi = w.astype(BF16)
    lo = (w - hi.astype(F32)).astype(BF16)
    return jnp.concatenate([hi, lo], axis=1)


def _gate_weights(w_in, a_log, dt_bias):
    d = w_in.shape[0]
    base = 4 * d
    wg = jnp.zeros((d, 2 * LANES), F32)
    gp = jnp.zeros((2, 2 * LANES), F32)
    for direction in range(2):
        o = direction * LANES
        wg = wg.at[:, o:o + N_HEADS].set(w_in[:, base + direction * N_HEADS:base + (direction + 1) * N_HEADS])
        wg = wg.at[:, o + N_HEADS:o + 2 * N_HEADS].set(
            w_in[:, base + (2 + direction) * N_HEADS:base + (3 + direction) * N_HEADS])
        wg = wg.at[:, o + 2 * N_HEADS:o + 3 * N_HEADS].set(
            w_in[:, base + (2 + direction) * N_HEADS:base + (3 + direction) * N_HEADS])
        for rep in (1, 2):
            gp = gp.at[0, o + rep * N_HEADS:o + (rep + 1) * N_HEADS].set(a_log[direction])
            gp = gp.at[1, o + rep * N_HEADS:o + (rep + 1) * N_HEADS].set(dt_bias[direction])
    return wg.astype(BF16), gp


def kernel(x, c, ctx, c_ctx, ada_w, ada_b, norm_mix, norm_ffn, w_in_a, conv_a, a_log_a, dt_bias_a, onorm_a, w_out_a,
           w_in_b, conv_b, w_out_b, router_g, router_e, w1, w3, w2, final_norm):
    b, s, d = x.shape
    chunk = SCAN_CHUNK
    tm = min(512, s)

    cond = jnp.zeros((16, d), F32).at[:b].set(c).at[b].set(c_ctx)
    mod = _ada_params(cond, ada_w, ada_b)

    def mods(layer, rows):
        m = mod[layer, rows]
        return [jnp.broadcast_to(m[:, None, k * d:(k + 1) * d], (b, 1, d)) for k in range(N_MOD)]

    sh_m, sc_m, gt_m, sh_f, sc_f, gt_f = mods(0, slice(0, b))
    csh_m, csc_m = mods(0, slice(b, b + 1))[:2]
    w_in = w_in_a[0]
    wqkv, wz = w_in[:, :3 * d].astype(BF16), w_in[:, 3 * d:4 * d].astype(BF16)
    wg, gp = _gate_weights(w_in, a_log_a[0], dt_bias_a[0])
    gn = norm_mix[0].reshape(1, d)
    qc, kc, vc, _, gc = _delta_in(ctx, csh_m, csc_m, gn, wqkv, conv_a[0], wz, wg, gp, tm)
    s_zero = jnp.zeros((b, 2, N_HEADS, HEAD_DIM, HEAD_DIM), F32)
    (s_ctx,) = _delta_scan(qc, kc, vc, gc, s_zero, chunk, SCAN_CHUNKS_PER_STEP, False)
    ql, kl, vl, zl, gl = _delta_in(x, sh_m, sc_m, gn, wqkv, conv_a[0], wz, wg, gp, tm)
    o_f, o_b, _ = _delta_scan(ql, kl, vl, gl, s_ctx, chunk, SCAN_CHUNKS_PER_STEP, True)
    wr = _router_weights(router_g[0], router_e[0])
    x1, hf, rt = _delta_out(o_f, o_b, zl, x, gt_m, onorm_a[0].reshape(1, HEAD_DIM), w_out_a[0].astype(BF16),
                            norm_ffn[0].reshape(1, d), sh_f, sc_f, wr, tm)
    fn = final_norm.reshape(1, d)
    xb0 = jnp.zeros(((-(-(b * s) // MOE_BLOCK) + N_CLASSES) * MOE_BLOCK, d + LANES), F32)
    x2, xb0 = _moe(hf, rt, x1, gt_f, fn, w1, w3, w2, 0, xb0, s, False, MOE_BLOCK)
    x2 = x2.reshape(b, s, d)

    sh_m, sc_m, gt_m, sh_f, sc_f, gt_f = mods(1, slice(0, b))
    wr = _router_weights(router_g[1], router_e[1])
    x1, hf, rt = _sconv(x2, sh_m, sc_m, norm_mix[1].reshape(1, d), w_in_b[0].astype(BF16), conv_b[0],
                        w_out_b[0].astype(BF16), gt_m, norm_ffn[1].reshape(1, d), sh_f, sc_f, wr, tm)
    out, _ = _moe(hf, rt, x1, gt_f, fn, w1, w3, w2, 1, xb0, s, True, MOE_BLOCK)
    return out.reshape(b, s, d)
```

```python
import functools

import jax
import jax.numpy as jnp
from jax import lax
from jax.experimental import pallas as pl
from jax.experimental.pallas import tpu as pltpu

F32 = jnp.float32
BF16 = jnp.bfloat16
I32 = jnp.int32
HIGHEST = lax.Precision.HIGHEST

EPS = 1e-6
N_HEADS = 8
HEAD_DIM = 128
GRID_W = 64
N_GROUPS = 4
EXPERTS_PER_GROUP = 8
N_EXPERTS = N_GROUPS * EXPERTS_PER_GROUP
PAIRS_PER_GROUP = EXPERTS_PER_GROUP * (EXPERTS_PER_GROUP - 1) // 2
N_CLASSES = N_GROUPS * PAIRS_PER_GROUP
_PAIRS = [(a, b) for a in range(EXPERTS_PER_GROUP) for b in range(a + 1, EXPERTS_PER_GROUP)]
CLASS_LOCAL_A = [a for _ in range(N_GROUPS) for a, _ in _PAIRS]
CLASS_LOCAL_B = [b for _ in range(N_GROUPS) for _, b in _PAIRS]
MOE_BLOCK = 256
N_MOD = 6
LANES = 128
SCAN_CHUNK = 64
SCAN_CHUNKS_PER_STEP = 2
SEQ_HALO = 8
VMEM_LIMIT = 48 * 1024 * 1024


def _params(*sem):
    return pltpu.CompilerParams(dimension_semantics=sem, vmem_limit_bytes=VMEM_LIMIT)


def _dot(a, b, precision=None):
    return jnp.dot(a, b, preferred_element_type=F32, precision=precision)


def _dot_nt(a, b, precision=None):
    return lax.dot_general(a, b, (((1,), (1,)), ((), ())), preferred_element_type=F32, precision=precision)


def _dot_tn(a, b, precision=None):
    return lax.dot_general(a, b, (((0,), (0,)), ((), ())), preferred_element_type=F32, precision=precision)


def _silu(x):
    return x * jax.nn.sigmoid(x)


def _rms_mod(x, g, shift, scale):
    ms = jnp.mean(x * x, axis=-1, keepdims=True)
    return x * lax.rsqrt(ms + EPS) * g * (1.0 + scale) + shift


def _ada_kernel(c_ref, w_ref, b_ref, o_ref):
    o_ref[0] = _dot(_silu(c_ref[...]), w_ref[0], HIGHEST) + b_ref[0]


def _ada_params(cond, ada_w, ada_b):
    depth, d, n = ada_w.shape
    tn = 512
    r = cond.shape[0]
    return pl.pallas_call(
        _ada_kernel,
        grid=(depth, n // tn),
        in_specs=[pl.BlockSpec((r, d), lambda l, j: (0, 0)),
                  pl.BlockSpec((1, d, tn), lambda l, j: (l, 0, j)),
                  pl.BlockSpec((1, 1, tn), lambda l, j: (l, 0, j))],
        out_specs=pl.BlockSpec((1, r, tn), lambda l, j: (l, 0, j)),
        out_shape=jax.ShapeDtypeStruct((depth, r, n), F32),
        compiler_params=_params("parallel", "parallel"),
    )(cond, ada_w, ada_b.reshape(depth, 1, n))


def _delta_in_kernel(xm_ref, xp_ref, xn_ref, sh_ref, sc_ref, g_ref, wqkv_ref, cw_ref, wz_ref, wg_ref, gp_ref, tri_ref,
                     q_ref, k_ref, v_ref, z_ref, gate_ref, *, tm, ck):
    i = pl.program_id(1)
    n = pl.num_programs(1)
    g, sh, sc = g_ref[...], sh_ref[0], sc_ref[0]
    hm = _rms_mod(xm_ref[0], g, sh, sc)
    hp = _rms_mod(xp_ref[0], g, sh, sc) * (i > 0).astype(F32)
    hn = _rms_mod(xn_ref[0], g, sh, sc) * (i < n - 1).astype(F32)
    hm16 = hm.astype(BF16)
    hext = jnp.concatenate([hp, hm, hn], axis=0).astype(BF16)
    rows = tm + 2 * SEQ_HALO
    d = hm.shape[1]
    outs = (q_ref, k_ref, v_ref)
    for j in range(3 * d // ck):
        p = _dot(hext, wqkv_ref[:, j * ck:(j + 1) * ck])
        cw = cw_ref[:, j * ck:(j + 1) * ck]
        y = cw[0:1] * pltpu.roll(p, 1, 0) + cw[1:2] * p + cw[2:3] * pltpu.roll(p, rows - 1, 0)
        y = _silu(y[SEQ_HALO:SEQ_HALO + tm])
        which, col0 = (j * ck) // d, (j * ck) % d
        for hh in range(ck // HEAD_DIM):
            yh = y[:, hh * HEAD_DIM:(hh + 1) * HEAD_DIM]
            if which < 2:
                inv = lax.rsqrt(jnp.sum(yh * yh, axis=-1, keepdims=True) + EPS)
                yh = yh * (inv * HEAD_DIM ** -0.5 if which == 0 else inv)
            c0 = col0 + hh * HEAD_DIM
            outs[which][0, :, c0:c0 + HEAD_DIM] = yh.astype(q_ref.dtype)
    z_ref[0] = _dot(hm16, wz_ref[...]).astype(z_ref.dtype)
    pg = _dot(hm16, wg_ref[...])
    lane = lax.broadcasted_iota(I32, pg.shape, 1) % LANES
    xg = pg + gp_ref[1:2]
    softplus = jnp.maximum(xg, 0.0) + jnp.log1p(jnp.exp(-jnp.abs(xg)))
    gval = -jnp.exp(gp_ref[0:1]) * softplus
    g_hi = gval.astype(BF16)
    r1 = gval - g_hi.astype(F32)
    g_mid = r1.astype(BF16)
    g_lo = (r1 - g_mid.astype(F32)).astype(BF16)
    cums = []
    for direction in range(2):
        sl = slice(direction * LANES, (direction + 1) * LANES)
        terms = jnp.concatenate([g_hi[:, sl], g_mid[:, sl], g_lo[:, sl]], axis=1)
        cs = _dot(tri_ref[direction], terms)
        cums.append(cs[:, :LANES] + cs[:, LANES:2 * LANES] + cs[:, 2 * LANES:])
    gcum = jnp.concatenate(cums, axis=1)
    gate_ref[0] = jnp.where(lane < N_HEADS, jax.nn.sigmoid(pg), jnp.where(
        lane < 2 * N_HEADS, gval, jnp.where(lane < 3 * N_HEADS, gcum, 0.0)))


def _delta_in(x, sh, sc, gnorm, wqkv, conv_w, wz, wg, gp, tm):
    b, s, d = x.shape
    tm = min(tm, s)
    hb = tm // SEQ_HALO
    last = s // SEQ_HALO - 1
    row = lambda bb, i: (bb, i, 0)
    vec = lambda bb, i: (bb, 0, 0)
    full = lambda bb, i: (0, 0)
    outs = [jax.ShapeDtypeStruct((b, s, d), BF16)] * 4 + [jax.ShapeDtypeStruct((b, s, 2 * LANES), F32)]
    ri = lax.broadcasted_iota(I32, (tm, tm), 0)
    ci = lax.broadcasted_iota(I32, (tm, tm), 1)
    same = (ri // SCAN_CHUNK) == (ci // SCAN_CHUNK)
    tri = jnp.stack([same & (ri >= ci), same & (ri <= ci)]).astype(BF16)
    return pl.pallas_call(
        functools.partial(_delta_in_kernel, tm=tm, ck=512),
        grid=(b, s // tm),
        in_specs=[pl.BlockSpec((1, tm, d), row),
                  pl.BlockSpec((1, SEQ_HALO, d), lambda bb, i: (bb, jnp.maximum(i * hb - 1, 0), 0)),
                  pl.BlockSpec((1, SEQ_HALO, d), lambda bb, i: (bb, jnp.minimum((i + 1) * hb, last), 0)),
                  pl.BlockSpec((1, 1, d), vec), pl.BlockSpec((1, 1, d), vec),
                  pl.BlockSpec((1, d), full),
                  pl.BlockSpec((d, 3 * d), full), pl.BlockSpec((3, 3 * d), full),
                  pl.BlockSpec((d, d), full), pl.BlockSpec((d, 2 * LANES), full),
                  pl.BlockSpec((2, 2 * LANES), full), pl.BlockSpec((2, tm, tm), lambda bb, i: (0, 0, 0))],
        out_specs=[pl.BlockSpec((1, tm, d), row)] * 4 + [pl.BlockSpec((1, tm, 2 * LANES), row)],
        out_shape=outs,
        compiler_params=_params("parallel", "parallel"),
    )(x, x, x, sh, sc, gnorm, wqkv, conv_w, wz, wg, gp, tri)


def _scan_kernel(qf_ref, kf_ref, vf_ref, gf_ref, qb_ref, kb_ref, vb_ref, gb_ref, s0_ref, *out_refs, c, cps,
                 with_output):
    if with_output:
        of_ref, ob_ref, s_ref = out_refs
    else:
        (s_ref,) = out_refs
        of_ref = ob_ref = None
    step = pl.program_id(1)

    @pl.when(step == 0)
    def _():
        s_ref[...] = s0_ref[...]

    in_refs = ((qf_ref, kf_ref, vf_ref, gf_ref, of_ref), (qb_ref, kb_ref, vb_ref, gb_ref, ob_ref))
    heads = range(N_HEADS)
    rb = cps * c
    assert cps == 2 and rb == LANES
    ri = lax.broadcasted_iota(I32, (rb, rb), 0)
    ci = lax.broadcasted_iota(I32, (rb, rb), 1)
    same = (ri // c) == (ci // c)
    incl = (same & (ri >= ci), same & (ri <= ci))
    strict = (same & (ri > ci), same & (ri < ci))
    rp = lax.broadcasted_iota(I32, (c, rb), 0)
    cp = lax.broadcasted_iota(I32, (c, rb), 1)
    eye_pack = (cp % c == rp).astype(F32)
    left = cp < c

    def lanes(h):
        return slice(h * HEAD_DIM, (h + 1) * HEAD_DIM)

    def chunk_rows(dd, j):
        r0 = (j if dd == 0 else cps - 1 - j) * c
        return slice(r0, r0 + c)

    pairs = [(dd, h) for dd in range(2) for h in heads]
    gates = [in_refs[dd][3][0] for dd in range(2)]
    gates_t = [g.T for g in gates]
    q16 = {(dd, h): in_refs[dd][0][0, :, lanes(h)] for dd, h in pairs}
    k16 = {(dd, h): in_refs[dd][1][0, :, lanes(h)] for dd, h in pairs}
    qf = {key: q16[key].astype(F32) for key in pairs}
    kf = {key: k16[key].astype(F32) for key in pairs}
    vf = {(dd, h): in_refs[dd][2][0, :, lanes(h)].astype(F32) for dd, h in pairs}
    cum = 2 * N_HEADS
    beta = {(dd, h): gates[dd][:, h:h + 1] for dd, h in pairs}
    gcc = {(dd, h): gates[dd][:, cum + h:cum + h + 1] for dd, h in pairs}
    gcr = {(dd, h): gates_t[dd][cum + h:cum + h + 1, :] for dd, h in pairs}
    decay = {key: jnp.exp(jnp.where(incl[key[0]], gcc[key] - gcr[key], -jnp.inf)) for key in pairs}
    e = {key: jnp.exp(gcc[key]) for key in pairs}
    kb = {key: kf[key] * beta[key] for key in pairs}
    a = {key: _dot_nt(jnp.concatenate([kb[key].astype(BF16), q16[key]], axis=0), k16[key]) for key in pairs}
    lbd = {key: jnp.where(strict[key[0]], a[key][:rb] * decay[key], 0.0) for key in pairs}
    qk = {key: (a[key][rb:] * decay[key]).astype(BF16) for key in pairs}
    p = {key: eye_pack - (lbd[key][:c] + lbd[key][c:]) for key in pairs}
    l16 = {key: lbd[key].astype(BF16) for key in pairs}
    m = {key: _dot(l16[key], l16[key]) for key in pairs}
    span = 2
    while span < c:
        span *= 2
        m16 = {key: m[key].astype(BF16) for key in pairs}
        if span < c:
            pm = {key: _dot(jnp.concatenate([p[key], m[key]], axis=0).astype(BF16), m16[key]) for key in pairs}
            p = {key: p[key] + pm[key][:c] for key in pairs}
            m = {key: pm[key][c:] for key in pairs}
        else:
            p = {key: p[key] + _dot(p[key].astype(BF16), m16[key]) for key in pairs}
    t_bd = {key: jnp.concatenate([jnp.where(left, p[key], 0.0), jnp.where(left, 0.0, p[key])], axis=0).astype(BF16)
            for key in pairs}
    rhs = {key: jnp.concatenate([vf[key] * beta[key], kb[key] * e[key]], axis=1).astype(BF16) for key in pairs}
    sol = {key: _dot(t_bd[key], rhs[key]) for key in pairs}
    qe = {key: qf[key] * e[key] for key in pairs}

    state = {key: s_ref[0, key[0], key[1]] for key in pairs}
    zeros = jnp.zeros((c, HEAD_DIM), BF16)
    for j in range(cps):
        rs = {key: chunk_rows(key[0], j) for key in pairs}
        tot = {}
        for dd, h in pairs:
            r = rs[(dd, h)]
            edge = r.stop - 1 if dd == 0 else r.start
            tot[(dd, h)] = gates[dd][edge:edge + 1, cum + h:cum + h + 1]
        s16 = {key: state[key].astype(BF16) for key in pairs}
        ws = {key: _dot(jnp.concatenate([sol[key][rs[key], HEAD_DIM:], qe[key][rs[key]]], axis=0).astype(BF16),
                        s16[key]) for key in pairs}
        v16 = {key: (sol[key][rs[key], :HEAD_DIM] - ws[key][:c]).astype(BF16) for key in pairs}
        k_dec = {key: (kf[key][rs[key]] * jnp.exp(tot[key] - gcc[key][rs[key]])).astype(BF16) for key in pairs}
        for key in pairs:
            state[key] = state[key] * jnp.exp(tot[key]) + _dot_tn(k_dec[key], v16[key])
        if with_output:
            vpad = {key: jnp.concatenate([v16[key], zeros] if rs[key].start == 0 else [zeros, v16[key]], axis=0)
                    for key in pairs}
            o = {key: ws[key][c:] + _dot(qk[key][rs[key], :], vpad[key]) for key in pairs}
            for dd, h in pairs:
                in_refs[dd][4][0, rs[(dd, h)], lanes(h)] = o[(dd, h)].astype(of_ref.dtype)
    for dd, h in pairs:
        s_ref[0, dd, h] = state[(dd, h)]


def _delta_scan(q, k, v, gates, s0, c, cps, with_output):
    b, s, d = q.shape
    rb = c * cps
    nb = s // rb
    fwd = lambda bb, i: (bb, i, 0)
    bwd = lambda bb, i: (bb, nb - 1 - i, 0)
    state_spec = pl.BlockSpec((1, 2, N_HEADS, HEAD_DIM, HEAD_DIM), lambda bb, i: (bb, 0, 0, 0, 0))
    state_shape = jax.ShapeDtypeStruct((b, 2, N_HEADS, HEAD_DIM, HEAD_DIM), F32)
    out_specs, out_shape = [state_spec], [state_shape]
    if with_output:
        out_specs = [pl.BlockSpec((1, rb, d), fwd), pl.BlockSpec((1, rb, d), bwd)] + out_specs
        out_shape = [jax.ShapeDtypeStruct((b, s, d), BF16)] * 2 + out_shape
    return pl.pallas_call(
        functools.partial(_scan_kernel, c=c, cps=cps, with_output=with_output),
        grid=(b, nb),
        in_specs=([pl.BlockSpec((1, rb, d), fwd)] * 3 + [pl.BlockSpec((1, rb, LANES), fwd)]
                  + [pl.BlockSpec((1, rb, d), bwd)] * 3
                  + [pl.BlockSpec((1, rb, LANES), lambda bb, i: (bb, nb - 1 - i, 1)), state_spec]),
        out_specs=out_specs,
        out_shape=out_shape,
        compiler_params=_params("parallel", "arbitrary"),
    )(q, k, v, gates, q, k, v, gates, s0)


def _ffn_pre(x1, gf, shf, scf, wr_ref, x1_ref, hf_ref, rt_ref):
    x1_ref[...] = x1
    hf = _rms_mod(x1, gf, shf, scf)
    d = hf.shape[1]
    hf_ref[:, :d] = hf
    hi = hf.astype(BF16)
    lo = (hf - hi.astype(F32)).astype(BF16)
    a = _dot(hi, wr_ref[...])
    logits = a[:, :LANES] + a[:, LANES:] + _dot(lo, wr_ref[:, :LANES])
    lane = lax.broadcasted_iota(I32, logits.shape, 1)
    lanef = lane.astype(F32)
    big = float(LANES)
    gl = jnp.where(lane < N_GROUPS, logits, -jnp.inf)
    gmax = jnp.max(gl, axis=-1, keepdims=True)
    gi = jnp.min(jnp.where(gl == gmax, lanef, big), axis=-1, keepdims=True)
    pg_sel = 1.0 / jnp.sum(jnp.exp(gl - gmax), axis=-1, keepdims=True)
    rel = lanef - (N_GROUPS + EXPERTS_PER_GROUP * gi)
    el = jnp.where((rel >= 0.0) & (rel < float(EXPERTS_PER_GROUP)), logits, -jnp.inf)
    m1 = jnp.max(el, axis=-1, keepdims=True)
    i1 = jnp.min(jnp.where(el == m1, lanef, big), axis=-1, keepdims=True)
    el2 = jnp.where(lanef == i1, -jnp.inf, el)
    m2 = jnp.max(el2, axis=-1, keepdims=True)
    i2 = jnp.min(jnp.where(el2 == m2, lanef, big), axis=-1, keepdims=True)
    r2 = jnp.exp(m2 - m1)
    w1 = pg_sel / (1.0 + r2)
    w2 = pg_sel * r2 / (1.0 + r2)
    first_low = i1 < i2
    a_loc = jnp.minimum(i1, i2) - (N_GROUPS + EXPERTS_PER_GROUP * gi)
    b_loc = jnp.maximum(i1, i2) - (N_GROUPS + EXPERTS_PER_GROUP * gi)
    pair = a_loc * (EXPERTS_PER_GROUP - 1) - a_loc * (a_loc - 1.0) * 0.5 + (b_loc - a_loc - 1.0)
    cls = gi * float(PAIRS_PER_GROUP) + pair
    rt = jnp.where(lane == 0, jnp.where(first_low, w1, w2), jnp.where(
        lane == 1, jnp.where(first_low, w2, w1), jnp.where(lane == 2, cls, 0.0)))
    rt_ref[...] = rt
    hf_ref[:, d:] = rt


def _delta_out_kernel(of_ref, ob_ref, z_ref, x_ref, gt_ref, on_ref, wo_ref, gf_ref, shf_ref, scf_ref, wr_ref,
                      x1_ref, hf_ref, rt_ref):
    o = of_ref[0].astype(F32) + ob_ref[0].astype(F32)
    z = z_ref[0].astype(F32)
    onorm = on_ref[...]
    parts = []
    for h in range(N_HEADS):
        oh = o[:, h * HEAD_DIM:(h + 1) * HEAD_DIM]
        parts.append(oh * lax.rsqrt(jnp.mean(oh * oh, axis=-1, keepdims=True) + EPS) * onorm)
    y = jnp.concatenate(parts, axis=1) * _silu(z)
    x1 = x_ref[0] + gt_ref[0] * _dot(y.astype(BF16), wo_ref[...])
    _ffn_pre(x1, gf_ref[...], shf_ref[0], scf_ref[0], wr_ref, x1_ref, hf_ref, rt_ref)


def _delta_out(o_f, o_b, z, x, gt, onorm, wo, gf, shf, scf, wr, tm):
    b, s, d = x.shape
    t = b * s
    nt = s // tm
    row = lambda bb, i: (bb, i, 0)
    vec = lambda bb, i: (bb, 0, 0)
    full = lambda bb, i: (0, 0)
    flat = lambda bb, i: (bb * nt + i, 0)
    return pl.pallas_call(
        _delta_out_kernel,
        grid=(b, nt),
        in_specs=[pl.BlockSpec((1, tm, d), row), pl.BlockSpec((1, tm, d), row),
                  pl.BlockSpec((1, tm, d), row), pl.BlockSpec((1, tm, d), row),
                  pl.BlockSpec((1, 1, d), vec), pl.BlockSpec((1, HEAD_DIM), full),
                  pl.BlockSpec((d, d), full), pl.BlockSpec((1, d), full),
                  pl.BlockSpec((1, 1, d), vec), pl.BlockSpec((1, 1, d), vec),
                  pl.BlockSpec((d, 2 * LANES), full)],
        out_specs=[pl.BlockSpec((tm, d), flat), pl.BlockSpec((tm, d + LANES), flat), pl.BlockSpec((tm, LANES), flat)],
        out_shape=[jax.ShapeDtypeStruct((t, d), F32), jax.ShapeDtypeStruct((t, d + LANES), F32),
                   jax.ShapeDtypeStruct((t, LANES), F32)],
        compiler_params=_params("parallel", "parallel"),
    )(o_f, o_b, z, x, gt, onorm, wo, gf, shf, scf, wr)


def _sconv_kernel(xm_ref, xp_ref, xn_ref, sh_ref, sc_ref, g_ref, win_ref, cw_ref, wo_ref, gt_ref,
                  gf_ref, shf_ref, scf_ref, wr_ref, x1_ref, hf_ref, rt_ref, *, tm, ck):
    i = pl.program_id(1)
    n = pl.num_programs(1)
    g, sh, sc = g_ref[...], sh_ref[0], sc_ref[0]
    xm = xm_ref[0]
    hm = _rms_mod(xm, g, sh, sc)
    hp = _rms_mod(xp_ref[0], g, sh, sc) * (i > 0).astype(F32)
    hn = _rms_mod(xn_ref[0], g, sh, sc) * (i < n - 1).astype(F32)
    hm16 = hm.astype(BF16)
    hext = jnp.concatenate([hp, hm, hn], axis=0).astype(BF16)
    d = xm.shape[1]
    col = lax.broadcasted_iota(I32, (tm, 1), 0) % GRID_W
    acc = jnp.zeros((tm, d), F32)
    for j in range(d // ck):
        c0 = j * ck
        gate_b = _dot(hm16, win_ref[:, c0:c0 + ck])
        cw = cw_ref[:, c0:c0 + ck]
        if c0 < d // 2:
            u = _dot(hm16, win_ref[:, d + c0:d + c0 + ck]) * _dot(hm16, win_ref[:, 2 * d + c0:2 * d + c0 + ck])
            left = jnp.where(col == 0, 0.0, pltpu.roll(u, 1, 0))
            right = jnp.where(col == GRID_W - 1, 0.0, pltpu.roll(u, tm - 1, 0))
            y = cw[0:1] * left + cw[1:2] * u + cw[2:3] * right
        else:
            u = _dot(hext, win_ref[:, d + c0:d + c0 + ck]) * _dot(hext, win_ref[:, 2 * d + c0:2 * d + c0 + ck])
            y = cw[0:1] * u[0:tm] + cw[1:2] * u[GRID_W:GRID_W + tm] + cw[2:3] * u[2 * GRID_W:2 * GRID_W + tm]
        acc = acc + _dot((gate_b * y).astype(BF16), wo_ref[c0:c0 + ck, :])
    x1 = xm + gt_ref[0] * acc
    _ffn_pre(x1, gf_ref[...], shf_ref[0], scf_ref[0], wr_ref, x1_ref, hf_ref, rt_ref)


def _sconv(x, sh, sc, gnorm, win, conv_w, wo, gt, gf, shf, scf, wr, tm):
    b, s, d = x.shape
    t = b * s
    nt = s // tm
    hb = tm // GRID_W
    last = s // GRID_W - 1
    row = lambda bb, i: (bb, i, 0)
    vec = lambda bb, i: (bb, 0, 0)
    full = lambda bb, i: (0, 0)
    flat = lambda bb, i: (bb * nt + i, 0)
    return pl.pallas_call(
        functools.partial(_sconv_kernel, tm=tm, ck=512),
        grid=(b, nt),
        in_specs=[pl.BlockSpec((1, tm, d), row),
                  pl.BlockSpec((1, GRID_W, d), lambda bb, i: (bb, jnp.maximum(i * hb - 1, 0), 0)),
                  pl.BlockSpec((1, GRID_W, d), lambda bb, i: (bb, jnp.minimum((i + 1) * hb, last), 0)),
                  pl.BlockSpec((1, 1, d), vec), pl.BlockSpec((1, 1, d), vec), pl.BlockSpec((1, d), full),
                  pl.BlockSpec((d, 3 * d), full), pl.BlockSpec((3, d), full), pl.BlockSpec((d, d), full),
                  pl.BlockSpec((1, 1, d), vec), pl.BlockSpec((1, d), full),
                  pl.BlockSpec((1, 1, d), vec), pl.BlockSpec((1, 1, d), vec),
                  pl.BlockSpec((d, 2 * LANES), full)],
        out_specs=[pl.BlockSpec((tm, d), flat), pl.BlockSpec((tm, d + LANES), flat), pl.BlockSpec((tm, LANES), flat)],
        out_shape=[jax.ShapeDtypeStruct((t, d), F32), jax.ShapeDtypeStruct((t, d + LANES), F32),
                   jax.ShapeDtypeStruct((t, LANES), F32)],
        compiler_params=_params("parallel", "parallel"),
    )(x, x, x, sh, sc, gnorm, win, conv_w, wo, gt, gf, shf, scf, wr)


def _rank_kernel(c_ref, dest_ref, ends_ref, cnt_ref, start_ref, carry_ref, *, tm, bm):
    p = pl.program_id(0)
    i = pl.program_id(1)
    onehot = lax.broadcasted_iota(I32, (LANES, tm), 0) == c_ref[0:1, :]
    ohf = onehot.astype(F32)
    per_class = jnp.broadcast_to(jnp.sum(ohf, axis=1, keepdims=True), (LANES, LANES))

    @pl.when(jnp.logical_and(p == 0, i == 0))
    def _():
        cnt_ref[...] = jnp.zeros_like(cnt_ref)

    @pl.when(p == 0)
    def _():
        cnt_ref[...] += per_class

    @pl.when(jnp.logical_and(p == 1, i == 0))
    def _():
        padded = ((cnt_ref[...].astype(I32) + (bm - 1)) & (-bm)).astype(F32)
        before = (lax.broadcasted_iota(I32, (LANES, LANES), 0) > lax.broadcasted_iota(I32, (LANES, LANES), 1))
        start = _dot(before.astype(F32), padded, HIGHEST)
        start_ref[...] = start
        ends_ref[...] = (start + padded).astype(I32)
        carry_ref[...] = jnp.zeros_like(carry_ref)

    @pl.when(p == 1)
    def _():
        earlier = (lax.broadcasted_iota(I32, (tm, tm), 0) < lax.broadcasted_iota(I32, (tm, tm), 1)).astype(BF16)
        slot = _dot(onehot.astype(BF16), earlier) + carry_ref[:, 0:1] + start_ref[:, 0:1]
        dest = jnp.sum(ohf * slot, axis=0, keepdims=True)
        dest_ref[...] = jnp.concatenate([dest, jnp.zeros((7, tm), F32)], axis=0).astype(I32)
        carry_ref[...] += per_class


def _rank(cls8, tm, bm):
    t = cls8.shape[1]
    return pl.pallas_call(
        functools.partial(_rank_kernel, tm=tm, bm=bm),
        grid=(2, t // tm),
        in_specs=[pl.BlockSpec((8, tm), lambda p, i: (0, i))],
        out_specs=[pl.BlockSpec((8, tm), lambda p, i: (0, i * p)), pl.BlockSpec((LANES, LANES), lambda p, i: (0, 0))],
        out_shape=[jax.ShapeDtypeStruct((8, t), I32), jax.ShapeDtypeStruct((LANES, LANES), I32)],
        scratch_shapes=[pltpu.VMEM((LANES, LANES), F32)] * 3,
        compiler_params=_params("arbitrary", "arbitrary"),
    )(cls8)


def _row_copy(src_ref, src_row, dst_ref, dst_row, sem):
    return pltpu.make_async_copy(src_ref.at[pl.ds(src_row, 1), :], dst_ref.at[pl.ds(dst_row, 1), :], sem)


def _dispatch_kernel(dest_ref, hf_ref, xb_in_ref, xb_ref, sem, *, td):
    del xb_in_ref

    def issue(r, carry):
        _row_copy(hf_ref, r, xb_ref, dest_ref[0, 0, r], sem).start()
        return carry

    for r in range(td):
        issue(r, 0)

    def drain(r, carry):
        _row_copy(hf_ref, 0, xb_ref, 0, sem).wait()
        return carry

    lax.fori_loop(0, td, drain, 0, unroll=8)


def _dispatch(dest3, hf, xb0, td):
    t, d = hf.shape
    cap = xb0.shape[0]
    return pl.pallas_call(
        functools.partial(_dispatch_kernel, td=td),
        grid=(t // td,),
        in_specs=[pl.BlockSpec((1, 1, td), lambda i: (i, 0, 0), memory_space=pltpu.SMEM),
                  pl.BlockSpec((td, d), lambda i: (i, 0)),
                  pl.BlockSpec(memory_space=pl.ANY)],
        out_specs=pl.BlockSpec(memory_space=pl.ANY),
        out_shape=jax.ShapeDtypeStruct((cap, d), hf.dtype),
        scratch_shapes=[pltpu.SemaphoreType.DMA],
        input_output_aliases={2: 0},
        compiler_params=_params("arbitrary"),
    )(dest3, hf, xb0)


def _expert_kernel(grp_ref, la_ref, lb_ref, bs_ref, nu_ref, x_ref, w1_hbm, w3_hbm, w2_hbm, y_ref,
                   res1, res3, res2, stage_up, stage_down, sems, *, layer):
    del bs_ref
    i = pl.program_id(0)
    d = y_ref.shape[1]
    grp = grp_ref[i]

    @pl.when(jnp.logical_or(i == 0, grp != grp_ref[jnp.maximum(i - 1, 0)]))
    def _():
        plan = []
        for e in range(EXPERTS_PER_GROUP):
            plan += [(w1_hbm, res1, stage_up.at[0], 0, e), (w3_hbm, res3, stage_up.at[1], 1, e),
                     (w2_hbm, res2, stage_down, 2, e)]

        def copy(step):
            src, _, stage, sem, e = plan[step]
            return pltpu.make_async_copy(src.at[layer, grp * EXPERTS_PER_GROUP + e], stage, sems.at[sem])

        copy(0).start()
        for step, (_, dst, stage, _, e) in enumerate(plan):
            if step + 1 < len(plan):
                copy(step + 1).start()
            copy(step).wait()
            dst[e] = stage[...].astype(BF16)

    @pl.when(i < nu_ref[0])
    def _():
        x = x_ref[:, :d].astype(BF16)

        def swiglu(e):
            h = _silu(_dot(x, res1[e])) * _dot(x, res3[e])
            return _dot(h.astype(BF16), res2[e])

        y_ref[...] = x_ref[:, d:d + 1] * swiglu(la_ref[i]) + x_ref[:, d + 1:d + 2] * swiglu(lb_ref[i])

    @pl.when(i >= nu_ref[0])
    def _():
        y_ref[...] = jnp.zeros_like(y_ref)


def _experts(block_grp, block_la, block_lb, block_src, n_used, xb, w1, w3, w2, layer, bm):
    cap, dx = xb.shape
    _, _, d, de = w1.shape
    grid_spec = pltpu.PrefetchScalarGridSpec(
        num_scalar_prefetch=5,
        grid=(cap // bm,),
        in_specs=[pl.BlockSpec((bm, dx), lambda i, g, la, lb, bs, nu: (bs[i], 0)),
                  pl.BlockSpec(memory_space=pl.ANY), pl.BlockSpec(memory_space=pl.ANY),
                  pl.BlockSpec(memory_space=pl.ANY)],
        out_specs=pl.BlockSpec((bm, d), lambda i, g, la, lb, bs, nu: (i, 0)),
        scratch_shapes=[pltpu.VMEM((EXPERTS_PER_GROUP, d, de), BF16), pltpu.VMEM((EXPERTS_PER_GROUP, d, de), BF16),
                        pltpu.VMEM((EXPERTS_PER_GROUP, de, d), BF16), pltpu.VMEM((2, d, de), F32),
                        pltpu.VMEM((de, d), F32), pltpu.SemaphoreType.DMA((3,))],
    )
    return pl.pallas_call(
        functools.partial(_expert_kernel, layer=layer),
        grid_spec=grid_spec,
        out_shape=jax.ShapeDtypeStruct((cap, d), F32),
        compiler_params=_params("arbitrary"),
    )(block_grp, block_la, block_lb, block_src, n_used, xb, w1, w3, w2)


def _combine_kernel(dest_ref, x1_ref, gt_ref, fn_ref, yb_ref, o_ref, buf, sems, *, tc, final):
    i = pl.program_id(0)
    n_tiles = pl.num_programs(0) - 1

    @pl.when(i < n_tiles)
    def _():
        slot = i % 2
        for r in range(tc):
            _row_copy(yb_ref, dest_ref[0, 0, r], buf.at[slot], r, sems.at[slot]).start()

    @pl.when(i > 0)
    def _():
        slot = (i - 1) % 2

        def drain(r, carry):
            _row_copy(yb_ref, 0, buf.at[slot], 0, sems.at[slot]).wait()
            return carry

        lax.fori_loop(0, tc, drain, 0, unroll=8)
        x2 = x1_ref[...] + gt_ref[0] * buf[slot]
        if final:
            ms = jnp.mean(x2 * x2, axis=-1, keepdims=True)
            x2 = x2 * lax.rsqrt(ms + EPS) * fn_ref[...]
        o_ref[...] = x2


def _combine(dest3, x1, gt, fnorm, yb, s, tc, final):
    t, d = x1.shape
    per_b = s // tc
    n_tiles = t // tc
    done = lambda i: jnp.maximum(i - 1, 0)
    return pl.pallas_call(
        functools.partial(_combine_kernel, tc=tc, final=final),
        grid=(n_tiles + 1,),
        in_specs=[pl.BlockSpec((1, 1, tc), lambda i: (jnp.minimum(i, n_tiles - 1), 0, 0), memory_space=pltpu.SMEM),
                  pl.BlockSpec((tc, d), lambda i: (done(i), 0)),
                  pl.BlockSpec((1, 1, d), lambda i: (done(i) // per_b, 0, 0)),
                  pl.BlockSpec((1, d), lambda i: (0, 0)),
                  pl.BlockSpec(memory_space=pl.ANY)],
        out_specs=pl.BlockSpec((tc, d), lambda i: (done(i), 0)),
        out_shape=jax.ShapeDtypeStruct((t, d), F32),
        scratch_shapes=[pltpu.VMEM((2, tc, d), F32), pltpu.SemaphoreType.DMA((2,))],
        compiler_params=_params("arbitrary"),
    )(dest3, x1, gt, fnorm, yb)


def _moe(hfx, rt, x1, gt, fnorm, w1, w3, w2, layer, xb0, s, final, bm, tr=512, td=1024):
    t = hfx.shape[0]
    cls8 = jnp.zeros((8, t), I32).at[0].set(rt[:, 2].astype(I32))
    dest8, ends2 = _rank(cls8, tr, bm)
    dest3 = dest8[0].reshape(t // td, 1, td)
    ends = ends2[:N_CLASSES, 0]
    n_blocks = xb0.shape[0] // bm
    n_used = ends[-1] // bm
    block_src = jnp.maximum(jnp.minimum(jnp.arange(n_blocks, dtype=I32), n_used - 1), 0)
    block_cls = jnp.minimum(jnp.sum((ends[None, :] <= (block_src * bm)[:, None]).astype(I32), axis=1), N_CLASSES - 1)
    block_la = jnp.asarray(CLASS_LOCAL_A, I32)[block_cls]
    block_lb = jnp.asarray(CLASS_LOCAL_B, I32)[block_cls]
    xb = _dispatch(dest3, hfx, xb0, td)
    yb = _experts(block_cls // PAIRS_PER_GROUP, block_la, block_lb, block_src, n_used.reshape(1), xb, w1, w3, w2,
                  layer, bm)
    return _combine(dest3, x1, gt, fnorm, yb, s, td, final), xb


def _router_weights(router_g, router_e):
    d = router_g.shape[0]
    w = jnp.zeros((d, LANES), F32).at[:, :N_GROUPS].set(router_g).at[:, N_GROUPS:N_GROUPS + N_EXPERTS].set(router_e)
    hi = w.astype(BF16)
    lo = (w - hi.astype(F32)).astype(BF16)
    return jnp.concatenate([hi, lo], axis=1)


def _gate_weights(w_in, a_log, dt_bias):
    d = w_in.shape[0]
    base = 4 * d
    wg = jnp.zeros((d, 2 * LANES), F32)
    gp = jnp.zeros((2, 2 * LANES), F32)
    for direction in range(2):
        o = direction * LANES
        wg = wg.at[:, o:o + N_HEADS].set(w_in[:, base + direction * N_HEADS:base + (direction + 1) * N_HEADS])
        wg = wg.at[:, o + N_HEADS:o + 2 * N_HEADS].set(
            w_in[:, base + (2 + direction) * N_HEADS:base + (3 + direction) * N_HEADS])
        wg = wg.at[:, o + 2 * N_HEADS:o + 3 * N_HEADS].set(
            w_in[:, base + (2 + direction) * N_HEADS:base + (3 + direction) * N_HEADS])
        for rep in (1, 2):
            gp = gp.at[0, o + rep * N_HEADS:o + (rep + 1) * N_HEADS].set(a_log[direction])
            gp = gp.at[1, o + rep * N_HEADS:o + (rep + 1) * N_HEADS].set(dt_bias[direction])
    return wg.astype(BF16), gp


def kernel(x, c, ctx, c_ctx, ada_w, ada_b, norm_mix, norm_ffn, w_in_a, conv_a, a_log_a, dt_bias_a, onorm_a, w_out_a,
           w_in_b, conv_b, w_out_b, router_g, router_e, w1, w3, w2, final_norm):
    b, s, d = x.shape
    chunk = SCAN_CHUNK
    tm = min(512, s)

    cond = jnp.zeros((16, d), F32).at[:b].set(c).at[b].set(c_ctx)
    mod = _ada_params(cond, ada_w, ada_b)

    def mods(layer, rows):
        m = mod[layer, rows]
        return [jnp.broadcast_to(m[:, None, k * d:(k + 1) * d], (b, 1, d)) for k in range(N_MOD)]

    sh_m, sc_m, gt_m, sh_f, sc_f, gt_f = mods(0, slice(0, b))
    csh_m, csc_m = mods(0, slice(b, b + 1))[:2]
    w_in = w_in_a[0]
    wqkv, wz = w_in[:, :3 * d].astype(BF16), w_in[:, 3 * d:4 * d].astype(BF16)
    wg, gp = _gate_weights(w_in, a_log_a[0], dt_bias_a[0])
    gn = norm_mix[0].reshape(1, d)
    qc, kc, vc, _, gc = _delta_in(ctx, csh_m, csc_m, gn, wqkv, conv_a[0], wz, wg, gp, tm)
    s_zero = jnp.zeros((b, 2, N_HEADS, HEAD_DIM, HEAD_DIM), F32)
    (s_ctx,) = _delta_scan(qc, kc, vc, gc, s_zero, chunk, SCAN_CHUNKS_PER_STEP, False)
    ql, kl, vl, zl, gl = _delta_in(x, sh_m, sc_m, gn, wqkv, conv_a[0], wz, wg, gp, tm)
    o_f, o_b, _ = _delta_scan(ql, kl, vl, gl, s_ctx, chunk, SCAN_CHUNKS_PER_STEP, True)
    wr = _router_weights(router_g[0], router_e[0])
    x1, hf, rt = _delta_out(o_f, o_b, zl, x, gt_m, onorm_a[0].reshape(1, HEAD_DIM), w_out_a[0].astype(BF16),
                            norm_ffn[0].reshape(1, d), sh_f, sc_f, wr, tm)
    fn = final_norm.reshape(1, d)
    xb0 = jnp.zeros(((-(-(b * s) // MOE_BLOCK) + N_CLASSES) * MOE_BLOCK, d + LANES), F32)
    x2, xb0 = _moe(hf, rt, x1, gt_f, fn, w1, w3, w2, 0, xb0, s, False, MOE_BLOCK)
    x2 = x2.reshape(b, s, d)

    sh_m, sc_m, gt_m, sh_f, sc_f, gt_f = mods(1, slice(0, b))
    wr = _router_weights(router_g[1], router_e[1])
    x1, hf, rt = _sconv(x2, sh_m, sc_m, norm_mix[1].reshape(1, d), w_in_b[0].astype(BF16), conv_b[0],
                        w_out_b[0].astype(BF16), gt_m, norm_ffn[1].reshape(1, d), sh_f, sc_f, wr, tm)
    out, _ = _moe(hf, rt, x1, gt_f, fn, w1, w3, w2, 1, xb0, s, True, MOE_BLOCK)
    return out.reshape(b, s, d)
```

```python
import functools

import jax
import jax.numpy as jnp
from jax import lax
from jax.experimental import pallas as pl
from jax.experimental.pallas import tpu as pltpu

F32 = jnp.float32
BF16 = jnp.bfloat16
I32 = jnp.int32
HIGHEST = lax.Precision.HIGHEST

EPS = 1e-6
N_HEADS = 8
HEAD_DIM = 128
GRID_W = 64
N_GROUPS = 4
EXPERTS_PER_GROUP = 8
N_EXPERTS = N_GROUPS * EXPERTS_PER_GROUP
PAIRS_PER_GROUP = EXPERTS_PER_GROUP * (EXPERTS_PER_GROUP - 1) // 2
N_CLASSES = N_GROUPS * PAIRS_PER_GROUP
_PAIRS = [(a, b) for a in range(EXPERTS_PER_GROUP) for b in range(a + 1, EXPERTS_PER_GROUP)]
CLASS_LOCAL_A = [a for _ in range(N_GROUPS) for a, _ in _PAIRS]
CLASS_LOCAL_B = [b for _ in range(N_GROUPS) for _, b in _PAIRS]
MOE_BLOCK = 256
N_MOD = 6
LANES = 128
SCAN_CHUNK = 64
SCAN_CHUNKS_PER_STEP = 4
SEQ_HALO = 8
VMEM_LIMIT = 48 * 1024 * 1024


def _params(*sem):
    return pltpu.CompilerParams(dimension_semantics=sem, vmem_limit_bytes=VMEM_LIMIT)


def _dot(a, b, precision=None):
    return jnp.dot(a, b, preferred_element_type=F32, precision=precision)


def _dot_nt(a, b, precision=None):
    return lax.dot_general(a, b, (((1,), (1,)), ((), ())), preferred_element_type=F32, precision=precision)


def _dot_tn(a, b, precision=None):
    return lax.dot_general(a, b, (((0,), (0,)), ((), ())), preferred_element_type=F32, precision=precision)


def _silu(x):
    return x * jax.nn.sigmoid(x)


def _rms_mod(x, g, shift, scale):
    ms = jnp.mean(x * x, axis=-1, keepdims=True)
    return x * lax.rsqrt(ms + EPS) * g * (1.0 + scale) + shift


def _ada_kernel(c_ref, w_ref, b_ref, o_ref):
    o_ref[0] = _dot(_silu(c_ref[...]), w_ref[0], HIGHEST) + b_ref[0]


def _ada_params(cond, ada_w, ada_b):
    depth, d, n = ada_w.shape
    tn = 512
    r = cond.shape[0]
    return pl.pallas_call(
        _ada_kernel,
        grid=(depth, n // tn),
        in_specs=[pl.BlockSpec((r, d), lambda l, j: (0, 0)),
                  pl.BlockSpec((1, d, tn), lambda l, j: (l, 0, j)),
                  pl.BlockSpec((1, 1, tn), lambda l, j: (l, 0, j))],
        out_specs=pl.BlockSpec((1, r, tn), lambda l, j: (l, 0, j)),
        out_shape=jax.ShapeDtypeStruct((depth, r, n), F32),
        compiler_params=_params("parallel", "parallel"),
    )(cond, ada_w, ada_b.reshape(depth, 1, n))


def _delta_in_kernel(xm_ref, xp_ref, xn_ref, sh_ref, sc_ref, g_ref, wqkv_ref, cw_ref, wz_ref, wg_ref, gp_ref, tri_ref,
                     q_ref, k_ref, v_ref, z_ref, gate_ref, *, tm, ck):
    i = pl.program_id(1)
    n = pl.num_programs(1)
    g, sh, sc = g_ref[...], sh_ref[0], sc_ref[0]
    hm = _rms_mod(xm_ref[0], g, sh, sc)
    hp = _rms_mod(xp_ref[0], g, sh, sc) * (i > 0).astype(F32)
    hn = _rms_mod(xn_ref[0], g, sh, sc) * (i < n - 1).astype(F32)
    hm16 = hm.astype(BF16)
    hext = jnp.concatenate([hp, hm, hn], axis=0).astype(BF16)
    rows = tm + 2 * SEQ_HALO
    d = hm.shape[1]
    outs = (q_ref, k_ref, v_ref)
    for j in range(3 * d // ck):
        p = _dot(hext, wqkv_ref[:, j * ck:(j + 1) * ck])
        cw = cw_ref[:, j * ck:(j + 1) * ck]
        y = cw[0:1] * pltpu.roll(p, 1, 0) + cw[1:2] * p + cw[2:3] * pltpu.roll(p, rows - 1, 0)
        y = _silu(y[SEQ_HALO:SEQ_HALO + tm])
        which, col0 = (j * ck) // d, (j * ck) % d
        for hh in range(ck // HEAD_DIM):
            yh = y[:, hh * HEAD_DIM:(hh + 1) * HEAD_DIM]
            if which < 2:
                inv = lax.rsqrt(jnp.sum(yh * yh, axis=-1, keepdims=True) + EPS)
                yh = yh * (inv * HEAD_DIM ** -0.5 if which == 0 else inv)
            c0 = col0 + hh * HEAD_DIM
            outs[which][0, :, c0:c0 + HEAD_DIM] = yh.astype(q_ref.dtype)
    z_ref[0] = _dot(hm16, wz_ref[...]).astype(z_ref.dtype)
    pg = _dot(hm16, wg_ref[...])
    lane = lax.broadcasted_iota(I32, pg.shape, 1) % LANES
    xg = pg + gp_ref[1:2]
    softplus = jnp.maximum(xg, 0.0) + jnp.log1p(jnp.exp(-jnp.abs(xg)))
    gval = -jnp.exp(gp_ref[0:1]) * softplus
    g_hi = gval.astype(BF16)
    r1 = gval - g_hi.astype(F32)
    g_mid = r1.astype(BF16)
    g_lo = (r1 - g_mid.astype(F32)).astype(BF16)
    cums = []
    for direction in range(2):
        sl = slice(direction * LANES, (direction + 1) * LANES)
        terms = jnp.concatenate([g_hi[:, sl], g_mid[:, sl], g_lo[:, sl]], axis=1)
        cs = _dot(tri_ref[direction], terms)
        cums.append(cs[:, :LANES] + cs[:, LANES:2 * LANES] + cs[:, 2 * LANES:])
    gcum = jnp.concatenate(cums, axis=1)
    gate_ref[0] = jnp.where(lane < N_HEADS, jax.nn.sigmoid(pg), jnp.where(
        lane < 2 * N_HEADS, gval, jnp.where(lane < 3 * N_HEADS, gcum, 0.0)))


def _delta_in(x, sh, sc, gnorm, wqkv, conv_w, wz, wg, gp, tm):
    b, s, d = x.shape
    tm = min(tm, s)
    hb = tm // SEQ_HALO
    last = s // SEQ_HALO - 1
    row = lambda bb, i: (bb, i, 0)
    vec = lambda bb, i: (bb, 0, 0)
    full = lambda bb, i: (0, 0)
    outs = [jax.ShapeDtypeStruct((b, s, d), BF16)] * 4 + [jax.ShapeDtypeStruct((b, s, 2 * LANES), F32)]
    ri = lax.broadcasted_iota(I32, (tm, tm), 0)
    ci = lax.broadcasted_iota(I32, (tm, tm), 1)
    same = (ri // SCAN_CHUNK) == (ci // SCAN_CHUNK)
    tri = jnp.stack([same & (ri >= ci), same & (ri <= ci)]).astype(BF16)
    return pl.pallas_call(
        functools.partial(_delta_in_kernel, tm=tm, ck=512),
        grid=(b, s // tm),
        in_specs=[pl.BlockSpec((1, tm, d), row),
                  pl.BlockSpec((1, SEQ_HALO, d), lambda bb, i: (bb, jnp.maximum(i * hb - 1, 0), 0)),
                  pl.BlockSpec((1, SEQ_HALO, d), lambda bb, i: (bb, jnp.minimum((i + 1) * hb, last), 0)),
                  pl.BlockSpec((1, 1, d), vec), pl.BlockSpec((1, 1, d), vec),
                  pl.BlockSpec((1, d), full),
                  pl.BlockSpec((d, 3 * d), full), pl.BlockSpec((3, 3 * d), full),
                  pl.BlockSpec((d, d), full), pl.BlockSpec((d, 2 * LANES), full),
                  pl.BlockSpec((2, 2 * LANES), full), pl.BlockSpec((2, tm, tm), lambda bb, i: (0, 0, 0))],
        out_specs=[pl.BlockSpec((1, tm, d), row)] * 4 + [pl.BlockSpec((1, tm, 2 * LANES), row)],
        out_shape=outs,
        compiler_params=_params("parallel", "parallel"),
    )(x, x, x, sh, sc, gnorm, wqkv, conv_w, wz, wg, gp, tri)


def _scan_kernel(qf_ref, kf_ref, vf_ref, gf_ref, qb_ref, kb_ref, vb_ref, gb_ref, s0_ref, *out_refs, c, cps,
                 with_output):
    if with_output:
        of_ref, ob_ref, s_ref = out_refs
    else:
        (s_ref,) = out_refs
        of_ref = ob_ref = None
    step = pl.program_id(1)

    @pl.when(step == 0)
    def _():
        s_ref[...] = s0_ref[...]

    in_refs = ((qf_ref, kf_ref, vf_ref, gf_ref, of_ref), (qb_ref, kb_ref, vb_ref, gb_ref, ob_ref))
    heads = range(N_HEADS)
    rb = 2 * c
    assert cps % 2 == 0 and rb == LANES
    n_pairs = cps // 2
    ri = lax.broadcasted_iota(I32, (rb, rb), 0)
    ci = lax.broadcasted_iota(I32, (rb, rb), 1)
    same = (ri // c) == (ci // c)
    incl = (same & (ri >= ci), same & (ri <= ci))
    strict = (same & (ri > ci), same & (ri < ci))
    rp = lax.broadcasted_iota(I32, (c, rb), 0)
    cp = lax.broadcasted_iota(I32, (c, rb), 1)
    eye_pack = (cp % c == rp).astype(F32)
    left = cp < c

    def lanes(h):
        return slice(h * HEAD_DIM, (h + 1) * HEAD_DIM)

    def pair_rows(dd, pp):
        r0 = (pp if dd == 0 else n_pairs - 1 - pp) * rb
        return slice(r0, r0 + rb)

    def chunk_rows(dd, j):
        r0 = (j if dd == 0 else 1 - j) * c
        return slice(r0, r0 + c)

    pairs = [(dd, pp, h) for dd in range(2) for pp in range(n_pairs) for h in heads]
    gates = {(dd, pp): in_refs[dd][3][0, pair_rows(dd, pp), :]
             for dd in range(2) for pp in range(n_pairs)}
    gates_t = {key: g.T for key, g in gates.items()}
    q16 = {(dd, pp, h): in_refs[dd][0][0, pair_rows(dd, pp), lanes(h)] for dd, pp, h in pairs}
    k16 = {(dd, pp, h): in_refs[dd][1][0, pair_rows(dd, pp), lanes(h)] for dd, pp, h in pairs}
    qf = {key: q16[key].astype(F32) for key in pairs}
    kf = {key: k16[key].astype(F32) for key in pairs}
    vf = {(dd, pp, h): in_refs[dd][2][0, pair_rows(dd, pp), lanes(h)].astype(F32) for dd, pp, h in pairs}
    cum = 2 * N_HEADS
    beta = {(dd, pp, h): gates[(dd, pp)][:, h:h + 1] for dd, pp, h in pairs}
    gcc = {(dd, pp, h): gates[(dd, pp)][:, cum + h:cum + h + 1] for dd, pp, h in pairs}
    gcr = {(dd, pp, h): gates_t[(dd, pp)][cum + h:cum + h + 1, :] for dd, pp, h in pairs}
    decay = {key: jnp.exp(jnp.where(incl[key[0]], gcc[key] - gcr[key], -jnp.inf)) for key in pairs}
    e = {key: jnp.exp(gcc[key]) for key in pairs}
    kb = {key: kf[key] * beta[key] for key in pairs}
    a = {key: _dot_nt(jnp.concatenate([kb[key].astype(BF16), q16[key]], axis=0), k16[key]) for key in pairs}
    lbd = {key: jnp.where(strict[key[0]], a[key][:rb] * decay[key], 0.0) for key in pairs}
    qk = {key: (a[key][rb:] * decay[key]).astype(BF16) for key in pairs}
    p = {key: eye_pack - (lbd[key][:c] + lbd[key][c:]) for key in pairs}
    l16 = {key: lbd[key].astype(BF16) for key in pairs}
    m = {key: _dot(l16[key], l16[key]) for key in pairs}
    span = 2
    while span < c:
        span *= 2
        m16 = {key: m[key].astype(BF16) for key in pairs}
        if span < c:
            pm = {key: _dot(jnp.concatenate([p[key], m[key]], axis=0).astype(BF16), m16[key]) for key in pairs}
            p = {key: p[key] + pm[key][:c] for key in pairs}
            m = {key: pm[key][c:] for key in pairs}
        else:
            p = {key: p[key] + _dot(p[key].astype(BF16), m16[key]) for key in pairs}
    t_bd = {key: jnp.concatenate([jnp.where(left, p[key], 0.0), jnp.where(left, 0.0, p[key])], axis=0).astype(BF16)
            for key in pairs}
    rhs = {key: jnp.concatenate([vf[key] * beta[key], kb[key] * e[key]], axis=1).astype(BF16) for key in pairs}
    sol = {key: _dot(t_bd[key], rhs[key]) for key in pairs}
    qe = {key: qf[key] * e[key] for key in pairs}

    state = {(dd, h): s_ref[0, dd, h] for dd in range(2) for h in heads}
    zeros = jnp.zeros((c, HEAD_DIM), BF16)
    for pp, j in [(pp, j) for pp in range(n_pairs) for j in range(2)]:
        now = [(dd, pp, h) for dd in range(2) for h in heads]
        rs = {key: chunk_rows(key[0], j) for key in now}
        tot = {}
        for dd, _, h in now:
            r = rs[(dd, pp, h)]
            edge = r.stop - 1 if dd == 0 else r.start
            tot[(dd, pp, h)] = gates[(dd, pp)][edge:edge + 1, cum + h:cum + h + 1]
        s16 = {key: state[(key[0], key[2])].astype(BF16) for key in now}
        ws = {key: _dot(jnp.concatenate([sol[key][rs[key], HEAD_DIM:], qe[key][rs[key]]], axis=0).astype(BF16),
                        s16[key]) for key in now}
        v16 = {key: (sol[key][rs[key], :HEAD_DIM] - ws[key][:c]).astype(BF16) for key in now}
        k_dec = {key: (kf[key][rs[key]] * jnp.exp(tot[key] - gcc[key][rs[key]])).astype(BF16) for key in now}
        for key in now:
            sk = (key[0], key[2])
            state[sk] = state[sk] * jnp.exp(tot[key]) + _dot_tn(k_dec[key], v16[key])
        if with_output:
            vpad = {key: jnp.concatenate([v16[key], zeros] if rs[key].start == 0 else [zeros, v16[key]], axis=0)
                    for key in now}
            o = {key: ws[key][c:] + _dot(qk[key][rs[key], :], vpad[key]) for key in now}
            for dd, _, h in now:
                r0 = pair_rows(dd, pp).start + rs[(dd, pp, h)].start
                in_refs[dd][4][0, r0:r0 + c, lanes(h)] = o[(dd, pp, h)].astype(of_ref.dtype)
    for dd in range(2):
        for h in heads:
            s_ref[0, dd, h] = state[(dd, h)]


def _delta_scan(q, k, v, gates, s0, c, cps, with_output):
    b, s, d = q.shape
    rb = c * cps
    nb = s // rb
    fwd = lambda bb, i: (bb, i, 0)
    bwd = lambda bb, i: (bb, nb - 1 - i, 0)
    state_spec = pl.BlockSpec((1, 2, N_HEADS, HEAD_DIM, HEAD_DIM), lambda bb, i: (bb, 0, 0, 0, 0))
    state_shape = jax.ShapeDtypeStruct((b, 2, N_HEADS, HEAD_DIM, HEAD_DIM), F32)
    out_specs, out_shape = [state_spec], [state_shape]
    if with_output:
        out_specs = [pl.BlockSpec((1, rb, d), fwd), pl.BlockSpec((1, rb, d), bwd)] + out_specs
        out_shape = [jax.ShapeDtypeStruct((b, s, d), BF16)] * 2 + out_shape
    return pl.pallas_call(
        functools.partial(_scan_kernel, c=c, cps=cps, with_output=with_output),
        grid=(b, nb),
        in_specs=([pl.BlockSpec((1, rb, d), fwd)] * 3 + [pl.BlockSpec((1, rb, LANES), fwd)]
                  + [pl.BlockSpec((1, rb, d), bwd)] * 3
                  + [pl.BlockSpec((1, rb, LANES), lambda bb, i: (bb, nb - 1 - i, 1)), state_spec]),
        out_specs=out_specs,
        out_shape=out_shape,
        compiler_params=_params("parallel", "arbitrary"),
    )(q, k, v, gates, q, k, v, gates, s0)


def _ffn_pre(x1, gf, shf, scf, wr_ref, x1_ref, hf_ref, rt_ref):
    x1_ref[...] = x1
    hf = _rms_mod(x1, gf, shf, scf)
    d = hf.shape[1]
    hf_ref[:, :d] = hf
    hi = hf.astype(BF16)
    lo = (hf - hi.astype(F32)).astype(BF16)
    a = _dot(hi, wr_ref[...])
    logits = a[:, :LANES] + a[:, LANES:] + _dot(lo, wr_ref[:, :LANES])
    lane = lax.broadcasted_iota(I32, logits.shape, 1)
    lanef = lane.astype(F32)
    big = float(LANES)
    gl = jnp.where(lane < N_GROUPS, logits, -jnp.inf)
    gmax = jnp.max(gl, axis=-1, keepdims=True)
    gi = jnp.min(jnp.where(gl == gmax, lanef, big), axis=-1, keepdims=True)
    pg_sel = 1.0 / jnp.sum(jnp.exp(gl - gmax), axis=-1, keepdims=True)
    rel = lanef - (N_GROUPS + EXPERTS_PER_GROUP * gi)
    el = jnp.where((rel >= 0.0) & (rel < float(EXPERTS_PER_GROUP)), logits, -jnp.inf)
    m1 = jnp.max(el, axis=-1, keepdims=True)
    i1 = jnp.min(jnp.where(el == m1, lanef, big), axis=-1, keepdims=True)
    el2 = jnp.where(lanef == i1, -jnp.inf, el)
    m2 = jnp.max(el2, axis=-1, keepdims=True)
    i2 = jnp.min(jnp.where(el2 == m2, lanef, big), axis=-1, keepdims=True)
    r2 = jnp.exp(m2 - m1)
    w1 = pg_sel / (1.0 + r2)
    w2 = pg_sel * r2 / (1.0 + r2)
    first_low = i1 < i2
    a_loc = jnp.minimum(i1, i2) - (N_GROUPS + EXPERTS_PER_GROUP * gi)
    b_loc = jnp.maximum(i1, i2) - (N_GROUPS + EXPERTS_PER_GROUP * gi)
    pair = a_loc * (EXPERTS_PER_GROUP - 1) - a_loc * (a_loc - 1.0) * 0.5 + (b_loc - a_loc - 1.0)
    cls = gi * float(PAIRS_PER_GROUP) + pair
    rt = jnp.where(lane == 0, jnp.where(first_low, w1, w2), jnp.where(
        lane == 1, jnp.where(first_low, w2, w1), jnp.where(lane == 2, cls, 0.0)))
    rt_ref[...] = rt
    hf_ref[:, d:] = rt


def _delta_out_kernel(of_ref, ob_ref, z_ref, x_ref, gt_ref, on_ref, wo_ref, gf_ref, shf_ref, scf_ref, wr_ref,
                      x1_ref, hf_ref, rt_ref):
    o = of_ref[0].astype(F32) + ob_ref[0].astype(F32)
    z = z_ref[0].astype(F32)
    onorm = on_ref[...]
    parts = []
    for h in range(N_HEADS):
        oh = o[:, h * HEAD_DIM:(h + 1) * HEAD_DIM]
        parts.append(oh * lax.rsqrt(jnp.mean(oh * oh, axis=-1, keepdims=True) + EPS) * onorm)
    y = jnp.concatenate(parts, axis=1) * _silu(z)
    x1 = x_ref[0] + gt_ref[0] * _dot(y.astype(BF16), wo_ref[...])
    _ffn_pre(x1, gf_ref[...], shf_ref[0], scf_ref[0], wr_ref, x1_ref, hf_ref, rt_ref)


def _delta_out(o_f, o_b, z, x, gt, onorm, wo, gf, shf, scf, wr, tm):
    b, s, d = x.shape
    t = b * s
    nt = s // tm
    row = lambda bb, i: (bb, i, 0)
    vec = lambda bb, i: (bb, 0, 0)
    full = lambda bb, i: (0, 0)
    flat = lambda bb, i: (bb * nt + i, 0)
    return pl.pallas_call(
        _delta_out_kernel,
        grid=(b, nt),
        in_specs=[pl.BlockSpec((1, tm, d), row), pl.BlockSpec((1, tm, d), row),
                  pl.BlockSpec((1, tm, d), row), pl.BlockSpec((1, tm, d), row),
                  pl.BlockSpec((1, 1, d), vec), pl.BlockSpec((1, HEAD_DIM), full),
                  pl.BlockSpec((d, d), full), pl.BlockSpec((1, d), full),
                  pl.BlockSpec((1, 1, d), vec), pl.BlockSpec((1, 1, d), vec),
                  pl.BlockSpec((d, 2 * LANES), full)],
        out_specs=[pl.BlockSpec((tm, d), flat), pl.BlockSpec((tm, d + LANES), flat), pl.BlockSpec((tm, LANES), flat)],
        out_shape=[jax.ShapeDtypeStruct((t, d), F32), jax.ShapeDtypeStruct((t, d + LANES), F32),
                   jax.ShapeDtypeStruct((t, LANES), F32)],
        compiler_params=_params("parallel", "parallel"),
    )(o_f, o_b, z, x, gt, onorm, wo, gf, shf, scf, wr)


def _sconv_kernel(xm_ref, xp_ref, xn_ref, sh_ref, sc_ref, g_ref, win_ref, cw_ref, wo_ref, gt_ref,
                  gf_ref, shf_ref, scf_ref, wr_ref, x1_ref, hf_ref, rt_ref, *, tm, ck):
    i = pl.program_id(1)
    n = pl.num_programs(1)
    g, sh, sc = g_ref[...], sh_ref[0], sc_ref[0]
    xm = xm_ref[0]
    hm = _rms_mod(xm, g, sh, sc)
    hp = _rms_mod(xp_ref[0], g, sh, sc) * (i > 0).astype(F32)
    hn = _rms_mod(xn_ref[0], g, sh, sc) * (i < n - 1).astype(F32)
    hm16 = hm.astype(BF16)
    hext = jnp.concatenate([hp, hm, hn], axis=0).astype(BF16)
    d = xm.shape[1]
    assert (d // 2) % ck == 0
    col = lax.broadcasted_iota(I32, (tm, 1), 0) % GRID_W
    acc = jnp.zeros((tm, d), F32)
    for j in range(d // ck):
        c0 = j * ck
        gate_b = _dot(hm16, win_ref[:, c0:c0 + ck])
        cw = cw_ref[:, c0:c0 + ck]
        if c0 < d // 2:
            u = _dot(hm16, win_ref[:, d + c0:d + c0 + ck]) * _dot(hm16, win_ref[:, 2 * d + c0:2 * d + c0 + ck])
            left = jnp.where(col == 0, 0.0, pltpu.roll(u, 1, 0))
            right = jnp.where(col == GRID_W - 1, 0.0, pltpu.roll(u, tm - 1, 0))
            y = cw[0:1] * left + cw[1:2] * u + cw[2:3] * right
        else:
            u = _dot(hext, win_ref[:, d + c0:d + c0 + ck]) * _dot(hext, win_ref[:, 2 * d + c0:2 * d + c0 + ck])
            y = cw[0:1] * u[0:tm] + cw[1:2] * u[GRID_W:GRID_W + tm] + cw[2:3] * u[2 * GRID_W:2 * GRID_W + tm]
        acc = acc + _dot((gate_b * y).astype(BF16), wo_ref[c0:c0 + ck, :])
    x1 = xm + gt_ref[0] * acc
    _ffn_pre(x1, gf_ref[...], shf_ref[0], scf_ref[0], wr_ref, x1_ref, hf_ref, rt_ref)


def _sconv(x, sh, sc, gnorm, win, conv_w, wo, gt, gf, shf, scf, wr, tm):
    b, s, d = x.shape
    t = b * s
    nt = s // tm
    hb = tm // GRID_W
    last = s // GRID_W - 1
    row = lambda bb, i: (bb, i, 0)
    vec = lambda bb, i: (bb, 0, 0)
    full = lambda bb, i: (0, 0)
    flat = lambda bb, i: (bb * nt + i, 0)
    return pl.pallas_call(
        functools.partial(_sconv_kernel, tm=tm, ck=512),
        grid=(b, nt),
        in_specs=[pl.BlockSpec((1, tm, d), row),
                  pl.BlockSpec((1, GRID_W, d), lambda bb, i: (bb, jnp.maximum(i * hb - 1, 0), 0)),
                  pl.BlockSpec((1, GRID_W, d), lambda bb, i: (bb, jnp.minimum((i + 1) * hb, last), 0)),
                  pl.BlockSpec((1, 1, d), vec), pl.BlockSpec((1, 1, d), vec), pl.BlockSpec((1, d), full),
                  pl.BlockSpec((d, 3 * d), full), pl.BlockSpec((3, d), full), pl.BlockSpec((d, d), full),
                  pl.BlockSpec((1, 1, d), vec), pl.BlockSpec((1, d), full),
                  pl.BlockSpec((1, 1, d), vec), pl.BlockSpec((1, 1, d), vec),
                  pl.BlockSpec((d, 2 * LANES), full)],
        out_specs=[pl.BlockSpec((tm, d), flat), pl.BlockSpec((tm, d + LANES), flat), pl.BlockSpec((tm, LANES), flat)],
        out_shape=[jax.ShapeDtypeStruct((t, d), F32), jax.ShapeDtypeStruct((t, d + LANES), F32),
                   jax.ShapeDtypeStruct((t, LANES), F32)],
        compiler_params=_params("parallel", "parallel"),
    )(x, x, x, sh, sc, gnorm, win, conv_w, wo, gt, gf, shf, scf, wr)


def _rank_kernel(c_ref, dest_ref, ends_ref, cnt_ref, start_ref, carry_ref, *, tm, bm):
    p = pl.program_id(0)
    i = pl.program_id(1)
    onehot = lax.broadcasted_iota(I32, (LANES, tm), 0) == c_ref[0:1, :]
    ohf = onehot.astype(F32)
    per_class = jnp.broadcast_to(jnp.sum(ohf, axis=1, keepdims=True), (LANES, LANES))

    @pl.when(jnp.logical_and(p == 0, i == 0))
    def _():
        cnt_ref[...] = jnp.zeros_like(cnt_ref)

    @pl.when(p == 0)
    def _():
        cnt_ref[...] += per_class

    @pl.when(jnp.logical_and(p == 1, i == 0))
    def _():
        padded = ((cnt_ref[...].astype(I32) + (bm - 1)) & (-bm)).astype(F32)
        before = (lax.broadcasted_iota(I32, (LANES, LANES), 0) > lax.broadcasted_iota(I32, (LANES, LANES), 1))
        start = _dot(before.astype(F32), padded, HIGHEST)
        start_ref[...] = start
        ends_ref[...] = (start + padded).astype(I32)
        carry_ref[...] = jnp.zeros_like(carry_ref)

    @pl.when(p == 1)
    def _():
        earlier = (lax.broadcasted_iota(I32, (tm, tm), 0) < lax.broadcasted_iota(I32, (tm, tm), 1)).astype(BF16)
        slot = _dot(onehot.astype(BF16), earlier) + carry_ref[:, 0:1] + start_ref[:, 0:1]
        dest = jnp.sum(ohf * slot, axis=0, keepdims=True)
        dest_ref[...] = jnp.concatenate([dest, jnp.zeros((7, tm), F32)], axis=0).astype(I32)
        carry_ref[...] += per_class


def _rank(cls8, tm, bm):
    t = cls8.shape[1]
    return pl.pallas_call(
        functools.partial(_rank_kernel, tm=tm, bm=bm),
        grid=(2, t // tm),
        in_specs=[pl.BlockSpec((8, tm), lambda p, i: (0, i))],
        out_specs=[pl.BlockSpec((8, tm), lambda p, i: (0, i * p)), pl.BlockSpec((LANES, LANES), lambda p, i: (0, 0))],
        out_shape=[jax.ShapeDtypeStruct((8, t), I32), jax.ShapeDtypeStruct((LANES, LANES), I32)],
        scratch_shapes=[pltpu.VMEM((LANES, LANES), F32)] * 3,
        compiler_params=_params("arbitrary", "arbitrary"),
    )(cls8)


def _row_copy(src_ref, src_row, dst_ref, dst_row, sem):
    return pltpu.make_async_copy(src_ref.at[pl.ds(src_row, 1), :], dst_ref.at[pl.ds(dst_row, 1), :], sem)


def _dispatch_kernel(dest_ref, hf_ref, xb_in_ref, xb_ref, sem, *, td):
    del xb_in_ref

    def issue(r, carry):
        _row_copy(hf_ref, r, xb_ref, dest_ref[0, 0, r], sem).start()
        return carry

    for r in range(td):
        issue(r, 0)

    def drain(r, carry):
        _row_copy(hf_ref, 0, xb_ref, 0, sem).wait()
        return carry

    lax.fori_loop(0, td, drain, 0, unroll=8)


def _dispatch(dest3, hf, xb0, td):
    t, d = hf.shape
    cap = xb0.shape[0]
    return pl.pallas_call(
        functools.partial(_dispatch_kernel, td=td),
        grid=(t // td,),
        in_specs=[pl.BlockSpec((1, 1, td), lambda i: (i, 0, 0), memory_space=pltpu.SMEM),
                  pl.BlockSpec((td, d), lambda i: (i, 0)),
                  pl.BlockSpec(memory_space=pl.ANY)],
        out_specs=pl.BlockSpec(memory_space=pl.ANY),
        out_shape=jax.ShapeDtypeStruct((cap, d), hf.dtype),
        scratch_shapes=[pltpu.SemaphoreType.DMA],
        input_output_aliases={2: 0},
        compiler_params=_params("arbitrary"),
    )(dest3, hf, xb0)


def _expert_kernel(grp_ref, la_ref, lb_ref, bs_ref, nu_ref, x_ref, w1_hbm, w3_hbm, w2_hbm, y_ref,
                   res1, res3, res2, stage_up, stage_down, sems, *, layer):
    del bs_ref
    i = pl.program_id(0)
    d = y_ref.shape[1]
    grp = grp_ref[i]

    @pl.when(jnp.logical_or(i == 0, grp != grp_ref[jnp.maximum(i - 1, 0)]))
    def _():
        plan = []
        for e in range(EXPERTS_PER_GROUP):
            plan += [(w1_hbm, res1, stage_up.at[0], 0, e), (w3_hbm, res3, stage_up.at[1], 1, e),
                     (w2_hbm, res2, stage_down, 2, e)]

        def copy(step):
            src, _, stage, sem, e = plan[step]
            return pltpu.make_async_copy(src.at[layer, grp * EXPERTS_PER_GROUP + e], stage, sems.at[sem])

        copy(0).start()
        for step, (_, dst, stage, _, e) in enumerate(plan):
            if step + 1 < len(plan):
                copy(step + 1).start()
            copy(step).wait()
            dst[e] = stage[...].astype(BF16)

    @pl.when(i < nu_ref[0])
    def _():
        x = x_ref[:, :d].astype(BF16)

        def swiglu(e):
            h = _silu(_dot(x, res1[e])) * _dot(x, res3[e])
            return _dot(h.astype(BF16), res2[e])

        y_ref[...] = x_ref[:, d:d + 1] * swiglu(la_ref[i]) + x_ref[:, d + 1:d + 2] * swiglu(lb_ref[i])

    @pl.when(i >= nu_ref[0])
    def _():
        y_ref[...] = jnp.zeros_like(y_ref)


def _experts(block_grp, block_la, block_lb, block_src, n_used, xb, w1, w3, w2, layer, bm):
    cap, dx = xb.shape
    _, _, d, de = w1.shape
    grid_spec = pltpu.PrefetchScalarGridSpec(
        num_scalar_prefetch=5,
        grid=(cap // bm,),
        in_specs=[pl.BlockSpec((bm, dx), lambda i, g, la, lb, bs, nu: (bs[i], 0)),
                  pl.BlockSpec(memory_space=pl.ANY), pl.BlockSpec(memory_space=pl.ANY),
                  pl.BlockSpec(memory_space=pl.ANY)],
        out_specs=pl.BlockSpec((bm, d), lambda i, g, la, lb, bs, nu: (i, 0)),
        scratch_shapes=[pltpu.VMEM((EXPERTS_PER_GROUP, d, de), BF16), pltpu.VMEM((EXPERTS_PER_GROUP, d, de), BF16),
                        pltpu.VMEM((EXPERTS_PER_GROUP, de, d), BF16), pltpu.VMEM((2, d, de), F32),
                        pltpu.VMEM((de, d), F32), pltpu.SemaphoreType.DMA((3,))],
    )
    return pl.pallas_call(
        functools.partial(_expert_kernel, layer=layer),
        grid_spec=grid_spec,
        out_shape=jax.ShapeDtypeStruct((cap, d), F32),
        compiler_params=_params("arbitrary"),
    )(block_grp, block_la, block_lb, block_src, n_used, xb, w1, w3, w2)


def _combine_kernel(dest_ref, x1_ref, gt_ref, fn_ref, yb_ref, o_ref, buf, sems, *, tc, final):
    i = pl.program_id(0)
    n_tiles = pl.num_programs(0) - 1

    @pl.when(i < n_tiles)
    def _():
        slot = i % 2
        for r in range(tc):
            _row_copy(yb_ref, dest_ref[0, 0, r], buf.at[slot], r, sems.at[slot]).start()

    @pl.when(i > 0)
    def _():
        slot = (i - 1) % 2

        def drain(r, carry):
            _row_copy(yb_ref, 0, buf.at[slot], 0, sems.at[slot]).wait()
            return carry

        lax.fori_loop(0, tc, drain, 0, unroll=8)
        x2 = x1_ref[...] + gt_ref[0] * buf[slot]
        if final:
            ms = jnp.mean(x2 * x2, axis=-1, keepdims=True)
            x2 = x2 * lax.rsqrt(ms + EPS) * fn_ref[...]
        o_ref[...] = x2


def _combine(dest3, x1, gt, fnorm, yb, s, tc, final):
    t, d = x1.shape
    per_b = s // tc
    n_tiles = t // tc
    done = lambda i: jnp.maximum(i - 1, 0)
    return pl.pallas_call(
        functools.partial(_combine_kernel, tc=tc, final=final),
        grid=(n_tiles + 1,),
        in_specs=[pl.BlockSpec((1, 1, tc), lambda i: (jnp.minimum(i, n_tiles - 1), 0, 0), memory_space=pltpu.SMEM),
                  pl.BlockSpec((tc, d), lambda i: (done(i), 0)),
                  pl.BlockSpec((1, 1, d), lambda i: (done(i) // per_b, 0, 0)),
                  pl.BlockSpec((1, d), lambda i: (0, 0)),
                  pl.BlockSpec(memory_space=pl.ANY)],
        out_specs=pl.BlockSpec((tc, d), lambda i: (done(i), 0)),
        out_shape=jax.ShapeDtypeStruct((t, d), F32),
        scratch_shapes=[pltpu.VMEM((2, tc, d), F32), pltpu.SemaphoreType.DMA((2,))],
        compiler_params=_params("arbitrary"),
    )(dest3, x1, gt, fnorm, yb)


def _moe(hfx, rt, x1, gt, fnorm, w1, w3, w2, layer, xb0, s, final, bm, tr=512, td=1024):
    t = hfx.shape[0]
    cls8 = jnp.zeros((8, t), I32).at[0].set(rt[:, 2].astype(I32))
    dest8, ends2 = _rank(cls8, tr, bm)
    dest3 = dest8[0].reshape(t // td, 1, td)
    ends = ends2[:N_CLASSES, 0]
    n_blocks = xb0.shape[0] // bm
    n_used = ends[-1] // bm
    block_src = jnp.maximum(jnp.minimum(jnp.arange(n_blocks, dtype=I32), n_used - 1), 0)
    block_cls = jnp.minimum(jnp.sum((ends[None, :] <= (block_src * bm)[:, None]).astype(I32), axis=1), N_CLASSES - 1)
    block_la = jnp.asarray(CLASS_LOCAL_A, I32)[block_cls]
    block_lb = jnp.asarray(CLASS_LOCAL_B, I32)[block_cls]
    xb = _dispatch(dest3, hfx, xb0, td)
    yb = _experts(block_cls // PAIRS_PER_GROUP, block_la, block_lb, block_src, n_used.reshape(1), xb, w1, w3, w2,
                  layer, bm)
    return _combine(dest3, x1, gt, fnorm, yb, s, td, final), xb


def _router_weights(router_g, router_e):
    d = router_g.shape[0]
    w = jnp.zeros((d, LANES), F32).at[:, :N_GROUPS].set(router_g).at[:, N_GROUPS:N_GROUPS + N_EXPERTS].set(router_e)
    hi = w.astype(BF16)
    lo = (w - hi.astype(F32)).astype(BF16)
    return jnp.concatenate([hi, lo], axis=1)


def _gate_weights(w_in, a_log, dt_bias):
    d = w_in.shape[0]
    base = 4 * d
    wg = jnp.zeros((d, 2 * LANES), F32)
    gp = jnp.zeros((2, 2 * LANES), F32)
    for direction in range(2):
        o = direction * LANES
        wg = wg.at[:, o:o + N_HEADS].set(w_in[:, base + direction * N_HEADS:base + (direction + 1) * N_HEADS])
        wg = wg.at[:, o + N_HEADS:o + 2 * N_HEADS].set(
            w_in[:, base + (2 + direction) * N_HEADS:base + (3 + direction) * N_HEADS])
        wg = wg.at[:, o + 2 * N_HEADS:o + 3 * N_HEADS].set(
            w_in[:, base + (2 + direction) * N_HEADS:base + (3 + direction) * N_HEADS])
        for rep in (1, 2):
            gp = gp.at[0, o + rep * N_HEADS:o + (rep + 1) * N_HEADS].set(a_log[direction])
            gp = gp.at[1, o + rep * N_HEADS:o + (rep + 1) * N_HEADS].set(dt_bias[direction])
    return wg.astype(BF16), gp


def kernel(x, c, ctx, c_ctx, ada_w, ada_b, norm_mix, norm_ffn, w_in_a, conv_a, a_log_a, dt_bias_a, onorm_a, w_out_a,
           w_in_b, conv_b, w_out_b, router_g, router_e, w1, w3, w2, final_norm):
    b, s, d = x.shape
    chunk = SCAN_CHUNK
    tm = min(512, s)

    cond = jnp.zeros((16, d), F32).at[:b].set(c).at[b].set(c_ctx)
    mod = _ada_params(cond, ada_w, ada_b)

    def mods(layer, rows):
        m = mod[layer, rows]
        return [jnp.broadcast_to(m[:, None, k * d:(k + 1) * d], (b, 1, d)) for k in range(N_MOD)]

    sh_m, sc_m, gt_m, sh_f, sc_f, gt_f = mods(0, slice(0, b))
    csh_m, csc_m = mods(0, slice(b, b + 1))[:2]
    w_in = w_in_a[0]
    wqkv, wz = w_in[:, :3 * d].astype(BF16), w_in[:, 3 * d:4 * d].astype(BF16)
    wg, gp = _gate_weights(w_in, a_log_a[0], dt_bias_a[0])
    gn = norm_mix[0].reshape(1, d)
    qc, kc, vc, _, gc = _delta_in(ctx, csh_m, csc_m, gn, wqkv, conv_a[0], wz, wg, gp, tm)
    s_zero = jnp.zeros((b, 2, N_HEADS, HEAD_DIM, HEAD_DIM), F32)
    (s_ctx,) = _delta_scan(qc, kc, vc, gc, s_zero, chunk, SCAN_CHUNKS_PER_STEP, False)
    ql, kl, vl, zl, gl = _delta_in(x, sh_m, sc_m, gn, wqkv, conv_a[0], wz, wg, gp, tm)
    o_f, o_b, _ = _delta_scan(ql, kl, vl, gl, s_ctx, chunk, SCAN_CHUNKS_PER_STEP, True)
    wr = _router_weights(router_g[0], router_e[0])
    x1, hf, rt = _delta_out(o_f, o_b, zl, x, gt_m, onorm_a[0].reshape(1, HEAD_DIM), w_out_a[0].astype(BF16),
                            norm_ffn[0].reshape(1, d), sh_f, sc_f, wr, tm)
    fn = final_norm.reshape(1, d)
    xb0 = jnp.zeros(((-(-(b * s) // MOE_BLOCK) + N_CLASSES) * MOE_BLOCK, d + LANES), F32)
    x2, xb0 = _moe(hf, rt, x1, gt_f, fn, w1, w3, w2, 0, xb0, s, False, MOE_BLOCK)
    x2 = x2.reshape(b, s, d)

    sh_m, sc_m, gt_m, sh_f, sc_f, gt_f = mods(1, slice(0, b))
    wr = _router_weights(router_g[1], router_e[1])
    x1, hf, rt = _sconv(x2, sh_m, sc_m, norm_mix[1].reshape(1, d), w_in_b[0].astype(BF16), conv_b[0],
                        w_out_b[0].astype(BF16), gt_m, norm_ffn[1].reshape(1, d), sh_f, sc_f, wr, tm)
    out, _ = _moe(hf, rt, x1, gt_f, fn, w1, w3, w2, 1, xb0, s, True, MOE_BLOCK)
    return out.reshape(b, s, d)
```

```python
import functools

import jax
import jax.numpy as jnp
from jax import lax
from jax.experimental import pallas as pl
from jax.experimental.pallas import tpu as pltpu

F32 = jnp.float32
BF16 = jnp.bfloat16
I32 = jnp.int32
HIGHEST = lax.Precision.HIGHEST

EPS = 1e-6
N_HEADS = 8
HEAD_DIM = 128
GRID_W = 64
N_GROUPS = 4
EXPERTS_PER_GROUP = 8
N_EXPERTS = N_GROUPS * EXPERTS_PER_GROUP
PAIRS_PER_GROUP = EXPERTS_PER_GROUP * (EXPERTS_PER_GROUP - 1) // 2
N_CLASSES = N_GROUPS * PAIRS_PER_GROUP
_PAIRS = [(a, b) for a in range(EXPERTS_PER_GROUP) for b in range(a + 1, EXPERTS_PER_GROUP)]
CLASS_LOCAL_A = [a for _ in range(N_GROUPS) for a, _ in _PAIRS]
CLASS_LOCAL_B = [b for _ in range(N_GROUPS) for _, b in _PAIRS]
MOE_BLOCK = 256
N_MOD = 6
LANES = 128
SCAN_CHUNK = 64
SCAN_CHUNKS_PER_STEP = 4
SEQ_HALO = 8
VMEM_LIMIT = 48 * 1024 * 1024


def _params(*sem):
    return pltpu.CompilerParams(dimension_semantics=sem, vmem_limit_bytes=VMEM_LIMIT)


def _dot(a, b, precision=None):
    return jnp.dot(a, b, preferred_element_type=F32, precision=precision)


def _dot_nt(a, b, precision=None):
    return lax.dot_general(a, b, (((1,), (1,)), ((), ())), preferred_element_type=F32, precision=precision)


def _dot_tn(a, b, precision=None):
    return lax.dot_general(a, b, (((0,), (0,)), ((), ())), preferred_element_type=F32, precision=precision)


def _silu(x):
    return x * jax.nn.sigmoid(x)


def _rms_mod(x, g, shift, scale):
    ms = jnp.mean(x * x, axis=-1, keepdims=True)
    return x * lax.rsqrt(ms + EPS) * g * (1.0 + scale) + shift


def _ada_kernel(c_ref, w_ref, b_ref, o_ref):
    o_ref[0] = _dot(_silu(c_ref[...]), w_ref[0], HIGHEST) + b_ref[0]


def _ada_params(cond, ada_w, ada_b):
    depth, d, n = ada_w.shape
    tn = 512
    r = cond.shape[0]
    return pl.pallas_call(
        _ada_kernel,
        grid=(depth, n // tn),
        in_specs=[pl.BlockSpec((r, d), lambda l, j: (0, 0)),
                  pl.BlockSpec((1, d, tn), lambda l, j: (l, 0, j)),
                  pl.BlockSpec((1, 1, tn), lambda l, j: (l, 0, j))],
        out_specs=pl.BlockSpec((1, r, tn), lambda l, j: (l, 0, j)),
        out_shape=jax.ShapeDtypeStruct((depth, r, n), F32),
        compiler_params=_params("parallel", "parallel"),
    )(cond, ada_w, ada_b.reshape(depth, 1, n))


def _delta_in_kernel(xm_ref, xp_ref, xn_ref, sh_ref, sc_ref, g_ref, wqkv_ref, cw_ref, wz_ref, wg_ref, gp_ref, tri_ref,
                     q_ref, k_ref, v_ref, z_ref, gate_ref, *, tm, ck):
    i = pl.program_id(1)
    n = pl.num_programs(1)
    g, sh, sc = g_ref[...], sh_ref[0], sc_ref[0]
    hm = _rms_mod(xm_ref[0], g, sh, sc)
    hp = _rms_mod(xp_ref[0], g, sh, sc) * (i > 0).astype(F32)
    hn = _rms_mod(xn_ref[0], g, sh, sc) * (i < n - 1).astype(F32)
    hm16 = hm.astype(BF16)
    hext = jnp.concatenate([hp, hm, hn], axis=0).astype(BF16)
    rows = tm + 2 * SEQ_HALO
    d = hm.shape[1]
    outs = (q_ref, k_ref, v_ref)
    for j in range(3 * d // ck):
        p = _dot(hext, wqkv_ref[:, j * ck:(j + 1) * ck])
        cw = cw_ref[:, j * ck:(j + 1) * ck]
        y = cw[0:1] * pltpu.roll(p, 1, 0) + cw[1:2] * p + cw[2:3] * pltpu.roll(p, rows - 1, 0)
        y = _silu(y[SEQ_HALO:SEQ_HALO + tm])
        which, col0 = (j * ck) // d, (j * ck) % d
        for hh in range(ck // HEAD_DIM):
            yh = y[:, hh * HEAD_DIM:(hh + 1) * HEAD_DIM]
            if which < 2:
                inv = lax.rsqrt(jnp.sum(yh * yh, axis=-1, keepdims=True) + EPS)
                yh = yh * (inv * HEAD_DIM ** -0.5 if which == 0 else inv)
            c0 = col0 + hh * HEAD_DIM
            outs[which][0, :, c0:c0 + HEAD_DIM] = yh.astype(q_ref.dtype)
    z_ref[0] = _dot(hm16, wz_ref[...]).astype(z_ref.dtype)
    pg = _dot(hm16, wg_ref[...])
    lane = lax.broadcasted_iota(I32, pg.shape, 1) % LANES
    xg = pg + gp_ref[1:2]
    softplus = jnp.maximum(xg, 0.0) + jnp.log1p(jnp.exp(-jnp.abs(xg)))
    gval = -jnp.exp(gp_ref[0:1]) * softplus
    g_hi = gval.astype(BF16)
    r1 = gval - g_hi.astype(F32)
    g_mid = r1.astype(BF16)
    g_lo = (r1 - g_mid.astype(F32)).astype(BF16)
    cums = []
    for direction in range(2):
        sl = slice(direction * LANES, (direction + 1) * LANES)
        terms = jnp.concatenate([g_hi[:, sl], g_mid[:, sl], g_lo[:, sl]], axis=1)
        cs = _dot(tri_ref[direction], terms)
        cums.append(cs[:, :LANES] + cs[:, LANES:2 * LANES] + cs[:, 2 * LANES:])
    gcum = jnp.concatenate(cums, axis=1)
    gate_ref[0] = jnp.where(lane < N_HEADS, jax.nn.sigmoid(pg), jnp.where(
        lane < 2 * N_HEADS, gval, jnp.where(lane < 3 * N_HEADS, gcum, 0.0)))


def _delta_in(x, sh, sc, gnorm, wqkv, conv_w, wz, wg, gp, tm):
    b, s, d = x.shape
    tm = min(tm, s)
    hb = tm // SEQ_HALO
    last = s // SEQ_HALO - 1
    row = lambda bb, i: (bb, i, 0)
    vec = lambda bb, i: (bb, 0, 0)
    full = lambda bb, i: (0, 0)
    outs = [jax.ShapeDtypeStruct((b, s, d), BF16)] * 4 + [jax.ShapeDtypeStruct((b, s, 2 * LANES), F32)]
    ri = lax.broadcasted_iota(I32, (tm, tm), 0)
    ci = lax.broadcasted_iota(I32, (tm, tm), 1)
    same = (ri // SCAN_CHUNK) == (ci // SCAN_CHUNK)
    tri = jnp.stack([same & (ri >= ci), same & (ri <= ci)]).astype(BF16)
    return pl.pallas_call(
        functools.partial(_delta_in_kernel, tm=tm, ck=512),
        grid=(b, s // tm),
        in_specs=[pl.BlockSpec((1, tm, d), row),
                  pl.BlockSpec((1, SEQ_HALO, d), lambda bb, i: (bb, jnp.maximum(i * hb - 1, 0), 0)),
                  pl.BlockSpec((1, SEQ_HALO, d), lambda bb, i: (bb, jnp.minimum((i + 1) * hb, last), 0)),
                  pl.BlockSpec((1, 1, d), vec), pl.BlockSpec((1, 1, d), vec),
                  pl.BlockSpec((1, d), full),
                  pl.BlockSpec((d, 3 * d), full), pl.BlockSpec((3, 3 * d), full),
                  pl.BlockSpec((d, d), full), pl.BlockSpec((d, 2 * LANES), full),
                  pl.BlockSpec((2, 2 * LANES), full), pl.BlockSpec((2, tm, tm), lambda bb, i: (0, 0, 0))],
        out_specs=[pl.BlockSpec((1, tm, d), row)] * 4 + [pl.BlockSpec((1, tm, 2 * LANES), row)],
        out_shape=outs,
        compiler_params=_params("parallel", "parallel"),
    )(x, x, x, sh, sc, gnorm, wqkv, conv_w, wz, wg, gp, tri)


def _scan_kernel(qf_ref, kf_ref, vf_ref, gf_ref, qb_ref, kb_ref, vb_ref, gb_ref, s0_ref, *out_refs, c, cps,
                 with_output):
    if with_output:
        of_ref, ob_ref, s_ref = out_refs
    else:
        (s_ref,) = out_refs
        of_ref = ob_ref = None
    step = pl.program_id(1)

    @pl.when(step == 0)
    def _():
        s_ref[...] = s0_ref[...]

    in_refs = ((qf_ref, kf_ref, vf_ref, gf_ref, of_ref), (qb_ref, kb_ref, vb_ref, gb_ref, ob_ref))
    heads = range(N_HEADS)
    rb = 2 * c
    assert cps % 2 == 0 and rb == LANES
    n_pairs = cps // 2
    ri = lax.broadcasted_iota(I32, (rb, rb), 0)
    ci = lax.broadcasted_iota(I32, (rb, rb), 1)
    same = (ri // c) == (ci // c)
    incl = (same & (ri >= ci), same & (ri <= ci))
    strict = (same & (ri > ci), same & (ri < ci))
    rp = lax.broadcasted_iota(I32, (c, rb), 0)
    cp = lax.broadcasted_iota(I32, (c, rb), 1)
    eye_pack = (cp % c == rp).astype(F32)
    left = cp < c

    def lanes(h):
        return slice(h * HEAD_DIM, (h + 1) * HEAD_DIM)

    def pair_rows(dd, pp):
        r0 = (pp if dd == 0 else n_pairs - 1 - pp) * rb
        return slice(r0, r0 + rb)

    def chunk_rows(dd, j):
        r0 = (j if dd == 0 else 1 - j) * c
        return slice(r0, r0 + c)

    pairs = [(dd, pp, h) for dd in range(2) for pp in range(n_pairs) for h in heads]
    gates = {(dd, pp): in_refs[dd][3][0, pair_rows(dd, pp), :]
             for dd in range(2) for pp in range(n_pairs)}
    gates_t = {key: g.T for key, g in gates.items()}
    q16 = {(dd, pp, h): in_refs[dd][0][0, pair_rows(dd, pp), lanes(h)] for dd, pp, h in pairs}
    k16 = {(dd, pp, h): in_refs[dd][1][0, pair_rows(dd, pp), lanes(h)] for dd, pp, h in pairs}
    qf = {key: q16[key].astype(F32) for key in pairs}
    kf = {key: k16[key].astype(F32) for key in pairs}
    vf = {(dd, pp, h): in_refs[dd][2][0, pair_rows(dd, pp), lanes(h)].astype(F32) for dd, pp, h in pairs}
    cum = 2 * N_HEADS
    beta = {(dd, pp, h): gates[(dd, pp)][:, h:h + 1] for dd, pp, h in pairs}
    gcc = {(dd, pp, h): gates[(dd, pp)][:, cum + h:cum + h + 1] for dd, pp, h in pairs}
    gcr = {(dd, pp, h): gates_t[(dd, pp)][cum + h:cum + h + 1, :] for dd, pp, h in pairs}
    decay = {key: jnp.exp(jnp.where(incl[key[0]], gcc[key] - gcr[key], -jnp.inf)) for key in pairs}
    e = {key: jnp.exp(gcc[key]) for key in pairs}
    kb = {key: kf[key] * beta[key] for key in pairs}
    a = {key: _dot_nt(jnp.concatenate([kb[key].astype(BF16), q16[key]], axis=0), k16[key]) for key in pairs}
    lbd = {key: jnp.where(strict[key[0]], a[key][:rb] * decay[key], 0.0) for key in pairs}
    qk = {key: (a[key][rb:] * decay[key]).astype(BF16) for key in pairs}
    p = {key: eye_pack - (lbd[key][:c] + lbd[key][c:]) for key in pairs}
    l16 = {key: lbd[key].astype(BF16) for key in pairs}
    m = {key: _dot(l16[key], l16[key]) for key in pairs}
    span = 2
    while span < c:
        span *= 2
        m16 = {key: m[key].astype(BF16) for key in pairs}
        if span < c:
            pm = {key: _dot(jnp.concatenate([p[key], m[key]], axis=0).astype(BF16), m16[key]) for key in pairs}
            p = {key: p[key] + pm[key][:c] for key in pairs}
            m = {key: pm[key][c:] for key in pairs}
        else:
            p = {key: p[key] + _dot(p[key].astype(BF16), m16[key]) for key in pairs}
    t_bd = {key: jnp.concatenate([jnp.where(left, p[key], 0.0), jnp.where(left, 0.0, p[key])], axis=0).astype(BF16)
            for key in pairs}
    rhs = {key: jnp.concatenate([vf[key] * beta[key], kb[key] * e[key]], axis=1).astype(BF16) for key in pairs}
    sol = {key: _dot(t_bd[key], rhs[key]) for key in pairs}
    qe = {key: qf[key] * e[key] for key in pairs}

    state = {(dd, h): s_ref[0, dd, h] for dd in range(2) for h in heads}
    zeros = jnp.zeros((c, HEAD_DIM), BF16)
    for pp, j in [(pp, j) for pp in range(n_pairs) for j in range(2)]:
        now = [(dd, pp, h) for dd in range(2) for h in heads]
        rs = {key: chunk_rows(key[0], j) for key in now}
        tot = {}
        for dd, _, h in now:
            r = rs[(dd, pp, h)]
            edge = r.stop - 1 if dd == 0 else r.start
            tot[(dd, pp, h)] = gates[(dd, pp)][edge:edge + 1, cum + h:cum + h + 1]
        s16 = {key: state[(key[0], key[2])].astype(BF16) for key in now}
        ws = {key: _dot(jnp.concatenate([sol[key][rs[key], HEAD_DIM:], qe[key][rs[key]]], axis=0).astype(BF16),
                        s16[key]) for key in now}
        v16 = {key: (sol[key][rs[key], :HEAD_DIM] - ws[key][:c]).astype(BF16) for key in now}
        k_dec = {key: (kf[key][rs[key]] * jnp.exp(tot[key] - gcc[key][rs[key]])).astype(BF16) for key in now}
        for key in now:
            sk = (key[0], key[2])
            state[sk] = state[sk] * jnp.exp(tot[key]) + _dot_tn(k_dec[key], v16[key])
        if with_output:
            vpad = {key: jnp.concatenate([v16[key], zeros] if rs[key].start == 0 else [zeros, v16[key]], axis=0)
                    for key in now}
            o = {key: ws[key][c:] + _dot(qk[key][rs[key], :], vpad[key]) for key in now}
            for dd, _, h in now:
                r0 = pair_rows(dd, pp).start + rs[(dd, pp, h)].start
                in_refs[dd][4][0, r0:r0 + c, lanes(h)] = o[(dd, pp, h)].astype(of_ref.dtype)
    for dd in range(2):
        for h in heads:
            s_ref[0, dd, h] = state[(dd, h)]


def _delta_scan(q, k, v, gates, s0, c, cps, with_output):
    b, s, d = q.shape
    rb = c * cps
    nb = s // rb
    fwd = lambda bb, i: (bb, i, 0)
    bwd = lambda bb, i: (bb, nb - 1 - i, 0)
    state_spec = pl.BlockSpec((1, 2, N_HEADS, HEAD_DIM, HEAD_DIM), lambda bb, i: (bb, 0, 0, 0, 0))
    state_shape = jax.ShapeDtypeStruct((b, 2, N_HEADS, HEAD_DIM, HEAD_DIM), F32)
    out_specs, out_shape = [state_spec], [state_shape]
    if with_output:
        out_specs = [pl.BlockSpec((1, rb, d), fwd), pl.BlockSpec((1, rb, d), bwd)] + out_specs
        out_shape = [jax.ShapeDtypeStruct((b, s, d), BF16)] * 2 + out_shape
    return pl.pallas_call(
        functools.partial(_scan_kernel, c=c, cps=cps, with_output=with_output),
        grid=(b, nb),
        in_specs=([pl.BlockSpec((1, rb, d), fwd)] * 3 + [pl.BlockSpec((1, rb, LANES), fwd)]
                  + [pl.BlockSpec((1, rb, d), bwd)] * 3
                  + [pl.BlockSpec((1, rb, LANES), lambda bb, i: (bb, nb - 1 - i, 1)), state_spec]),
        out_specs=out_specs,
        out_shape=out_shape,
        compiler_params=_params("parallel", "arbitrary"),
    )(q, k, v, gates, q, k, v, gates, s0)


def _ffn_pre(x1, gf, shf, scf, wr_ref, x1_ref, hf_ref, rt_ref):
    x1_ref[...] = x1
    hf = _rms_mod(x1, gf, shf, scf)
    d = hf.shape[1]
    hf_ref[:, :d] = hf
    hi = hf.astype(BF16)
    lo = (hf - hi.astype(F32)).astype(BF16)
    a = _dot(hi, wr_ref[...])
    logits = a[:, :LANES] + a[:, LANES:] + _dot(lo, wr_ref[:, :LANES])
    lane = lax.broadcasted_iota(I32, logits.shape, 1)
    lanef = lane.astype(F32)
    big = float(LANES)
    gl = jnp.where(lane < N_GROUPS, logits, -jnp.inf)
    gmax = jnp.max(gl, axis=-1, keepdims=True)
    gi = jnp.min(jnp.where(gl == gmax, lanef, big), axis=-1, keepdims=True)
    pg_sel = 1.0 / jnp.sum(jnp.exp(gl - gmax), axis=-1, keepdims=True)
    rel = lanef - (N_GROUPS + EXPERTS_PER_GROUP * gi)
    el = jnp.where((rel >= 0.0) & (rel < float(EXPERTS_PER_GROUP)), logits, -jnp.inf)
    m1 = jnp.max(el, axis=-1, keepdims=True)
    i1 = jnp.min(jnp.where(el == m1, lanef, big), axis=-1, keepdims=True)
    el2 = jnp.where(lanef == i1, -jnp.inf, el)
    m2 = jnp.max(el2, axis=-1, keepdims=True)
    i2 = jnp.min(jnp.where(el2 == m2, lanef, big), axis=-1, keepdims=True)
    r2 = jnp.exp(m2 - m1)
    w1 = pg_sel / (1.0 + r2)
    w2 = pg_sel * r2 / (1.0 + r2)
    first_low = i1 < i2
    a_loc = jnp.minimum(i1, i2) - (N_GROUPS + EXPERTS_PER_GROUP * gi)
    b_loc = jnp.maximum(i1, i2) - (N_GROUPS + EXPERTS_PER_GROUP * gi)
    pair = a_loc * (EXPERTS_PER_GROUP - 1) - a_loc * (a_loc - 1.0) * 0.5 + (b_loc - a_loc - 1.0)
    cls = gi * float(PAIRS_PER_GROUP) + pair
    rt = jnp.where(lane == 0, jnp.where(first_low, w1, w2), jnp.where(
        lane == 1, jnp.where(first_low, w2, w1), jnp.where(lane == 2, cls, 0.0)))
    rt_ref[...] = rt
    hf_ref[:, d:] = rt


def _delta_out_kernel(of_ref, ob_ref, z_ref, x_ref, gt_ref, on_ref, wo_ref, gf_ref, shf_ref, scf_ref, wr_ref,
                      x1_ref, hf_ref, rt_ref):
    o = of_ref[0].astype(F32) + ob_ref[0].astype(F32)
    z = z_ref[0].astype(F32)
    onorm = on_ref[...]
    parts = []
    for h in range(N_HEADS):
        oh = o[:, h * HEAD_DIM:(h + 1) * HEAD_DIM]
        parts.append(oh * lax.rsqrt(jnp.mean(oh * oh, axis=-1, keepdims=True) + EPS) * onorm)
    y = jnp.concatenate(parts, axis=1) * _silu(z)
    x1 = x_ref[0] + gt_ref[0] * _dot(y.astype(BF16), wo_ref[...])
    _ffn_pre(x1, gf_ref[...], shf_ref[0], scf_ref[0], wr_ref, x1_ref, hf_ref, rt_ref)


def _delta_out(o_f, o_b, z, x, gt, onorm, wo, gf, shf, scf, wr, tm):
    b, s, d = x.shape
    t = b * s
    nt = s // tm
    row = lambda bb, i: (bb, i, 0)
    vec = lambda bb, i: (bb, 0, 0)
    full = lambda bb, i: (0, 0)
    flat = lambda bb, i: (bb * nt + i, 0)
    return pl.pallas_call(
        _delta_out_kernel,
        grid=(b, nt),
        in_specs=[pl.BlockSpec((1, tm, d), row), pl.BlockSpec((1, tm, d), row),
                  pl.BlockSpec((1, tm, d), row), pl.BlockSpec((1, tm, d), row),
                  pl.BlockSpec((1, 1, d), vec), pl.BlockSpec((1, HEAD_DIM), full),
                  pl.BlockSpec((d, d), full), pl.BlockSpec((1, d), full),
                  pl.BlockSpec((1, 1, d), vec), pl.BlockSpec((1, 1, d), vec),
                  pl.BlockSpec((d, 2 * LANES), full)],
        out_specs=[pl.BlockSpec((tm, d), flat), pl.BlockSpec((tm, d + LANES), flat), pl.BlockSpec((tm, LANES), flat)],
        out_shape=[jax.ShapeDtypeStruct((t, d), F32), jax.ShapeDtypeStruct((t, d + LANES), F32),
                   jax.ShapeDtypeStruct((t, LANES), F32)],
        compiler_params=_params("parallel", "parallel"),
    )(o_f, o_b, z, x, gt, onorm, wo, gf, shf, scf, wr)


def _sconv_kernel(xm_ref, xp_ref, xn_ref, sh_ref, sc_ref, g_ref, win_ref, cw_ref, wo_ref, gt_ref,
                  gf_ref, shf_ref, scf_ref, wr_ref, x1_ref, hf_ref, rt_ref, *, tm, ck):
    i = pl.program_id(1)
    n = pl.num_programs(1)
    g, sh, sc = g_ref[...], sh_ref[0], sc_ref[0]
    xm = xm_ref[0]
    hm = _rms_mod(xm, g, sh, sc)
    hp = _rms_mod(xp_ref[0], g, sh, sc) * (i > 0).astype(F32)
    hn = _rms_mod(xn_ref[0], g, sh, sc) * (i < n - 1).astype(F32)
    hm16 = hm.astype(BF16)
    hext = jnp.concatenate([hp, hm, hn], axis=0).astype(BF16)
    d = xm.shape[1]
    assert (d // 2) % ck == 0
    col = lax.broadcasted_iota(I32, (tm, 1), 0) % GRID_W
    acc = jnp.zeros((tm, d), F32)
    for j in range(d // ck):
        c0 = j * ck
        gate_b = _dot(hm16, win_ref[:, c0:c0 + ck])
        cw = cw_ref[:, c0:c0 + ck]
        if c0 < d // 2:
            u = _dot(hm16, win_ref[:, d + c0:d + c0 + ck]) * _dot(hm16, win_ref[:, 2 * d + c0:2 * d + c0 + ck])
            left = jnp.where(col == 0, 0.0, pltpu.roll(u, 1, 0))
            right = jnp.where(col == GRID_W - 1, 0.0, pltpu.roll(u, tm - 1, 0))
            y = cw[0:1] * left + cw[1:2] * u + cw[2:3] * right
        else:
            u = _dot(hext, win_ref[:, d + c0:d + c0 + ck]) * _dot(hext, win_ref[:, 2 * d + c0:2 * d + c0 + ck])
            y = cw[0:1] * u[0:tm] + cw[1:2] * u[GRID_W:GRID_W + tm] + cw[2:3] * u[2 * GRID_W:2 * GRID_W + tm]
        acc = acc + _dot((gate_b * y).astype(BF16), wo_ref[c0:c0 + ck, :])
    x1 = xm + gt_ref[0] * acc
    _ffn_pre(x1, gf_ref[...], shf_ref[0], scf_ref[0], wr_ref, x1_ref, hf_ref, rt_ref)


def _sconv(x, sh, sc, gnorm, win, conv_w, wo, gt, gf, shf, scf, wr, tm):
    b, s, d = x.shape
    t = b * s
    nt = s // tm
    hb = tm // GRID_W
    last = s // GRID_W - 1
    row = lambda bb, i: (bb, i, 0)
    vec = lambda bb, i: (bb, 0, 0)
    full = lambda bb, i: (0, 0)
    flat = lambda bb, i: (bb * nt + i, 0)
    return pl.pallas_call(
        functools.partial(_sconv_kernel, tm=tm, ck=512),
        grid=(b, nt),
        in_specs=[pl.BlockSpec((1, tm, d), row),
                  pl.BlockSpec((1, GRID_W, d), lambda bb, i: (bb, jnp.maximum(i * hb - 1, 0), 0)),
                  pl.BlockSpec((1, GRID_W, d), lambda bb, i: (bb, jnp.minimum((i + 1) * hb, last), 0)),
                  pl.BlockSpec((1, 1, d), vec), pl.BlockSpec((1, 1, d), vec), pl.BlockSpec((1, d), full),
                  pl.BlockSpec((d, 3 * d), full), pl.BlockSpec((3, d), full), pl.BlockSpec((d, d), full),
                  pl.BlockSpec((1, 1, d), vec), pl.BlockSpec((1, d), full),
                  pl.BlockSpec((1, 1, d), vec), pl.BlockSpec((1, 1, d), vec),
                  pl.BlockSpec((d, 2 * LANES), full)],
        out_specs=[pl.BlockSpec((tm, d), flat), pl.BlockSpec((tm, d + LANES), flat), pl.BlockSpec((tm, LANES), flat)],
        out_shape=[jax.ShapeDtypeStruct((t, d), F32), jax.ShapeDtypeStruct((t, d + LANES), F32),
                   jax.ShapeDtypeStruct((t, LANES), F32)],
        compiler_params=_params("parallel", "parallel"),
    )(x, x, x, sh, sc, gnorm, win, conv_w, wo, gt, gf, shf, scf, wr)


def _rank_kernel(c_ref, dest_ref, ends_ref, cnt_ref, start_ref, carry_ref, *, tm, bm):
    p = pl.program_id(0)
    i = pl.program_id(1)
    onehot = lax.broadcasted_iota(I32, (LANES, tm), 0) == c_ref[0:1, :]
    ohf = onehot.astype(F32)
    per_class = jnp.broadcast_to(jnp.sum(ohf, axis=1, keepdims=True), (LANES, LANES))

    @pl.when(jnp.logical_and(p == 0, i == 0))
    def _():
        cnt_ref[...] = jnp.zeros_like(cnt_ref)

    @pl.when(p == 0)
    def _():
        cnt_ref[...] += per_class

    @pl.when(jnp.logical_and(p == 1, i == 0))
    def _():
        padded = ((cnt_ref[...].astype(I32) + (bm - 1)) & (-bm)).astype(F32)
        before = (lax.broadcasted_iota(I32, (LANES, LANES), 0) > lax.broadcasted_iota(I32, (LANES, LANES), 1))
        start = _dot(before.astype(F32), padded, HIGHEST)
        start_ref[...] = start
        lane1 = lax.broadcasted_iota(I32, (LANES, LANES), 1) == 1
        ends_ref[...] = jnp.where(lane1, cnt_ref[...].astype(I32), (start + padded).astype(I32))
        carry_ref[...] = jnp.zeros_like(carry_ref)

    @pl.when(p == 1)
    def _():
        earlier = (lax.broadcasted_iota(I32, (tm, tm), 0) < lax.broadcasted_iota(I32, (tm, tm), 1)).astype(BF16)
        slot = _dot(onehot.astype(BF16), earlier) + carry_ref[:, 0:1] + start_ref[:, 0:1]
        dest = jnp.sum(ohf * slot, axis=0, keepdims=True)
        dest_ref[...] = jnp.concatenate([dest, jnp.zeros((7, tm), F32)], axis=0).astype(I32)
        carry_ref[...] += per_class


def _rank(cls8, tm, bm):
    t = cls8.shape[1]
    return pl.pallas_call(
        functools.partial(_rank_kernel, tm=tm, bm=bm),
        grid=(2, t // tm),
        in_specs=[pl.BlockSpec((8, tm), lambda p, i: (0, i))],
        out_specs=[pl.BlockSpec((8, tm), lambda p, i: (0, i * p)), pl.BlockSpec((LANES, LANES), lambda p, i: (0, 0))],
        out_shape=[jax.ShapeDtypeStruct((8, t), I32), jax.ShapeDtypeStruct((LANES, LANES), I32)],
        scratch_shapes=[pltpu.VMEM((LANES, LANES), F32)] * 3,
        compiler_params=_params("arbitrary", "arbitrary"),
    )(cls8)


def _row_copy(src_ref, src_row, dst_ref, dst_row, sem):
    return pltpu.make_async_copy(src_ref.at[pl.ds(src_row, 1), :], dst_ref.at[pl.ds(dst_row, 1), :], sem)


def _dispatch_kernel(dest_ref, hf_ref, xb_in_ref, xb_ref, sem, *, td):
    del xb_in_ref

    def issue(r, carry):
        _row_copy(hf_ref, r, xb_ref, dest_ref[0, 0, r], sem).start()
        return carry

    for r in range(td):
        issue(r, 0)

    def drain(r, carry):
        _row_copy(hf_ref, 0, xb_ref, 0, sem).wait()
        return carry

    lax.fori_loop(0, td, drain, 0, unroll=8)


def _dispatch(dest3, hf, xb0, td):
    t, d = hf.shape
    cap = xb0.shape[0]
    return pl.pallas_call(
        functools.partial(_dispatch_kernel, td=td),
        grid=(t // td,),
        in_specs=[pl.BlockSpec((1, 1, td), lambda i: (i, 0, 0), memory_space=pltpu.SMEM),
                  pl.BlockSpec((td, d), lambda i: (i, 0)),
                  pl.BlockSpec(memory_space=pl.ANY)],
        out_specs=pl.BlockSpec(memory_space=pl.ANY),
        out_shape=jax.ShapeDtypeStruct((cap, d), hf.dtype),
        scratch_shapes=[pltpu.SemaphoreType.DMA],
        input_output_aliases={2: 0},
        compiler_params=_params("arbitrary"),
    )(dest3, hf, xb0)


def _expert_kernel(grp_ref, la_ref, lb_ref, bs_ref, nu_ref, nv_ref, x_ref, w1_hbm, w3_hbm, w2_hbm, y_ref,
                   res1, res3, res2, stage_up, stage_down, sems, *, layer):
    del bs_ref
    i = pl.program_id(0)
    d = y_ref.shape[1]
    grp = grp_ref[i]

    @pl.when(jnp.logical_or(i == 0, grp != grp_ref[jnp.maximum(i - 1, 0)]))
    def _():
        plan = []
        for e in range(EXPERTS_PER_GROUP):
            plan += [(w1_hbm, res1, stage_up.at[0], 0, e), (w3_hbm, res3, stage_up.at[1], 1, e),
                     (w2_hbm, res2, stage_down, 2, e)]

        def copy(step):
            src, _, stage, sem, e = plan[step]
            return pltpu.make_async_copy(src.at[layer, grp * EXPERTS_PER_GROUP + e], stage, sems.at[sem])

        copy(0).start()
        for step, (_, dst, stage, _, e) in enumerate(plan):
            if step + 1 < len(plan):
                copy(step + 1).start()
            copy(step).wait()
            dst[e] = stage[...].astype(BF16)

    bm = y_ref.shape[0]
    used = i < nu_ref[0]
    n_valid = nv_ref[i]

    def run(rows):
        x = x_ref[:rows, :d].astype(BF16)

        def swiglu(e):
            h = _silu(_dot(x, res1[e])) * _dot(x, res3[e])
            return _dot(h.astype(BF16), res2[e])

        y_ref[:rows] = (x_ref[:rows, d:d + 1] * swiglu(la_ref[i]) + x_ref[:rows, d + 1:d + 2] * swiglu(lb_ref[i]))
        if rows < bm:
            y_ref[rows:] = jnp.zeros((bm - rows, d), y_ref.dtype)

    sizes = (bm // 4, bm // 2, bm)
    for k, rows in enumerate(sizes):
        lower = sizes[k - 1] if k else -1
        fits = jnp.logical_and(n_valid > lower, n_valid <= rows) if rows < bm else n_valid > lower

        @pl.when(jnp.logical_and(used, fits))
        def _(rows=rows):
            run(rows)

    @pl.when(i >= nu_ref[0])
    def _():
        y_ref[...] = jnp.zeros_like(y_ref)


def _experts(block_grp, block_la, block_lb, block_src, n_used, block_rows, xb, w1, w3, w2, layer, bm):
    cap, dx = xb.shape
    _, _, d, de = w1.shape
    grid_spec = pltpu.PrefetchScalarGridSpec(
        num_scalar_prefetch=6,
        grid=(cap // bm,),
        in_specs=[pl.BlockSpec((bm, dx), lambda i, g, la, lb, bs, nu, nv: (bs[i], 0)),
                  pl.BlockSpec(memory_space=pl.ANY), pl.BlockSpec(memory_space=pl.ANY),
                  pl.BlockSpec(memory_space=pl.ANY)],
        out_specs=pl.BlockSpec((bm, d), lambda i, g, la, lb, bs, nu, nv: (i, 0)),
        scratch_shapes=[pltpu.VMEM((EXPERTS_PER_GROUP, d, de), BF16), pltpu.VMEM((EXPERTS_PER_GROUP, d, de), BF16),
                        pltpu.VMEM((EXPERTS_PER_GROUP, de, d), BF16), pltpu.VMEM((2, d, de), F32),
                        pltpu.VMEM((de, d), F32), pltpu.SemaphoreType.DMA((3,))],
    )
    return pl.pallas_call(
        functools.partial(_expert_kernel, layer=layer),
        grid_spec=grid_spec,
        out_shape=jax.ShapeDtypeStruct((cap, d), F32),
        compiler_params=_params("arbitrary"),
    )(block_grp, block_la, block_lb, block_src, n_used, block_rows, xb, w1, w3, w2)


def _combine_kernel(dest_ref, x1_ref, gt_ref, fn_ref, yb_ref, o_ref, buf, sems, *, tc, final):
    i = pl.program_id(0)
    n_tiles = pl.num_programs(0) - 1

    @pl.when(i < n_tiles)
    def _():
        slot = i % 2
        for r in range(tc):
            _row_copy(yb_ref, dest_ref[0, 0, r], buf.at[slot], r, sems.at[slot]).start()

    @pl.when(i > 0)
    def _():
        slot = (i - 1) % 2

        def drain(r, carry):
            _row_copy(yb_ref, 0, buf.at[slot], 0, sems.at[slot]).wait()
            return carry

        lax.fori_loop(0, tc, drain, 0, unroll=8)
        x2 = x1_ref[...] + gt_ref[0] * buf[slot]
        if final:
            ms = jnp.mean(x2 * x2, axis=-1, keepdims=True)
            x2 = x2 * lax.rsqrt(ms + EPS) * fn_ref[...]
        o_ref[...] = x2


def _combine(dest3, x1, gt, fnorm, yb, s, tc, final):
    t, d = x1.shape
    per_b = s // tc
    n_tiles = t // tc
    done = lambda i: jnp.maximum(i - 1, 0)
    return pl.pallas_call(
        functools.partial(_combine_kernel, tc=tc, final=final),
        grid=(n_tiles + 1,),
        in_specs=[pl.BlockSpec((1, 1, tc), lambda i: (jnp.minimum(i, n_tiles - 1), 0, 0), memory_space=pltpu.SMEM),
                  pl.BlockSpec((tc, d), lambda i: (done(i), 0)),
                  pl.BlockSpec((1, 1, d), lambda i: (done(i) // per_b, 0, 0)),
                  pl.BlockSpec((1, d), lambda i: (0, 0)),
                  pl.BlockSpec(memory_space=pl.ANY)],
        out_specs=pl.BlockSpec((tc, d), lambda i: (done(i), 0)),
        out_shape=jax.ShapeDtypeStruct((t, d), F32),
        scratch_shapes=[pltpu.VMEM((2, tc, d), F32), pltpu.SemaphoreType.DMA((2,))],
        compiler_params=_params("arbitrary"),
    )(dest3, x1, gt, fnorm, yb)


def _moe(hfx, rt, x1, gt, fnorm, w1, w3, w2, layer, xb0, s, final, bm, tr=512, td=1024):
    t = hfx.shape[0]
    cls8 = jnp.zeros((8, t), I32).at[0].set(rt[:, 2].astype(I32))
    dest8, ends2 = _rank(cls8, tr, bm)
    dest3 = dest8[0].reshape(t // td, 1, td)
    ends, counts = ends2[:N_CLASSES, 0], ends2[:N_CLASSES, 1]
    n_blocks = xb0.shape[0] // bm
    n_used = ends[-1] // bm
    block_src = jnp.maximum(jnp.minimum(jnp.arange(n_blocks, dtype=I32), n_used - 1), 0)
    block_cls = jnp.minimum(jnp.sum((ends[None, :] <= (block_src * bm)[:, None]).astype(I32), axis=1), N_CLASSES - 1)
    first_block = (ends - (counts + bm - 1) // bm * bm) // bm
    block_rows = jnp.clip(counts[block_cls] - (block_src - first_block[block_cls]) * bm, 0, bm).astype(I32)
    block_la = jnp.asarray(CLASS_LOCAL_A, I32)[block_cls]
    block_lb = jnp.asarray(CLASS_LOCAL_B, I32)[block_cls]
    xb = _dispatch(dest3, hfx, xb0, td)
    yb = _experts(block_cls // PAIRS_PER_GROUP, block_la, block_lb, block_src, n_used.reshape(1), block_rows, xb,
                  w1, w3, w2, layer, bm)
    return _combine(dest3, x1, gt, fnorm, yb, s, td, final), xb


def _router_weights(router_g, router_e):
    d = router_g.shape[0]
    w = jnp.zeros((d, LANES), F32).at[:, :N_GROUPS].set(router_g).at[:, N_GROUPS:N_GROUPS + N_EXPERTS].set(router_e)
    hi = w.astype(BF16)
    lo = (w - hi.astype(F32)).astype(BF16)
    return jnp.concatenate([hi, lo], axis=1)


def _gate_weights(w_in, a_log, dt_bias):
    d = w_in.shape[0]
    base = 4 * d
    wg = jnp.zeros((d, 2 * LANES), F32)
    gp = jnp.zeros((2, 2 * LANES), F32)
    for direction in range(2):
        o = direction * LANES
        wg = wg.at[:, o:o + N_HEADS].set(w_in[:, base + direction * N_HEADS:base + (direction + 1) * N_HEADS])
        wg = wg.at[:, o + N_HEADS:o + 2 * N_HEADS].set(
            w_in[:, base + (2 + direction) * N_HEADS:base + (3 + direction) * N_HEADS])
        wg = wg.at[:, o + 2 * N_HEADS:o + 3 * N_HEADS].set(
            w_in[:, base + (2 + direction) * N_HEADS:base + (3 + direction) * N_HEADS])
        for rep in (1, 2):
            gp = gp.at[0, o + rep * N_HEADS:o + (rep + 1) * N_HEADS].set(a_log[direction])
            gp = gp.at[1, o + rep * N_HEADS:o + (rep + 1) * N_HEADS].set(dt_bias[direction])
    return wg.astype(BF16), gp


def kernel(x, c, ctx, c_ctx, ada_w, ada_b, norm_mix, norm_ffn, w_in_a, conv_a, a_log_a, dt_bias_a, onorm_a, w_out_a,
           w_in_b, conv_b, w_out_b, router_g, router_e, w1, w3, w2, final_norm):
    b, s, d = x.shape
    chunk = SCAN_CHUNK
    tm = min(512, s)

    cond = jnp.zeros((16, d), F32).at[:b].set(c).at[b].set(c_ctx)
    mod = _ada_params(cond, ada_w, ada_b)

    def mods(layer, rows):
        m = mod[layer, rows]
        return [jnp.broadcast_to(m[:, None, k * d:(k + 1) * d], (b, 1, d)) for k in range(N_MOD)]

    sh_m, sc_m, gt_m, sh_f, sc_f, gt_f = mods(0, slice(0, b))
    csh_m, csc_m = mods(0, slice(b, b + 1))[:2]
    w_in = w_in_a[0]
    wqkv, wz = w_in[:, :3 * d].astype(BF16), w_in[:, 3 * d:4 * d].astype(BF16)
    wg, gp = _gate_weights(w_in, a_log_a[0], dt_bias_a[0])
    gn = norm_mix[0].reshape(1, d)
    qc, kc, vc, _, gc = _delta_in(ctx, csh_m, csc_m, gn, wqkv, conv_a[0], wz, wg, gp, tm)
    s_zero = jnp.zeros((b, 2, N_HEADS, HEAD_DIM, HEAD_DIM), F32)
    (s_ctx,) = _delta_scan(qc, kc, vc, gc, s_zero, chunk, SCAN_CHUNKS_PER_STEP, False)
    ql, kl, vl, zl, gl = _delta_in(x, sh_m, sc_m, gn, wqkv, conv_a[0], wz, wg, gp, tm)
    o_f, o_b, _ = _delta_scan(ql, kl, vl, gl, s_ctx, chunk, SCAN_CHUNKS_PER_STEP, True)
    wr = _router_weights(router_g[0], router_e[0])
    x1, hf, rt = _delta_out(o_f, o_b, zl, x, gt_m, onorm_a[0].reshape(1, HEAD_DIM), w_out_a[0].astype(BF16),
                            norm_ffn[0].reshape(1, d), sh_f, sc_f, wr, tm)
    fn = final_norm.reshape(1, d)
    xb0 = jnp.zeros(((-(-(b * s) // MOE_BLOCK) + N_CLASSES) * MOE_BLOCK, d + LANES), F32)
    x2, xb0 = _moe(hf, rt, x1, gt_f, fn, w1, w3, w2, 0, xb0, s, False, MOE_BLOCK)
    x2 = x2.reshape(b, s, d)

    sh_m, sc_m, gt_m, sh_f, sc_f, gt_f = mods(1, slice(0, b))
    wr = _router_weights(router_g[1], router_e[1])
    x1, hf, rt = _sconv(x2, sh_m, sc_m, norm_mix[1].reshape(1, d), w_in_b[0].astype(BF16), conv_b[0],
                        w_out_b[0].astype(BF16), gt_m, norm_ffn[1].reshape(1, d), sh_f, sc_f, wr, tm)
    out, _ = _moe(hf, rt, x1, gt_f, fn, w1, w3, w2, 1, xb0, s, True, MOE_BLOCK)
    return out.reshape(b, s, d)
```

```python
import functools

import jax
import jax.numpy as jnp
from jax import lax
from jax.experimental import pallas as pl
from jax.experimental.pallas import tpu as pltpu

F32 = jnp.float32
BF16 = jnp.bfloat16
I32 = jnp.int32
HIGHEST = lax.Precision.HIGHEST

EPS = 1e-6
N_HEADS = 8
HEAD_DIM = 128
GRID_W = 64
N_GROUPS = 4
EXPERTS_PER_GROUP = 8
N_EXPERTS = N_GROUPS * EXPERTS_PER_GROUP
PAIRS_PER_GROUP = EXPERTS_PER_GROUP * (EXPERTS_PER_GROUP - 1) // 2
N_CLASSES = N_GROUPS * PAIRS_PER_GROUP
_PAIRS = [(a, b) for a in range(EXPERTS_PER_GROUP) for b in range(a + 1, EXPERTS_PER_GROUP)]
CLASS_LOCAL_A = [a for _ in range(N_GROUPS) for a, _ in _PAIRS]
CLASS_LOCAL_B = [b for _ in range(N_GROUPS) for _, b in _PAIRS]
MOE_BLOCK = 256
N_MOD = 6
LANES = 128
SCAN_CHUNK = 64
SCAN_CHUNKS_PER_STEP = 4
SEQ_HALO = 8
VMEM_LIMIT = 48 * 1024 * 1024


def _params(*sem):
    return pltpu.CompilerParams(dimension_semantics=sem, vmem_limit_bytes=VMEM_LIMIT)


def _dot(a, b, precision=None):
    return jnp.dot(a, b, preferred_element_type=F32, precision=precision)


def _dot_nt(a, b, precision=None):
    return lax.dot_general(a, b, (((1,), (1,)), ((), ())), preferred_element_type=F32, precision=precision)


def _dot_tn(a, b, precision=None):
    return lax.dot_general(a, b, (((0,), (0,)), ((), ())), preferred_element_type=F32, precision=precision)


def _silu(x):
    return x * jax.nn.sigmoid(x)


def _rms_mod(x, g, shift, scale):
    ms = jnp.mean(x * x, axis=-1, keepdims=True)
    return x * lax.rsqrt(ms + EPS) * g * (1.0 + scale) + shift


def _ada_kernel(c_ref, w_ref, b_ref, o_ref):
    o_ref[0] = _dot(_silu(c_ref[...]), w_ref[0], HIGHEST) + b_ref[0]


def _ada_params(cond, ada_w, ada_b):
    depth, d, n = ada_w.shape
    tn = 512
    r = cond.shape[0]
    return pl.pallas_call(
        _ada_kernel,
        grid=(depth, n // tn),
        in_specs=[pl.BlockSpec((r, d), lambda l, j: (0, 0)),
                  pl.BlockSpec((1, d, tn), lambda l, j: (l, 0, j)),
                  pl.BlockSpec((1, 1, tn), lambda l, j: (l, 0, j))],
        out_specs=pl.BlockSpec((1, r, tn), lambda l, j: (l, 0, j)),
        out_shape=jax.ShapeDtypeStruct((depth, r, n), F32),
        compiler_params=_params("parallel", "parallel"),
    )(cond, ada_w, ada_b.reshape(depth, 1, n))


def _delta_in_kernel(xm_ref, xp_ref, xn_ref, sh_ref, sc_ref, g_ref, wqkv_ref, cw_ref, wz_ref, wg_ref, gp_ref, tri_ref,
                     q_ref, k_ref, v_ref, z_ref, gate_ref, *, tm, ck):
    i = pl.program_id(1)
    n = pl.num_programs(1)
    g, sh, sc = g_ref[...], sh_ref[0], sc_ref[0]
    hm = _rms_mod(xm_ref[0], g, sh, sc)
    hp = _rms_mod(xp_ref[0], g, sh, sc) * (i > 0).astype(F32)
    hn = _rms_mod(xn_ref[0], g, sh, sc) * (i < n - 1).astype(F32)
    hm16 = hm.astype(BF16)
    hext = jnp.concatenate([hp, hm, hn], axis=0).astype(BF16)
    rows = tm + 2 * SEQ_HALO
    d = hm.shape[1]
    outs = (q_ref, k_ref, v_ref)
    for j in range(3 * d // ck):
        p = _dot(hext, wqkv_ref[:, j * ck:(j + 1) * ck])
        cw = cw_ref[:, j * ck:(j + 1) * ck]
        y = cw[0:1] * pltpu.roll(p, 1, 0) + cw[1:2] * p + cw[2:3] * pltpu.roll(p, rows - 1, 0)
        y = _silu(y[SEQ_HALO:SEQ_HALO + tm])
        which, col0 = (j * ck) // d, (j * ck) % d
        for hh in range(ck // HEAD_DIM):
            yh = y[:, hh * HEAD_DIM:(hh + 1) * HEAD_DIM]
            if which < 2:
                inv = lax.rsqrt(jnp.sum(yh * yh, axis=-1, keepdims=True) + EPS)
                yh = yh * (inv * HEAD_DIM ** -0.5 if which == 0 else inv)
            c0 = col0 + hh * HEAD_DIM
            outs[which][0, :, c0:c0 + HEAD_DIM] = yh.astype(q_ref.dtype)
    z_ref[0] = _dot(hm16, wz_ref[...]).astype(z_ref.dtype)
    pg = _dot(hm16, wg_ref[...])
    lane = lax.broadcasted_iota(I32, pg.shape, 1) % LANES
    xg = pg + gp_ref[1:2]
    softplus = jnp.maximum(xg, 0.0) + jnp.log1p(jnp.exp(-jnp.abs(xg)))
    gval = -jnp.exp(gp_ref[0:1]) * softplus
    g_hi = gval.astype(BF16)
    r1 = gval - g_hi.astype(F32)
    g_mid = r1.astype(BF16)
    g_lo = (r1 - g_mid.astype(F32)).astype(BF16)
    cums = []
    for direction in range(2):
        sl = slice(direction * LANES, (direction + 1) * LANES)
        terms = jnp.concatenate([g_hi[:, sl], g_mid[:, sl], g_lo[:, sl]], axis=1)
        cs = _dot(tri_ref[direction], terms)
        cums.append(cs[:, :LANES] + cs[:, LANES:2 * LANES] + cs[:, 2 * LANES:])
    gcum = jnp.concatenate(cums, axis=1)
    gate_ref[0] = jnp.where(lane < N_HEADS, jax.nn.sigmoid(pg), jnp.where(
        lane < 2 * N_HEADS, gval, jnp.where(lane < 3 * N_HEADS, gcum, 0.0)))


def _delta_in(x, sh, sc, gnorm, wqkv, conv_w, wz, wg, gp, tm):
    b, s, d = x.shape
    tm = min(tm, s)
    hb = tm // SEQ_HALO
    last = s // SEQ_HALO - 1
    row = lambda bb, i: (bb, i, 0)
    vec = lambda bb, i: (bb, 0, 0)
    full = lambda bb, i: (0, 0)
    outs = [jax.ShapeDtypeStruct((b, s, d), BF16)] * 4 + [jax.ShapeDtypeStruct((b, s, 2 * LANES), F32)]
    ri = lax.broadcasted_iota(I32, (tm, tm), 0)
    ci = lax.broadcasted_iota(I32, (tm, tm), 1)
    same = (ri // SCAN_CHUNK) == (ci // SCAN_CHUNK)
    tri = jnp.stack([same & (ri >= ci), same & (ri <= ci)]).astype(BF16)
    return pl.pallas_call(
        functools.partial(_delta_in_kernel, tm=tm, ck=512),
        grid=(b, s // tm),
        in_specs=[pl.BlockSpec((1, tm, d), row),
                  pl.BlockSpec((1, SEQ_HALO, d), lambda bb, i: (bb, jnp.maximum(i * hb - 1, 0), 0)),
                  pl.BlockSpec((1, SEQ_HALO, d), lambda bb, i: (bb, jnp.minimum((i + 1) * hb, last), 0)),
                  pl.BlockSpec((1, 1, d), vec), pl.BlockSpec((1, 1, d), vec),
                  pl.BlockSpec((1, d), full),
                  pl.BlockSpec((d, 3 * d), full), pl.BlockSpec((3, 3 * d), full),
                  pl.BlockSpec((d, d), full), pl.BlockSpec((d, 2 * LANES), full),
                  pl.BlockSpec((2, 2 * LANES), full), pl.BlockSpec((2, tm, tm), lambda bb, i: (0, 0, 0))],
        out_specs=[pl.BlockSpec((1, tm, d), row)] * 4 + [pl.BlockSpec((1, tm, 2 * LANES), row)],
        out_shape=outs,
        compiler_params=_params("parallel", "parallel"),
    )(x, x, x, sh, sc, gnorm, wqkv, conv_w, wz, wg, gp, tri)


def _scan_kernel(qf_ref, kf_ref, vf_ref, gf_ref, qb_ref, kb_ref, vb_ref, gb_ref, s0_ref, *out_refs, c, cps,
                 with_output):
    if with_output:
        of_ref, ob_ref, s_ref = out_refs
    else:
        (s_ref,) = out_refs
        of_ref = ob_ref = None
    step = pl.program_id(1)

    @pl.when(step == 0)
    def _():
        s_ref[...] = s0_ref[...]

    in_refs = ((qf_ref, kf_ref, vf_ref, gf_ref, of_ref), (qb_ref, kb_ref, vb_ref, gb_ref, ob_ref))
    heads = range(N_HEADS)
    rb = 2 * c
    assert cps % 2 == 0 and rb == LANES
    n_pairs = cps // 2
    ri = lax.broadcasted_iota(I32, (rb, rb), 0)
    ci = lax.broadcasted_iota(I32, (rb, rb), 1)
    same = (ri // c) == (ci // c)
    incl = (same & (ri >= ci), same & (ri <= ci))
    strict = (same & (ri > ci), same & (ri < ci))
    rp = lax.broadcasted_iota(I32, (c, rb), 0)
    cp = lax.broadcasted_iota(I32, (c, rb), 1)
    eye_pack = (cp % c == rp).astype(F32)
    left = cp < c

    def lanes(h):
        return slice(h * HEAD_DIM, (h + 1) * HEAD_DIM)

    def pair_rows(dd, pp):
        r0 = (pp if dd == 0 else n_pairs - 1 - pp) * rb
        return slice(r0, r0 + rb)

    def chunk_rows(dd, j):
        r0 = (j if dd == 0 else 1 - j) * c
        return slice(r0, r0 + c)

    pairs = [(dd, pp, h) for dd in range(2) for pp in range(n_pairs) for h in heads]
    gates = {(dd, pp): in_refs[dd][3][0, pair_rows(dd, pp), :]
             for dd in range(2) for pp in range(n_pairs)}
    gates_t = {key: g.T for key, g in gates.items()}
    q16 = {(dd, pp, h): in_refs[dd][0][0, pair_rows(dd, pp), lanes(h)] for dd, pp, h in pairs}
    k16 = {(dd, pp, h): in_refs[dd][1][0, pair_rows(dd, pp), lanes(h)] for dd, pp, h in pairs}
    qf = {key: q16[key].astype(F32) for key in pairs}
    kf = {key: k16[key].astype(F32) for key in pairs}
    vf = {(dd, pp, h): in_refs[dd][2][0, pair_rows(dd, pp), lanes(h)].astype(F32) for dd, pp, h in pairs}
    cum = 2 * N_HEADS
    beta = {(dd, pp, h): gates[(dd, pp)][:, h:h + 1] for dd, pp, h in pairs}
    gcc = {(dd, pp, h): gates[(dd, pp)][:, cum + h:cum + h + 1] for dd, pp, h in pairs}
    gcr = {(dd, pp, h): gates_t[(dd, pp)][cum + h:cum + h + 1, :] for dd, pp, h in pairs}
    decay = {key: jnp.exp(jnp.where(incl[key[0]], gcc[key] - gcr[key], -jnp.inf)) for key in pairs}
    e = {key: jnp.exp(gcc[key]) for key in pairs}
    kb = {key: kf[key] * beta[key] for key in pairs}
    a = {key: _dot_nt(jnp.concatenate([kb[key].astype(BF16), q16[key]], axis=0), k16[key]) for key in pairs}
    lbd = {key: jnp.where(strict[key[0]], a[key][:rb] * decay[key], 0.0) for key in pairs}
    qk = {key: (a[key][rb:] * decay[key]).astype(BF16) for key in pairs}
    p = {key: eye_pack - (lbd[key][:c] + lbd[key][c:]) for key in pairs}
    l16 = {key: lbd[key].astype(BF16) for key in pairs}
    m = {key: _dot(l16[key], l16[key]) for key in pairs}
    span = 2
    while span < c:
        span *= 2
        m16 = {key: m[key].astype(BF16) for key in pairs}
        if span < c:
            pm = {key: _dot(jnp.concatenate([p[key], m[key]], axis=0).astype(BF16), m16[key]) for key in pairs}
            p = {key: p[key] + pm[key][:c] for key in pairs}
            m = {key: pm[key][c:] for key in pairs}
        else:
            p = {key: p[key] + _dot(p[key].astype(BF16), m16[key]) for key in pairs}
    t_bd = {key: jnp.concatenate([jnp.where(left, p[key], 0.0), jnp.where(left, 0.0, p[key])], axis=0).astype(BF16)
            for key in pairs}
    rhs = {key: jnp.concatenate([vf[key] * beta[key], kb[key] * e[key]], axis=1).astype(BF16) for key in pairs}
    sol = {key: _dot(t_bd[key], rhs[key]) for key in pairs}
    qe = {key: qf[key] * e[key] for key in pairs}

    state = {(dd, h): s_ref[0, dd, h] for dd in range(2) for h in heads}
    zeros = jnp.zeros((c, HEAD_DIM), BF16)
    for pp, j in [(pp, j) for pp in range(n_pairs) for j in range(2)]:
        now = [(dd, pp, h) for dd in range(2) for h in heads]
        rs = {key: chunk_rows(key[0], j) for key in now}
        tot = {}
        for dd, _, h in now:
            r = rs[(dd, pp, h)]
            edge = r.stop - 1 if dd == 0 else r.start
            tot[(dd, pp, h)] = gates[(dd, pp)][edge:edge + 1, cum + h:cum + h + 1]
        s16 = {key: state[(key[0], key[2])].astype(BF16) for key in now}
        ws = {key: _dot(jnp.concatenate([sol[key][rs[key], HEAD_DIM:], qe[key][rs[key]]], axis=0).astype(BF16),
                        s16[key]) for key in now}
        v16 = {key: (sol[key][rs[key], :HEAD_DIM] - ws[key][:c]).astype(BF16) for key in now}
        k_dec = {key: (kf[key][rs[key]] * jnp.exp(tot[key] - gcc[key][rs[key]])).astype(BF16) for key in now}
        for key in now:
            sk = (key[0], key[2])
            state[sk] = state[sk] * jnp.exp(tot[key]) + _dot_tn(k_dec[key], v16[key])
        if with_output:
            vpad = {key: jnp.concatenate([v16[key], zeros] if rs[key].start == 0 else [zeros, v16[key]], axis=0)
                    for key in now}
            o = {key: ws[key][c:] + _dot(qk[key][rs[key], :], vpad[key]) for key in now}
            for dd, _, h in now:
                r0 = pair_rows(dd, pp).start + rs[(dd, pp, h)].start
                in_refs[dd][4][0, r0:r0 + c, lanes(h)] = o[(dd, pp, h)].astype(of_ref.dtype)
    for dd in range(2):
        for h in heads:
            s_ref[0, dd, h] = state[(dd, h)]


def _delta_scan(q, k, v, gates, s0, c, cps, with_output):
    b, s, d = q.shape
    rb = c * cps
    nb = s // rb
    fwd = lambda bb, i: (bb, i, 0)
    bwd = lambda bb, i: (bb, nb - 1 - i, 0)
    state_spec = pl.BlockSpec((1, 2, N_HEADS, HEAD_DIM, HEAD_DIM), lambda bb, i: (bb, 0, 0, 0, 0))
    state_shape = jax.ShapeDtypeStruct((b, 2, N_HEADS, HEAD_DIM, HEAD_DIM), F32)
    out_specs, out_shape = [state_spec], [state_shape]
    if with_output:
        out_specs = [pl.BlockSpec((1, rb, d), fwd), pl.BlockSpec((1, rb, d), bwd)] + out_specs
        out_shape = [jax.ShapeDtypeStruct((b, s, d), BF16)] * 2 + out_shape
    return pl.pallas_call(
        functools.partial(_scan_kernel, c=c, cps=cps, with_output=with_output),
        grid=(b, nb),
        in_specs=([pl.BlockSpec((1, rb, d), fwd)] * 3 + [pl.BlockSpec((1, rb, LANES), fwd)]
                  + [pl.BlockSpec((1, rb, d), bwd)] * 3
                  + [pl.BlockSpec((1, rb, LANES), lambda bb, i: (bb, nb - 1 - i, 1)), state_spec]),
        out_specs=out_specs,
        out_shape=out_shape,
        compiler_params=_params("parallel", "arbitrary"),
    )(q, k, v, gates, q, k, v, gates, s0)


def _ffn_pre(x1, gf, shf, scf, wr_ref, x1_ref, hf_ref, rt_ref):
    x1_ref[...] = x1
    hf = _rms_mod(x1, gf, shf, scf)
    d = hf.shape[1]
    hf_ref[:, :d] = hf
    hi = hf.astype(BF16)
    lo = (hf - hi.astype(F32)).astype(BF16)
    a = _dot(hi, wr_ref[...])
    logits = a[:, :LANES] + a[:, LANES:] + _dot(lo, wr_ref[:, :LANES])
    lane = lax.broadcasted_iota(I32, logits.shape, 1)
    lanef = lane.astype(F32)
    big = float(LANES)
    gl = jnp.where(lane < N_GROUPS, logits, -jnp.inf)
    gmax = jnp.max(gl, axis=-1, keepdims=True)
    gi = jnp.min(jnp.where(gl == gmax, lanef, big), axis=-1, keepdims=True)
    pg_sel = 1.0 / jnp.sum(jnp.exp(gl - gmax), axis=-1, keepdims=True)
    rel = lanef - (N_GROUPS + EXPERTS_PER_GROUP * gi)
    el = jnp.where((rel >= 0.0) & (rel < float(EXPERTS_PER_GROUP)), logits, -jnp.inf)
    m1 = jnp.max(el, axis=-1, keepdims=True)
    i1 = jnp.min(jnp.where(el == m1, lanef, big), axis=-1, keepdims=True)
    el2 = jnp.where(lanef == i1, -jnp.inf, el)
    m2 = jnp.max(el2, axis=-1, keepdims=True)
    i2 = jnp.min(jnp.where(el2 == m2, lanef, big), axis=-1, keepdims=True)
    r2 = jnp.exp(m2 - m1)
    w1 = pg_sel / (1.0 + r2)
    w2 = pg_sel * r2 / (1.0 + r2)
    first_low = i1 < i2
    a_loc = jnp.minimum(i1, i2) - (N_GROUPS + EXPERTS_PER_GROUP * gi)
    b_loc = jnp.maximum(i1, i2) - (N_GROUPS + EXPERTS_PER_GROUP * gi)
    pair = a_loc * (EXPERTS_PER_GROUP - 1) - a_loc * (a_loc - 1.0) * 0.5 + (b_loc - a_loc - 1.0)
    cls = gi * float(PAIRS_PER_GROUP) + pair
    rt = jnp.where(lane == 0, jnp.where(first_low, w1, w2), jnp.where(
        lane == 1, jnp.where(first_low, w2, w1), jnp.where(lane == 2, cls, 0.0)))
    rt_ref[...] = rt
    hf_ref[:, d:] = rt


def _delta_out_kernel(of_ref, ob_ref, z_ref, x_ref, gt_ref, on_ref, wo_ref, gf_ref, shf_ref, scf_ref, wr_ref,
                      x1_ref, hf_ref, rt_ref):
    o = of_ref[0].astype(F32) + ob_ref[0].astype(F32)
    z = z_ref[0].astype(F32)
    onorm = on_ref[...]
    parts = []
    for h in range(N_HEADS):
        oh = o[:, h * HEAD_DIM:(h + 1) * HEAD_DIM]
        parts.append(oh * lax.rsqrt(jnp.mean(oh * oh, axis=-1, keepdims=True) + EPS) * onorm)
    y = jnp.concatenate(parts, axis=1) * _silu(z)
    x1 = x_ref[0] + gt_ref[0] * _dot(y.astype(BF16), wo_ref[...])
    _ffn_pre(x1, gf_ref[...], shf_ref[0], scf_ref[0], wr_ref, x1_ref, hf_ref, rt_ref)


def _delta_out(o_f, o_b, z, x, gt, onorm, wo, gf, shf, scf, wr, tm):
    b, s, d = x.shape
    t = b * s
    nt = s // tm
    row = lambda bb, i: (bb, i, 0)
    vec = lambda bb, i: (bb, 0, 0)
    full = lambda bb, i: (0, 0)
    flat = lambda bb, i: (bb * nt + i, 0)
    return pl.pallas_call(
        _delta_out_kernel,
        grid=(b, nt),
        in_specs=[pl.BlockSpec((1, tm, d), row), pl.BlockSpec((1, tm, d), row),
                  pl.BlockSpec((1, tm, d), row), pl.BlockSpec((1, tm, d), row),
                  pl.BlockSpec((1, 1, d), vec), pl.BlockSpec((1, HEAD_DIM), full),
                  pl.BlockSpec((d, d), full), pl.BlockSpec((1, d), full),
                  pl.BlockSpec((1, 1, d), vec), pl.BlockSpec((1, 1, d), vec),
                  pl.BlockSpec((d, 2 * LANES), full)],
        out_specs=[pl.BlockSpec((tm, d), flat), pl.BlockSpec((tm, d + LANES), flat), pl.BlockSpec((tm, LANES), flat)],
        out_shape=[jax.ShapeDtypeStruct((t, d), F32), jax.ShapeDtypeStruct((t, d + LANES), F32),
                   jax.ShapeDtypeStruct((t, LANES), F32)],
        compiler_params=_params("parallel", "parallel"),
    )(o_f, o_b, z, x, gt, onorm, wo, gf, shf, scf, wr)


def _sconv_kernel(xm_ref, xp_ref, xn_ref, sh_ref, sc_ref, g_ref, win_ref, cw_ref, wo_ref, gt_ref,
                  gf_ref, shf_ref, scf_ref, wr_ref, x1_ref, hf_ref, rt_ref, *, tm, ck):
    i = pl.program_id(1)
    n = pl.num_programs(1)
    g, sh, sc = g_ref[...], sh_ref[0], sc_ref[0]
    xm = xm_ref[0]
    hm = _rms_mod(xm, g, sh, sc)
    hp = _rms_mod(xp_ref[0], g, sh, sc) * (i > 0).astype(F32)
    hn = _rms_mod(xn_ref[0], g, sh, sc) * (i < n - 1).astype(F32)
    hm16 = hm.astype(BF16)
    hext = jnp.concatenate([hp, hm, hn], axis=0).astype(BF16)
    d = xm.shape[1]
    assert (d // 2) % ck == 0
    col = lax.broadcasted_iota(I32, (tm, 1), 0) % GRID_W
    acc = jnp.zeros((tm, d), F32)
    for j in range(d // ck):
        c0 = j * ck
        gate_b = _dot(hm16, win_ref[:, c0:c0 + ck])
        cw = cw_ref[:, c0:c0 + ck]
        if c0 < d // 2:
            u = _dot(hm16, win_ref[:, d + c0:d + c0 + ck]) * _dot(hm16, win_ref[:, 2 * d + c0:2 * d + c0 + ck])
            left = jnp.where(col == 0, 0.0, pltpu.roll(u, 1, 0))
            right = jnp.where(col == GRID_W - 1, 0.0, pltpu.roll(u, tm - 1, 0))
            y = cw[0:1] * left + cw[1:2] * u + cw[2:3] * right
        else:
            u = _dot(hext, win_ref[:, d + c0:d + c0 + ck]) * _dot(hext, win_ref[:, 2 * d + c0:2 * d + c0 + ck])
            y = cw[0:1] * u[0:tm] + cw[1:2] * u[GRID_W:GRID_W + tm] + cw[2:3] * u[2 * GRID_W:2 * GRID_W + tm]
        acc = acc + _dot((gate_b * y).astype(BF16), wo_ref[c0:c0 + ck, :])
    x1 = xm + gt_ref[0] * acc
    _ffn_pre(x1, gf_ref[...], shf_ref[0], scf_ref[0], wr_ref, x1_ref, hf_ref, rt_ref)


def _sconv(x, sh, sc, gnorm, win, conv_w, wo, gt, gf, shf, scf, wr, tm):
    b, s, d = x.shape
    t = b * s
    nt = s // tm
    hb = tm // GRID_W
    last = s // GRID_W - 1
    row = lambda bb, i: (bb, i, 0)
    vec = lambda bb, i: (bb, 0, 0)
    full = lambda bb, i: (0, 0)
    flat = lambda bb, i: (bb * nt + i, 0)
    return pl.pallas_call(
        functools.partial(_sconv_kernel, tm=tm, ck=512),
        grid=(b, nt),
        in_specs=[pl.BlockSpec((1, tm, d), row),
                  pl.BlockSpec((1, GRID_W, d), lambda bb, i: (bb, jnp.maximum(i * hb - 1, 0), 0)),
                  pl.BlockSpec((1, GRID_W, d), lambda bb, i: (bb, jnp.minimum((i + 1) * hb, last), 0)),
                  pl.BlockSpec((1, 1, d), vec), pl.BlockSpec((1, 1, d), vec), pl.BlockSpec((1, d), full),
                  pl.BlockSpec((d, 3 * d), full), pl.BlockSpec((3, d), full), pl.BlockSpec((d, d), full),
                  pl.BlockSpec((1, 1, d), vec), pl.BlockSpec((1, d), full),
                  pl.BlockSpec((1, 1, d), vec), pl.BlockSpec((1, 1, d), vec),
                  pl.BlockSpec((d, 2 * LANES), full)],
        out_specs=[pl.BlockSpec((tm, d), flat), pl.BlockSpec((tm, d + LANES), flat), pl.BlockSpec((tm, LANES), flat)],
        out_shape=[jax.ShapeDtypeStruct((t, d), F32), jax.ShapeDtypeStruct((t, d + LANES), F32),
                   jax.ShapeDtypeStruct((t, LANES), F32)],
        compiler_params=_params("parallel", "parallel"),
    )(x, x, x, sh, sc, gnorm, win, conv_w, wo, gt, gf, shf, scf, wr)


def _rank_kernel(c_ref, dest_ref, ends_ref, cnt_ref, start_ref, carry_ref, *, tm, bm):
    p = pl.program_id(0)
    i = pl.program_id(1)
    onehot = lax.broadcasted_iota(I32, (LANES, tm), 0) == c_ref[0:1, :]
    ohf = onehot.astype(F32)
    per_class = jnp.broadcast_to(jnp.sum(ohf, axis=1, keepdims=True), (LANES, LANES))

    @pl.when(jnp.logical_and(p == 0, i == 0))
    def _():
        cnt_ref[...] = jnp.zeros_like(cnt_ref)

    @pl.when(p == 0)
    def _():
        cnt_ref[...] += per_class

    @pl.when(jnp.logical_and(p == 1, i == 0))
    def _():
        padded = ((cnt_ref[...].astype(I32) + (bm - 1)) & (-bm)).astype(F32)
        before = (lax.broadcasted_iota(I32, (LANES, LANES), 0) > lax.broadcasted_iota(I32, (LANES, LANES), 1))
        start = _dot(before.astype(F32), padded, HIGHEST)
        start_ref[...] = start
        ends_ref[...] = (start + padded).astype(I32)
        carry_ref[...] = jnp.zeros_like(carry_ref)

    @pl.when(p == 1)
    def _():
        earlier = (lax.broadcasted_iota(I32, (tm, tm), 0) < lax.broadcasted_iota(I32, (tm, tm), 1)).astype(BF16)
        slot = _dot(onehot.astype(BF16), earlier) + carry_ref[:, 0:1] + start_ref[:, 0:1]
        dest = jnp.sum(ohf * slot, axis=0, keepdims=True)
        dest_ref[...] = jnp.concatenate([dest, jnp.zeros((7, tm), F32)], axis=0).astype(I32)
        carry_ref[...] += per_class


def _rank(cls8, tm, bm):
    t = cls8.shape[1]
    return pl.pallas_call(
        functools.partial(_rank_kernel, tm=tm, bm=bm),
        grid=(2, t // tm),
        in_specs=[pl.BlockSpec((8, tm), lambda p, i: (0, i))],
        out_specs=[pl.BlockSpec((8, tm), lambda p, i: (0, i * p)), pl.BlockSpec((LANES, LANES), lambda p, i: (0, 0))],
        out_shape=[jax.ShapeDtypeStruct((8, t), I32), jax.ShapeDtypeStruct((LANES, LANES), I32)],
        scratch_shapes=[pltpu.VMEM((LANES, LANES), F32)] * 3,
        compiler_params=_params("arbitrary", "arbitrary"),
    )(cls8)


def _row_copy(src_ref, src_row, dst_ref, dst_row, sem):
    return pltpu.make_async_copy(src_ref.at[pl.ds(src_row, 1), :], dst_ref.at[pl.ds(dst_row, 1), :], sem)


def _dispatch_kernel(dest_ref, hf_ref, xb_in_ref, xb_ref, sem, *, td):
    del xb_in_ref

    def issue(r, carry):
        _row_copy(hf_ref, r, xb_ref, dest_ref[0, 0, r], sem).start()
        return carry

    for r in range(td):
        issue(r, 0)

    def drain(r, carry):
        _row_copy(hf_ref, 0, xb_ref, 0, sem).wait()
        return carry

    lax.fori_loop(0, td, drain, 0, unroll=8)


def _dispatch(dest3, hf, xb0, td):
    t, d = hf.shape
    cap = xb0.shape[0]
    return pl.pallas_call(
        functools.partial(_dispatch_kernel, td=td),
        grid=(t // td,),
        in_specs=[pl.BlockSpec((1, 1, td), lambda i: (i, 0, 0), memory_space=pltpu.SMEM),
                  pl.BlockSpec((td, d), lambda i: (i, 0)),
                  pl.BlockSpec(memory_space=pl.ANY)],
        out_specs=pl.BlockSpec(memory_space=pl.ANY),
        out_shape=jax.ShapeDtypeStruct((cap, d), hf.dtype),
        scratch_shapes=[pltpu.SemaphoreType.DMA],
        input_output_aliases={2: 0},
        compiler_params=_params("arbitrary"),
    )(dest3, hf, xb0)


def _expert_kernel(grp_ref, la_ref, lb_ref, bs_ref, nu_ref, x_ref, w1_hbm, w3_hbm, w2_hbm, y_ref,
                   res1, res3, res2, stage_up, stage_down, sems, *, layer):
    del bs_ref
    i = pl.program_id(0)
    d = y_ref.shape[1]
    grp = grp_ref[i]

    @pl.when(jnp.logical_or(i == 0, grp != grp_ref[jnp.maximum(i - 1, 0)]))
    def _():
        plan = []
        for e in range(EXPERTS_PER_GROUP):
            plan += [(w1_hbm, res1, stage_up.at[0], 0, e), (w3_hbm, res3, stage_up.at[1], 1, e),
                     (w2_hbm, res2, stage_down, 2, e)]

        def copy(step):
            src, _, stage, sem, e = plan[step]
            return pltpu.make_async_copy(src.at[layer, grp * EXPERTS_PER_GROUP + e], stage, sems.at[sem])

        copy(0).start()
        for step, (_, dst, stage, _, e) in enumerate(plan):
            if step + 1 < len(plan):
                copy(step + 1).start()
            copy(step).wait()
            dst[e] = stage[...].astype(BF16)

    @pl.when(i < nu_ref[0])
    def _():
        x = x_ref[:, :d].astype(BF16)

        def swiglu(e):
            h = _silu(_dot(x, res1[e])) * _dot(x, res3[e])
            return _dot(h.astype(BF16), res2[e])

        y_ref[...] = x_ref[:, d:d + 1] * swiglu(la_ref[i]) + x_ref[:, d + 1:d + 2] * swiglu(lb_ref[i])

    @pl.when(i >= nu_ref[0])
    def _():
        y_ref[...] = jnp.zeros_like(y_ref)


def _experts(block_grp, block_la, block_lb, block_src, n_used, xb, w1, w3, w2, layer, bm):
    cap, dx = xb.shape
    _, _, d, de = w1.shape
    grid_spec = pltpu.PrefetchScalarGridSpec(
        num_scalar_prefetch=5,
        grid=(cap // bm,),
        in_specs=[pl.BlockSpec((bm, dx), lambda i, g, la, lb, bs, nu: (bs[i], 0)),
                  pl.BlockSpec(memory_space=pl.ANY), pl.BlockSpec(memory_space=pl.ANY),
                  pl.BlockSpec(memory_space=pl.ANY)],
        out_specs=pl.BlockSpec((bm, d), lambda i, g, la, lb, bs, nu: (i, 0)),
        scratch_shapes=[pltpu.VMEM((EXPERTS_PER_GROUP, d, de), BF16), pltpu.VMEM((EXPERTS_PER_GROUP, d, de), BF16),
                        pltpu.VMEM((EXPERTS_PER_GROUP, de, d), BF16), pltpu.VMEM((2, d, de), F32),
                        pltpu.VMEM((de, d), F32), pltpu.SemaphoreType.DMA((3,))],
    )
    return pl.pallas_call(
        functools.partial(_expert_kernel, layer=layer),
        grid_spec=grid_spec,
        out_shape=jax.ShapeDtypeStruct((cap, d), F32),
        compiler_params=_params("arbitrary"),
    )(block_grp, block_la, block_lb, block_src, n_used, xb, w1, w3, w2)


def _combine_kernel(dest_ref, x1_ref, gt_ref, fn_ref, yb_ref, o_ref, buf, sems, *, tc, final):
    i = pl.program_id(0)
    n_tiles = pl.num_programs(0) - 1

    @pl.when(i < n_tiles)
    def _():
        slot = i % 2
        for r in range(tc):
            _row_copy(yb_ref, dest_ref[0, 0, r], buf.at[slot], r, sems.at[slot]).start()

    @pl.when(i > 0)
    def _():
        slot = (i - 1) % 2

        def drain(r, carry):
            _row_copy(yb_ref, 0, buf.at[slot], 0, sems.at[slot]).wait()
            return carry

        lax.fori_loop(0, tc, drain, 0, unroll=8)
        x2 = x1_ref[...] + gt_ref[0] * buf[slot]
        if final:
            ms = jnp.mean(x2 * x2, axis=-1, keepdims=True)
            x2 = x2 * lax.rsqrt(ms + EPS) * fn_ref[...]
        o_ref[...] = x2


def _combine(dest3, x1, gt, fnorm, yb, s, tc, final):
    t, d = x1.shape
    per_b = s // tc
    n_tiles = t // tc
    done = lambda i: jnp.maximum(i - 1, 0)
    return pl.pallas_call(
        functools.partial(_combine_kernel, tc=tc, final=final),
        grid=(n_tiles + 1,),
        in_specs=[pl.BlockSpec((1, 1, tc), lambda i: (jnp.minimum(i, n_tiles - 1), 0, 0), memory_space=pltpu.SMEM),
                  pl.BlockSpec((tc, d), lambda i: (done(i), 0)),
                  pl.BlockSpec((1, 1, d), lambda i: (done(i) // per_b, 0, 0)),
                  pl.BlockSpec((1, d), lambda i: (0, 0)),
                  pl.BlockSpec(memory_space=pl.ANY)],
        out_specs=pl.BlockSpec((tc, d), lambda i: (done(i), 0)),
        out_shape=jax.ShapeDtypeStruct((t, d), F32),
        scratch_shapes=[pltpu.VMEM((2, tc, d), F32), pltpu.SemaphoreType.DMA((2,))],
        compiler_params=_params("arbitrary"),
    )(dest3, x1, gt, fnorm, yb)


def _moe(hfx, rt, x1, gt, fnorm, w1, w3, w2, layer, xb0, s, final, bm, tr=1024, td=2048, tc=1024):
    t = hfx.shape[0]
    cls8 = jnp.zeros((8, t), I32).at[0].set(rt[:, 2].astype(I32))
    dest8, ends2 = _rank(cls8, tr, bm)
    dest = dest8[0]
    ends = ends2[:N_CLASSES, 0]
    n_blocks = xb0.shape[0] // bm
    n_used = ends[-1] // bm
    block_src = jnp.maximum(jnp.minimum(jnp.arange(n_blocks, dtype=I32), n_used - 1), 0)
    block_cls = jnp.minimum(jnp.sum((ends[None, :] <= (block_src * bm)[:, None]).astype(I32), axis=1), N_CLASSES - 1)
    block_la = jnp.asarray(CLASS_LOCAL_A, I32)[block_cls]
    block_lb = jnp.asarray(CLASS_LOCAL_B, I32)[block_cls]
    xb = _dispatch(dest.reshape(t // td, 1, td), hfx, xb0, td)
    yb = _experts(block_cls // PAIRS_PER_GROUP, block_la, block_lb, block_src, n_used.reshape(1), xb, w1, w3, w2,
                  layer, bm)
    return _combine(dest.reshape(t // tc, 1, tc), x1, gt, fnorm, yb, s, tc, final), xb


def _router_weights(router_g, router_e):
    d = router_g.shape[0]
    w = jnp.zeros((d, LANES), F32).at[:, :N_GROUPS].set(router_g).at[:, N_GROUPS:N_GROUPS + N_EXPERTS].set(router_e)
    hi = w.astype(BF16)
    lo = (w - hi.astype(F32)).astype(BF16)
    return jnp.concatenate([hi, lo], axis=1)


def _gate_weights(w_in, a_log, dt_bias):
    d = w_in.shape[0]
    base = 4 * d
    wg = jnp.zeros((d, 2 * LANES), F32)
    gp = jnp.zeros((2, 2 * LANES), F32)
    for direction in range(2):
        o = direction * LANES
        wg = wg.at[:, o:o + N_HEADS].set(w_in[:, base + direction * N_HEADS:base + (direction + 1) * N_HEADS])
        wg = wg.at[:, o + N_HEADS:o + 2 * N_HEADS].set(
            w_in[:, base + (2 + direction) * N_HEADS:base + (3 + direction) * N_HEADS])
        wg = wg.at[:, o + 2 * N_HEADS:o + 3 * N_HEADS].set(
            w_in[:, base + (2 + direction) * N_HEADS:base + (3 + direction) * N_HEADS])
        for rep in (1, 2):
            gp = gp.at[0, o + rep * N_HEADS:o + (rep + 1) * N_HEADS].set(a_log[direction])
            gp = gp.at[1, o + rep * N_HEADS:o + (rep + 1) * N_HEADS].set(dt_bias[direction])
    return wg.astype(BF16), gp


def kernel(x, c, ctx, c_ctx, ada_w, ada_b, norm_mix, norm_ffn, w_in_a, conv_a, a_log_a, dt_bias_a, onorm_a, w_out_a,
           w_in_b, conv_b, w_out_b, router_g, router_e, w1, w3, w2, final_norm):
    b, s, d = x.shape
    chunk = SCAN_CHUNK
    tm = min(512, s)

    cond = jnp.zeros((16, d), F32).at[:b].set(c).at[b].set(c_ctx)
    mod = _ada_params(cond, ada_w, ada_b)

    def mods(layer, rows):
        m = mod[layer, rows]
        return [jnp.broadcast_to(m[:, None, k * d:(k + 1) * d], (b, 1, d)) for k in range(N_MOD)]

    sh_m, sc_m, gt_m, sh_f, sc_f, gt_f = mods(0, slice(0, b))
    csh_m, csc_m = mods(0, slice(b, b + 1))[:2]
    w_in = w_in_a[0]
    wqkv, wz = w_in[:, :3 * d].astype(BF16), w_in[:, 3 * d:4 * d].astype(BF16)
    wg, gp = _gate_weights(w_in, a_log_a[0], dt_bias_a[0])
    gn = norm_mix[0].reshape(1, d)
    qc, kc, vc, _, gc = _delta_in(ctx, csh_m, csc_m, gn, wqkv, conv_a[0], wz, wg, gp, tm)
    s_zero = jnp.zeros((b, 2, N_HEADS, HEAD_DIM, HEAD_DIM), F32)
    (s_ctx,) = _delta_scan(qc, kc, vc, gc, s_zero, chunk, SCAN_CHUNKS_PER_STEP, False)
    ql, kl, vl, zl, gl = _delta_in(x, sh_m, sc_m, gn, wqkv, conv_a[0], wz, wg, gp, tm)
    o_f, o_b, _ = _delta_scan(ql, kl, vl, gl, s_ctx, chunk, SCAN_CHUNKS_PER_STEP, True)
    wr = _router_weights(router_g[0], router_e[0])
    x1, hf, rt = _delta_out(o_f, o_b, zl, x, gt_m, onorm_a[0].reshape(1, HEAD_DIM), w_out_a[0].astype(BF16),
                            norm_ffn[0].reshape(1, d), sh_f, sc_f, wr, tm)
    fn = final_norm.reshape(1, d)
    xb0 = jnp.zeros(((-(-(b * s) // MOE_BLOCK) + N_CLASSES) * MOE_BLOCK, d + LANES), F32)
    x2, xb0 = _moe(hf, rt, x1, gt_f, fn, w1, w3, w2, 0, xb0, s, False, MOE_BLOCK)
    x2 = x2.reshape(b, s, d)

    sh_m, sc_m, gt_m, sh_f, sc_f, gt_f = mods(1, slice(0, b))
    wr = _router_weights(router_g[1], router_e[1])
    x1, hf, rt = _sconv(x2, sh_m, sc_m, norm_mix[1].reshape(1, d), w_in_b[0].astype(BF16), conv_b[0],
                        w_out_b[0].astype(BF16), gt_m, norm_ffn[1].reshape(1, d), sh_f, sc_f, wr, tm)
    out, _ = _moe(hf, rt, x1, gt_f, fn, w1, w3, w2, 1, xb0, s, True, MOE_BLOCK)
    return out.reshape(b, s, d)
```
